```python
import math
import jax
import jax.numpy as jnp
from jax import lax
import numpy as np

D_MODEL = 2048
BATCH = 4
SEQ = 2048
DEPTH = 2
DEC_BATCH = 8
DEC_SEQ = 64
PAST_LEN = 2048

CHUNK = 64
N_EVEN = (DEPTH + 1) // 2
N_ODD = DEPTH // 2
HALF = D_MODEL // 2
NORM_EPS = 1e-6
ROPE_BASE = 10000.0
RET_HEADS = 4
RET_DK = HALF // RET_HEADS
RET_DV = HALF // RET_HEADS
SB_HEADS = 8
SB_DH = HALF // SB_HEADS
SB_QBLOCK = 128
LRU_WIDTH = HALF
LRU_BLOCKS = 8
LRU_BW = LRU_WIDTH // LRU_BLOCKS
CONV_W = 4
LRU_C = 8.0
CB_HEADS = 8
CB_DH = HALF // CB_HEADS
CB_PREV_CHUNKS = 8
CB_PAST = CB_PREV_CHUNKS * CHUNK
CB_BAND = (CB_PREV_CHUNKS + 1) * CHUNK
CB_MAX_REL = 128
NEG_BIG = -1e30
D_FF = 5632
N_EXPERTS = 8
TOP_K = 2

kernel_name = 'hybrid_stream_encoder_step'


def rmsnorm(x, g):
    xf = x.astype(jnp.float32)
    y = xf * lax.rsqrt(jnp.mean(xf * xf, axis=-1, keepdims=True) + NORM_EPS)
    return (y * g.astype(jnp.float32)).astype(x.dtype)


def rotary(x, pos):
    half = x.shape[-1] // 2
    inv = ROPE_BASE ** (-jnp.arange(half, dtype=jnp.float32) / half)
    ang = pos.astype(jnp.float32)[:, None] * inv[None, :]
    cos = jnp.cos(ang)[None, :, None, :]
    sin = jnp.sin(ang)[None, :, None, :]
    xf = x.astype(jnp.float32)
    x1, x2 = xf[..., :half], xf[..., half:]
    return jnp.concatenate([x1 * cos - x2 * sin, x1 * sin + x2 * cos], axis=-1).astype(x.dtype)


def swiglu(h, w1, w3, w2):
    return (jax.nn.silu(h @ w1) * (h @ w3)) @ w2


def retention(q, k, v, s0):
    B, L, H, dk = q.shape
    dv = v.shape[-1]
    c = CHUNK if L % CHUNK == 0 else L
    n = L // c
    f32 = jnp.float32
    log_g = jnp.log1p(-jnp.exp2(-5.0 - jnp.arange(H, dtype=f32)))
    idx = jnp.arange(c, dtype=f32)
    diff = idx[:, None] - idx[None, :]
    decay = jnp.where(diff >= 0, jnp.exp(log_g[:, None, None] * jnp.maximum(diff, 0.0)), 0.0)
    q_dec = jnp.exp(log_g[:, None] * (idx + 1.0))[:, :, None]
    k_dec = jnp.exp(log_g[:, None] * (c - 1.0 - idx))[:, :, None]
    blk_dec = jnp.exp(log_g * c)[:, None, None]

    def to_chunks(t):
        return t.astype(f32).reshape(B, n, c, H, t.shape[-1]).transpose(1, 0, 3, 2, 4)

    def step(s, inp):
        qi, ki, vi = inp
        scores = jnp.einsum('bhid,bhjd->bhij', qi, ki) * decay
        o = jnp.einsum('bhij,bhjv->bhiv', scores, vi) + jnp.einsum('bhid,bhdv->bhiv', qi * q_dec, s)
        s = s * blk_dec + jnp.einsum('bhjd,bhjv->bhdv', ki * k_dec, vi)
        return s, o

    s, o = lax.scan(step, s0.astype(f32), (to_chunks(q), to_chunks(k), to_chunks(v)))
    o = o.transpose(1, 0, 3, 2, 4).reshape(B, L, H, dv)
    return o.astype(q.dtype), s.astype(s0.dtype)


def _sb_block(q, k, v, q_pos, k_pos):
    z = jnp.einsum('bqhd,bkhd->bhqk', q, k).astype(jnp.float32) * (q.shape[-1] ** -0.5)
    causal = k_pos[None, :] < q_pos[:, None]
    log_keep = jnp.where(causal, jax.nn.log_sigmoid(-z), 0.0)
    after = lax.cumsum(log_keep, axis=3, reverse=True) - log_keep
    w = jnp.where(causal, jnp.exp(jax.nn.log_sigmoid(z) + after), 0.0)
    return jnp.einsum('bhqk,bkhd->bqhd', w, v.astype(jnp.float32)).astype(q.dtype)


def stick_breaking(q, k, v):
    B, L, H, d = q.shape
    past = k.shape[1] - L
    k_pos = jnp.arange(past + L)

    def block(qb, start):
        q_pos = past + start + jnp.arange(qb.shape[1])
        return _sb_block(qb, k, v, q_pos, k_pos)

    if L <= SB_QBLOCK:
        return block(q, 0)
    nb = L // SB_QBLOCK
    qbs = q.reshape(B, nb, SB_QBLOCK, H, d).transpose(1, 0, 2, 3, 4)
    starts = jnp.arange(nb) * SB_QBLOCK
    out = lax.map(lambda a: block(a[0], a[1]), (qbs, starts))
    return out.transpose(1, 0, 2, 3, 4).reshape(B, L, H, d)


def chunk_band_attention(q, k, v, bias_table):
    B, L, H, d = q.shape
    past = k.shape[1] - L
    nc = -(-L // CHUNK)
    lq = nc * CHUNK
    left = CB_PAST - past
    right = lq - L
    kp = jnp.pad(k, ((0, 0), (left, right), (0, 0), (0, 0)))
    vp = jnp.pad(v, ((0, 0), (left, right), (0, 0), (0, 0)))
    qc = jnp.pad(q, ((0, 0), (0, right), (0, 0), (0, 0))).reshape(B, nc, CHUNK, H, d)
    band = jnp.arange(nc)[:, None] + jnp.arange(CB_PREV_CHUNKS + 1)[None, :]
    kb = kp.reshape(B, nc + CB_PREV_CHUNKS, CHUNK, H, d)[:, band].reshape(B, nc, CB_BAND, H, d)
    vb = vp.reshape(B, nc + CB_PREV_CHUNKS, CHUNK, H, d)[:, band].reshape(B, nc, CB_BAND, H, d)
    s = jnp.einsum('bnqhd,bnkhd->bnhqk', qc, kb).astype(jnp.float32) * (d ** -0.5)
    rel = (CB_PAST + jnp.arange(CHUNK))[:, None] - jnp.arange(CB_BAND)[None, :]
    bias = bias_table.astype(jnp.float32)[:, jnp.clip(rel, -CB_MAX_REL, CB_MAX_REL) + CB_MAX_REL]
    key_idx = jnp.arange(nc)[:, None] * CHUNK + jnp.arange(CB_BAND)[None, :]
    valid = (key_idx >= left) & (key_idx < left + past + L)
    s = jnp.where(valid[None, :, None, None, :], s + bias[None, None], NEG_BIG)
    p = jax.nn.softmax(s, axis=-1)
    o = jnp.einsum('bnhqk,bnkhd->bnqhd', p, vb.astype(jnp.float32)).reshape(B, lq, H, d)[:, :L]
    return o.astype(q.dtype)


def causal_dwconv(x, buf, w, b):
    xp = jnp.concatenate([buf.astype(x.dtype), x], axis=1)
    y = lax.conv_general_dilated(xp, w[:, None, :].astype(x.dtype), window_strides=(1,), padding='VALID',
                                 dimension_numbers=('NWC', 'WIO', 'NWC'), feature_group_count=x.shape[-1])
    return y + b.astype(x.dtype), xp[:, -(CONV_W - 1):]


def rg_lru(x, h0, wr, br, wi, bi, lam):
    B, L, W = x.shape
    f32 = jnp.float32
    xf = x.astype(f32)
    xb = xf.reshape(B, L, LRU_BLOCKS, LRU_BW)
    r = jax.nn.sigmoid(jnp.einsum('blnc,ncd->blnd', xb, wr.astype(f32)).reshape(B, L, W) + br.astype(f32))
    i = jax.nn.sigmoid(jnp.einsum('blnc,ncd->blnd', xb, wi.astype(f32)).reshape(B, L, W) + bi.astype(f32))
    log_a = -LRU_C * r * jax.nn.softplus(-lam.astype(f32))
    a = jnp.exp(log_a)
    u = jnp.sqrt(-jnp.expm1(2.0 * log_a)) * (i * xf)
    u = u.at[:, 0].add(a[:, 0] * h0.astype(f32))

    def combine(e1, e2):
        a1, b1 = e1
        a2, b2 = e2
        return a1 * a2, a2 * b1 + b2

    _, hs = lax.associative_scan(combine, (a, u), axis=1)
    return hs.astype(x.dtype), hs[:, -1].astype(h0.dtype)


def moe_swiglu(h, router, w1, w3, w2):
    logits = jnp.einsum('bld,de->ble', h, router).astype(jnp.float32)
    top_val, top_idx = lax.top_k(logits, TOP_K)
    top_w = jax.nn.softmax(top_val, axis=-1)
    gate = jnp.einsum('blk,blke->ble', top_w, jax.nn.one_hot(top_idx, N_EXPERTS, dtype=jnp.float32)).astype(h.dtype)
    out = jnp.zeros_like(h)
    for e in range(N_EXPERTS):
        out = out + gate[..., e:e + 1] * swiglu(h, w1[e], w3[e], w2[e])
    return out


def even_layer(x, ret_s0, sb_k_past, sb_v_past, norm_mix, w_in, ret_gn, sb_qn, sb_kn, w_out,
               norm_ffn, w1, w3, w2):
    B, L, _ = x.shape
    pos = sb_k_past.shape[1] + jnp.arange(L)
    h = rmsnorm(x, norm_mix)
    qa, ka, va, ga, qb, kb, vb = jnp.split(h @ w_in, 7, axis=-1)
    qa = rotary(qa.reshape(B, L, RET_HEADS, RET_DK), pos) * (RET_DK ** -0.5)
    ka = rotary(ka.reshape(B, L, RET_HEADS, RET_DK), pos)
    ret, ret_s = retention(qa, ka, va.reshape(B, L, RET_HEADS, RET_DV), ret_s0)
    a_out = jax.nn.silu(ga) * rmsnorm(ret, ret_gn).reshape(B, L, HALF)
    qb = rmsnorm(qb.reshape(B, L, SB_HEADS, SB_DH), sb_qn)
    kb = rmsnorm(kb.reshape(B, L, SB_HEADS, SB_DH), sb_kn)
    vb = vb.reshape(B, L, SB_HEADS, SB_DH)
    b_out = stick_breaking(qb, jnp.concatenate([sb_k_past.astype(kb.dtype), kb], axis=1),
                           jnp.concatenate([sb_v_past.astype(vb.dtype), vb], axis=1))
    x = x + jnp.concatenate([a_out, b_out.reshape(B, L, HALF)], axis=-1) @ w_out
    x = x + swiglu(rmsnorm(x, norm_ffn), w1, w3, w2)
    return x, ret_s, kb, vb


def odd_layer(x, conv_buf, lru_h0, cb_k_past, cb_v_past, cb_keep, norm_mix, w_in, conv_w, conv_b,
              lru_wr, lru_br, lru_wi, lru_bi, lru_lam, cb_qn, cb_kn, cb_bias, w_out, norm_ffn,
              router, w1, w3, w2):
    B, L, _ = x.shape
    h = rmsnorm(x, norm_mix)
    gc, xc, qd, kd, vd = jnp.split(h @ w_in, 5, axis=-1)
    xc, conv_new = causal_dwconv(xc, conv_buf, conv_w, conv_b)
    hc, h_last = rg_lru(xc, lru_h0, lru_wr, lru_br, lru_wi, lru_bi, lru_lam)
    c_out = jax.nn.gelu(gc) * hc
    qd = rmsnorm(qd.reshape(B, L, CB_HEADS, CB_DH), cb_qn)
    kd = rmsnorm(kd.reshape(B, L, CB_HEADS, CB_DH), cb_kn)
    vd = vd.reshape(B, L, CB_HEADS, CB_DH)
    d_out = chunk_band_attention(qd, jnp.concatenate([cb_k_past.astype(kd.dtype), kd], axis=1),
                                 jnp.concatenate([cb_v_past.astype(vd.dtype), vd], axis=1), cb_bias)
    x = x + jnp.concatenate([c_out, d_out.reshape(B, L, HALF)], axis=-1) @ w_out
    x = x + moe_swiglu(rmsnorm(x, norm_ffn), router, w1, w3, w2)
    return x, conv_new, h_last, kd[:, L - cb_keep:], vd[:, L - cb_keep:]


def run_trunk(x, ret_s, sb_k, sb_v, conv_s, lru_s, cb_k, cb_v, cb_keep, ew, ow):
    ret_o, sbk_o, sbv_o, conv_o, lru_o, cbk_o, cbv_o = [], [], [], [], [], [], []
    for layer in range(DEPTH):
        i = layer // 2
        if layer % 2 == 0:
            x, s, k, v = even_layer(x, ret_s[i], sb_k[i], sb_v[i], *[w[i] for w in ew])
            ret_o.append(s)
            sbk_o.append(k)
            sbv_o.append(v)
        else:
            x, c, hl, k, v = odd_layer(x, conv_s[i], lru_s[i], cb_k[i], cb_v[i], cb_keep, *[w[i] for w in ow])
            conv_o.append(c)
            lru_o.append(hl)
            cbk_o.append(k)
            cbv_o.append(v)
    return (x, jnp.stack(ret_o), jnp.stack(sbk_o), jnp.stack(sbv_o), jnp.stack(conv_o),
            jnp.stack(lru_o), jnp.stack(cbk_o), jnp.stack(cbv_o))


def setup_inputs(seed: int = 0) -> dict:
    key = jax.random.key(seed)
    ks = iter(jax.random.split(key, 40))
    f32 = jnp.float32

    def nrm(shape, scale):
        return jax.random.normal(next(ks), shape, f32) * scale

    def gain(shape):
        return 1.0 + 0.01 * jax.random.normal(next(ks), shape, f32)

    cb_len = min(CB_PAST, PAST_LEN)
    d = D_MODEL
    inp = {}
    inp['x_prompt'] = nrm((BATCH, SEQ, d), 1.0)
    inp['x_sample'] = nrm((DEC_BATCH, DEC_SEQ, d), 1.0)
    inp['state_ret'] = nrm((N_EVEN, DEC_BATCH, RET_HEADS, RET_DK, RET_DV), 1.0)
    inp['cache_sb_k'] = nrm((N_EVEN, DEC_BATCH, PAST_LEN, SB_HEADS, SB_DH), 1.0)
    inp['cache_sb_v'] = nrm((N_EVEN, DEC_BATCH, PAST_LEN, SB_HEADS, SB_DH), 1.0)
    inp['state_conv'] = nrm((N_ODD, DEC_BATCH, CONV_W - 1, LRU_WIDTH), 1.0)
    inp['state_lru'] = nrm((N_ODD, DEC_BATCH, LRU_WIDTH), 0.5)
    inp['cache_cb_k'] = nrm((N_ODD, DEC_BATCH, cb_len, CB_HEADS, CB_DH), 1.0)
    inp['cache_cb_v'] = nrm((N_ODD, DEC_BATCH, cb_len, CB_HEADS, CB_DH), 1.0)
    inp['e_norm_mix'] = gain((N_EVEN, d))
    inp['e_w_in'] = nrm((N_EVEN, d, 7 * HALF), d ** -0.5)
    inp['e_ret_gn'] = gain((N_EVEN, RET_HEADS, RET_DV))
    inp['e_sb_qn'] = gain((N_EVEN, SB_DH))
    inp['e_sb_kn'] = gain((N_EVEN, SB_DH))
    inp['e_w_out'] = nrm((N_EVEN, 2 * HALF, d), (2 * HALF) ** -0.5)
    inp['e_norm_ffn'] = gain((N_EVEN, d))
    inp['e_w1'] = nrm((N_EVEN, d, D_FF), d ** -0.5)
    inp['e_w3'] = nrm((N_EVEN, d, D_FF), d ** -0.5)
    inp['e_w2'] = nrm((N_EVEN, D_FF, d), D_FF ** -0.5)
    inp['o_norm_mix'] = gain((N_ODD, d))
    inp['o_w_in'] = nrm((N_ODD, d, 5 * HALF), d ** -0.5)
    inp['o_conv_w'] = nrm((N_ODD, CONV_W, LRU_WIDTH), CONV_W ** -0.5)
    inp['o_conv_b'] = nrm((N_ODD, LRU_WIDTH), 0.01)
    inp['o_lru_wr'] = nrm((N_ODD, LRU_BLOCKS, LRU_BW, LRU_BW), LRU_BW ** -0.5)
    inp['o_lru_br'] = nrm((N_ODD, LRU_WIDTH), 0.01)
    inp['o_lru_wi'] = nrm((N_ODD, LRU_BLOCKS, LRU_BW, LRU_BW), LRU_BW ** -0.5)
    inp['o_lru_bi'] = nrm((N_ODD, LRU_WIDTH), 0.01)
    u = jax.random.uniform(next(ks), (N_ODD, LRU_WIDTH), f32, minval=0.9, maxval=0.999)
    p = u ** (1.0 / LRU_C)
    inp['o_lru_lam'] = jnp.log(p) - jnp.log1p(-p)
    inp['o_cb_qn'] = gain((N_ODD, CB_DH))
    inp['o_cb_kn'] = gain((N_ODD, CB_DH))
    inp['o_cb_bias'] = nrm((N_ODD, CB_HEADS, 2 * CB_MAX_REL + 1), 0.5)
    inp['o_w_out'] = nrm((N_ODD, 2 * HALF, d), (2 * HALF) ** -0.5)
    inp['o_norm_ffn'] = gain((N_ODD, d))
    inp['o_router'] = nrm((N_ODD, d, N_EXPERTS), d ** -0.5)
    inp['o_w1'] = nrm((N_ODD, N_EXPERTS, d, D_FF), d ** -0.5)
    inp['o_w3'] = nrm((N_ODD, N_EXPERTS, d, D_FF), d ** -0.5)
    inp['o_w2'] = nrm((N_ODD, N_EXPERTS, D_FF, d), D_FF ** -0.5)
    return inp


def reference(x_prompt, x_sample, state_ret, cache_sb_k, cache_sb_v, state_conv, state_lru,
              cache_cb_k, cache_cb_v,
              e_norm_mix, e_w_in, e_ret_gn, e_sb_qn, e_sb_kn, e_w_out, e_norm_ffn, e_w1, e_w3, e_w2,
              o_norm_mix, o_w_in, o_conv_w, o_conv_b, o_lru_wr, o_lru_br, o_lru_wi, o_lru_bi, o_lru_lam,
              o_cb_qn, o_cb_kn, o_cb_bias, o_w_out, o_norm_ffn, o_router, o_w1, o_w3, o_w2):
    ew = (e_norm_mix, e_w_in, e_ret_gn, e_sb_qn, e_sb_kn, e_w_out, e_norm_ffn, e_w1, e_w3, e_w2)
    ow = (o_norm_mix, o_w_in, o_conv_w, o_conv_b, o_lru_wr, o_lru_br, o_lru_wi, o_lru_bi, o_lru_lam,
          o_cb_qn, o_cb_kn, o_cb_bias, o_w_out, o_norm_ffn, o_router, o_w1, o_w3, o_w2)
    dt = x_prompt.dtype
    bp, lp = x_prompt.shape[0], x_prompt.shape[1]
    ret0 = jnp.zeros((N_EVEN, bp, RET_HEADS, RET_DK, RET_DV), dt)
    sb0 = jnp.zeros((N_EVEN, bp, 0, SB_HEADS, SB_DH), dt)
    conv0 = jnp.zeros((N_ODD, bp, CONV_W - 1, LRU_WIDTH), dt)
    lru0 = jnp.zeros((N_ODD, bp, LRU_WIDTH), dt)
    cb0 = jnp.zeros((N_ODD, bp, 0, CB_HEADS, CB_DH), dt)
    y_prompt, p_ret, p_sb_k, p_sb_v, p_conv, p_lru, p_cb_k, p_cb_v = run_trunk(
        x_prompt, ret0, sb0, sb0, conv0, lru0, cb0, cb0, min(CB_PAST, lp), ew, ow)
    y_sample, s_ret, s_sb_k, s_sb_v, s_conv, s_lru, s_cb_k, s_cb_v = run_trunk(
        x_sample, state_ret, cache_sb_k, cache_sb_v, state_conv, state_lru, cache_cb_k, cache_cb_v,
        x_sample.shape[1], ew, ow)
    return (y_prompt, y_sample, p_ret, p_sb_k, p_sb_v, p_conv, p_lru, p_cb_k, p_cb_v,
            s_ret, s_sb_k, s_sb_v, s_conv, s_lru, s_cb_k, s_cb_v)
```

```python
import functools
import math

import jax
import jax.numpy as jnp
from jax import lax
from jax.experimental import pallas as pl
from jax.experimental.pallas import tpu as pltpu

F32 = jnp.float32
BF16 = jnp.bfloat16

NORM_EPS = 1e-6
ROPE_BASE = 10000.0
CHUNK = 64
RET_HEADS = 4
SB_HEADS = 8
CB_HEADS = 8
LRU_BLOCKS = 8
CONV_W = 4
LRU_C = 8.0
CB_PREV_CHUNKS = 8
CB_PAST = CB_PREV_CHUNKS * CHUNK
CB_BAND = (CB_PREV_CHUNKS + 1) * CHUNK
CB_MAX_REL = 128
NEG_BIG = -1e30
TOP_K = 2
LANES = 128


def _row_tile(m, cap=512):
    t = cap
    while m % t:
        t //= 2
    return t


def _rms(x, g):
    ms = jnp.mean(x * x, axis=-1, keepdims=True)
    return x * lax.rsqrt(ms + NORM_EPS) * g


def _dot(a, b):
    return jnp.dot(a, b, preferred_element_type=F32)


def _dot_nt(a, b):
    return lax.dot_general(a, b, (((1,), (1,)), ((), ())), preferred_element_type=F32)


def _dot_tn(a, b):
    return lax.dot_general(a, b, (((0,), (0,)), ((), ())), preferred_element_type=F32)


def _split(x):
    hi = x.astype(BF16)
    lo = (x - hi.astype(F32)).astype(BF16)
    return hi, lo


def _dot3(a, b):
    a_hi, a_lo = _split(a)
    b_hi, b_lo = _split(b)
    return _dot(a_hi, b_hi) + _dot(a_hi, b_lo) + _dot(a_lo, b_hi)


def _softplus(z):
    return jnp.maximum(z, 0.0) + jnp.log(1.0 + jnp.exp(-jnp.abs(z)))


def _sigmoid(z):
    return 1.0 / (1.0 + jnp.exp(-z))


def _norm_matmul_kernel(x_ref, g_ref, w_ref, o_ref, h_ref):
    @pl.when(pl.program_id(1) == 0)
    def _():
        h_ref[...] = _rms(x_ref[...], g_ref[...]).astype(BF16)

    o_ref[...] = _dot(h_ref[...], w_ref[...])


def norm_matmul(x, g, w, tn=512):
    m, k = x.shape
    n = w.shape[1]
    tm = _row_tile(m)
    return pl.pallas_call(
        _norm_matmul_kernel,
        grid=(m // tm, n // tn),
        in_specs=[
            pl.BlockSpec((tm, k), lambda i, j: (i, 0)),
            pl.BlockSpec((1, k), lambda i, j: (0, 0)),
            pl.BlockSpec((k, tn), lambda i, j: (0, j)),
        ],
        out_specs=pl.BlockSpec((tm, tn), lambda i, j: (i, j)),
        out_shape=jax.ShapeDtypeStruct((m, n), F32),
        scratch_shapes=[pltpu.VMEM((tm, k), BF16)],
        compiler_params=pltpu.CompilerParams(dimension_semantics=("parallel", "arbitrary")),
        name="norm_matmul",
    )(x, g.reshape(1, k), w)


def _out_proj_kernel(x_ref, a_ref, b_ref, wa_ref, wb_ref, o_ref):
    o_ref[...] = x_ref[...] + _dot(a_ref[...], wa_ref[...]) + _dot(b_ref[...], wb_ref[...])


def out_proj(x, a, b, w, tn=512):
    m, n = x.shape
    kh = a.shape[1]
    tm = _row_tile(m)
    return pl.pallas_call(
        _out_proj_kernel,
        grid=(m // tm, n // tn),
        in_specs=[
            pl.BlockSpec((tm, tn), lambda i, j: (i, j)),
            pl.BlockSpec((tm, kh), lambda i, j: (i, 0)),
            pl.BlockSpec((tm, kh), lambda i, j: (i, 0)),
            pl.BlockSpec((kh, tn), lambda i, j: (0, j)),
            pl.BlockSpec((kh, tn), lambda i, j: (1, j)),
        ],
        out_specs=pl.BlockSpec((tm, tn), lambda i, j: (i, j)),
        out_shape=jax.ShapeDtypeStruct((m, n), F32),
        compiler_params=pltpu.CompilerParams(dimension_semantics=("parallel", "arbitrary")),
        name="out_proj",
    )(x, a, b, w, w)


def _ffn_kernel(x_ref, g_ref, gate_ref, w1_ref, w3_ref, w2_ref, o_ref, h_ref, acc_ref, *, gated):
    e = pl.program_id(1)
    f = pl.program_id(2)

    @pl.when((e == 0) & (f == 0))
    def _():
        h_ref[...] = _rms(x_ref[...], g_ref[...]).astype(BF16)
        acc_ref[...] = jnp.zeros_like(acc_ref)

    h = h_ref[...]
    a = _dot(h, w1_ref[...].astype(BF16))
    u = _dot(h, w3_ref[...].astype(BF16))
    act = a * _sigmoid(a) * u
    if gated:
        gate = gate_ref[...]
        lane = lax.broadcasted_iota(jnp.int32, gate.shape, 1)
        act = act * jnp.sum(jnp.where(lane == e, gate, 0.0), axis=1, keepdims=True)
    acc_ref[...] += _dot(act.astype(BF16), w2_ref[...].astype(BF16))

    @pl.when((e == pl.num_programs(1) - 1) & (f == pl.num_programs(2) - 1))
    def _():
        o_ref[...] = x_ref[...] + acc_ref[...]


def ffn(x, g, gate, w1, w3, w2, tf=512):
    m, d = x.shape
    n_e, _, ff = w1.shape
    tm = _row_tile(m)
    gated = gate is not None
    if not gated:
        gate = jnp.ones((m, 1), F32)
    ng = gate.shape[1]
    return pl.pallas_call(
        functools.partial(_ffn_kernel, gated=gated),
        grid=(m // tm, n_e, ff // tf),
        in_specs=[
            pl.BlockSpec((tm, d), lambda i, e, f: (i, 0)),
            pl.BlockSpec((1, d), lambda i, e, f: (0, 0)),
            pl.BlockSpec((tm, ng), lambda i, e, f: (i, 0)),
            pl.BlockSpec((None, d, tf), lambda i, e, f: (e, 0, f)),
            pl.BlockSpec((None, d, tf), lambda i, e, f: (e, 0, f)),
            pl.BlockSpec((None, tf, d), lambda i, e, f: (e, f, 0)),
        ],
        out_specs=pl.BlockSpec((tm, d), lambda i, e, f: (i, 0)),
        out_shape=jax.ShapeDtypeStruct((m, d), F32),
        scratch_shapes=[pltpu.VMEM((tm, d), BF16), pltpu.VMEM((tm, d), F32)],
        compiler_params=pltpu.CompilerParams(
            dimension_semantics=("parallel", "arbitrary", "arbitrary"),
            vmem_limit_bytes=56 * 1024 * 1024),
        name="ffn",
    )(x, g.reshape(1, d), gate, w1, w3, w2)


def _router_kernel(x_ref, g_ref, r_ref, gate_ref):
    h = _rms(x_ref[...], g_ref[...])
    logits = _dot3(h, r_ref[...])
    n_e = logits.shape[1]
    lane = lax.broadcasted_iota(jnp.int32, logits.shape, 1)
    m1 = jnp.max(logits, axis=1, keepdims=True)
    i1 = jnp.min(jnp.where(logits == m1, lane, n_e), axis=1, keepdims=True)
    rest = jnp.where(lane == i1, -jnp.inf, logits)
    m2 = jnp.max(rest, axis=1, keepdims=True)
    i2 = jnp.min(jnp.where(rest == m2, lane, n_e), axis=1, keepdims=True)
    e2 = jnp.exp(m2 - m1)
    p1 = 1.0 / (1.0 + e2)
    gate_ref[...] = jnp.where(lane == i1, p1, jnp.where(lane == i2, e2 * p1, 0.0))


def router_gates(x, g, router):
    m, d = x.shape
    n_e = router.shape[1]
    tm = _row_tile(m)
    return pl.pallas_call(
        _router_kernel,
        grid=(m // tm,),
        in_specs=[
            pl.BlockSpec((tm, d), lambda i: (i, 0)),
            pl.BlockSpec((1, d), lambda i: (0, 0)),
            pl.BlockSpec((d, n_e), lambda i: (0, 0)),
        ],
        out_specs=pl.BlockSpec((tm, n_e), lambda i: (i, 0)),
        out_shape=jax.ShapeDtypeStruct((m, n_e), F32),
        compiler_params=pltpu.CompilerParams(dimension_semantics=("parallel",)),
        name="router",
    )(x, g.reshape(1, d), router)


def _ret_kernel(q_ref, k_ref, v_ref, g_ref, cos_ref, sin_ref, gn_ref, s0_ref, o_ref, s_ref, st_ref,
                *, has_state):
    n = pl.program_id(1)
    c = q_ref.shape[0]
    dk = q_ref.shape[1] // RET_HEADS
    half = dk // 2

    @pl.when(n == 0)
    def _():
        if has_state:
            st_ref[...] = s0_ref[...]
        else:
            st_ref[...] = jnp.zeros_like(st_ref)

    cos = cos_ref[...]
    sin = sin_ref[...]
    row = lax.broadcasted_iota(jnp.int32, (c, c), 0)
    col = lax.broadcasted_iota(jnp.int32, (c, c), 1)
    diff = (row - col).astype(F32)
    ridx = lax.broadcasted_iota(jnp.int32, (c, 1), 0).astype(F32)

    def rot(x):
        x1, x2 = x[:, :half], x[:, half:]
        return jnp.concatenate([x1 * cos - x2 * sin, x1 * sin + x2 * cos], axis=1)

    for h in range(RET_HEADS):
        log_g = math.log1p(-(2.0 ** (-5.0 - h)))
        sl = slice(h * dk, (h + 1) * dk)
        qr = rot(q_ref[:, sl]) * (dk ** -0.5)
        kr = rot(k_ref[:, sl])
        vh = v_ref[:, sl].astype(BF16)
        decay = jnp.where(diff >= 0, jnp.exp(log_g * jnp.maximum(diff, 0.0)), 0.0)
        scores = _dot_nt(qr.astype(BF16), kr.astype(BF16)) * decay
        s = st_ref[h]
        q_dec = jnp.exp(log_g * (ridx + 1.0))
        k_dec = jnp.exp(log_g * (c - 1.0 - ridx))
        o = _dot(scores.astype(BF16), vh) + _dot((qr * q_dec).astype(BF16), s.astype(BF16))
        st_ref[h] = s * math.exp(log_g * c) + _dot_tn((kr * k_dec).astype(BF16), vh)
        gh = g_ref[:, sl]
        o_ref[:, sl] = (gh * _sigmoid(gh) * _rms(o, gn_ref[h:h + 1, :])).astype(BF16)

    @pl.when(n == pl.num_programs(1) - 1)
    def _():
        s_ref[...] = st_ref[...]


def retention_mixer(qkv, row0, batch, seq, cos, sin, gn, s0):
    half_d = qkv.shape[1] // 7
    dk = half_d // RET_HEADS
    c = CHUNK if seq % CHUNK == 0 else seq
    nc = seq // c
    blk0 = row0 // c
    has_state = s0 is not None
    if not has_state:
        s0 = jnp.zeros((1, RET_HEADS, dk, dk), F32)

    def col(j):
        return pl.BlockSpec((c, half_d), lambda b, n: (blk0 + b * nc + n, j))

    return pl.pallas_call(
        functools.partial(_ret_kernel, has_state=has_state),
        grid=(batch, nc),
        in_specs=[
            col(0), col(1), col(2), col(3),
            pl.BlockSpec((c, dk // 2), lambda b, n: (n, 0)),
            pl.BlockSpec((c, dk // 2), lambda b, n: (n, 0)),
            pl.BlockSpec((RET_HEADS, dk), lambda b, n: (0, 0)),
            pl.BlockSpec((None, RET_HEADS, dk, dk), lambda b, n: (b if has_state else 0, 0, 0, 0)),
        ],
        out_specs=[
            pl.BlockSpec((c, half_d), lambda b, n: (b * nc + n, 0)),
            pl.BlockSpec((None, RET_HEADS, dk, dk), lambda b, n: (b, 0, 0, 0)),
        ],
        out_shape=[
            jax.ShapeDtypeStruct((batch * seq, half_d), BF16),
            jax.ShapeDtypeStruct((batch, RET_HEADS, dk, dk), F32),
        ],
        scratch_shapes=[pltpu.VMEM((RET_HEADS, dk, dk), F32)],
        compiler_params=pltpu.CompilerParams(dimension_semantics=("parallel", "arbitrary")),
        name="retention",
    )(qkv, qkv, qkv, qkv, cos, sin, gn, s0)


def _sb_norm_kernel(q_ref, k_ref, v_ref, qg_ref, kg_ref, qn_ref, kn_ref, knb_ref, vf_ref, vb_ref):
    dh = qg_ref.shape[1]
    for h in range(q_ref.shape[1] // dh):
        sl = slice(h * dh, (h + 1) * dh)
        qn_ref[:, sl] = (_rms(q_ref[:, sl], qg_ref[...]) * (dh ** -0.5)).astype(BF16)
        kn = _rms(k_ref[:, sl], kg_ref[...])
        kn_ref[:, sl] = kn
        knb_ref[:, sl] = kn.astype(BF16)
    v = v_ref[...]
    vf_ref[...] = v
    vb_ref[...] = v.astype(BF16)


def sb_norm(qkv, qg, kg):
    m = qkv.shape[0]
    half_d = qkv.shape[1] // 7
    dh = qg.shape[0]
    tm = _row_tile(m)

    def col(j):
        return pl.BlockSpec((tm, half_d), lambda i: (i, j))

    out = pl.BlockSpec((tm, half_d), lambda i: (i, 0))
    gspec = pl.BlockSpec((1, dh), lambda i: (0, 0))
    return pl.pallas_call(
        _sb_norm_kernel,
        grid=(m // tm,),
        in_specs=[col(4), col(5), col(6), gspec, gspec],
        out_specs=[out] * 5,
        out_shape=[jax.ShapeDtypeStruct((m, half_d), dt) for dt in (BF16, F32, BF16, F32, BF16)],
        compiler_params=pltpu.CompilerParams(dimension_semantics=("parallel",)),
        name="sb_norm",
    )(qkv, qkv, qkv, qg.reshape(1, dh), kg.reshape(1, dh))


def _sb_block(q, kb, vb, carry, acc, strict_diag):
    tq, tk = q.shape[0], kb.shape[0]
    z = _dot_nt(q, kb)
    sp = _softplus(z)
    log_keep = -sp
    if strict_diag:
        mask = (lax.broadcasted_iota(jnp.int32, (tq, tk), 1) < lax.broadcasted_iota(jnp.int32, (tq, tk), 0))
        log_keep = jnp.where(mask, log_keep, 0.0)
    later = (lax.broadcasted_iota(jnp.int32, (tk, tk), 0) > lax.broadcasted_iota(jnp.int32, (tk, tk), 1))
    u = jnp.where(later, 1.0, 0.0).astype(BF16)
    hi, lo = _split(log_keep)
    after = _dot(hi, u) + _dot(lo, u) + carry
    w = jnp.exp(z - sp + after)
    if strict_diag:
        w = jnp.where(mask, w, 0.0)
    acc = acc + _dot(w.astype(BF16), vb)
    carry = carry + jnp.sum(log_keep, axis=1, keepdims=True)
    return carry, acc


def _sb_prompt_kernel(q_ref, k_ref, v_ref, o_ref):
    i = pl.program_id(2)
    tq, dh = q_ref.shape
    q = q_ref[...]
    start = pl.multiple_of(i * tq, tq)
    carry, acc = _sb_block(q, k_ref[pl.ds(start, tq), :], v_ref[pl.ds(start, tq), :],
                           jnp.zeros((tq, 1), F32), jnp.zeros((tq, dh), F32), True)

    def body(t, state):
        s = pl.multiple_of((i - 1 - t) * tq, tq)
        return _sb_block(q, k_ref[pl.ds(s, tq), :], v_ref[pl.ds(s, tq), :], state[0], state[1], False)

    carry, acc = lax.fori_loop(0, i, body, (carry, acc))
    o_ref[...] = acc.astype(BF16)


def sb_attention_prompt(qn, knb, vb, batch, seq, tq=128):
    heads = qn.shape[1] // LANES
    nq = seq // tq
    return pl.pallas_call(
        _sb_prompt_kernel,
        grid=(batch, heads, nq),
        in_specs=[
            pl.BlockSpec((tq, LANES), lambda b, h, i: (b * nq + i, h)),
            pl.BlockSpec((seq, LANES), lambda b, h, i: (b, h)),
            pl.BlockSpec((seq, LANES), lambda b, h, i: (b, h)),
        ],
        out_specs=pl.BlockSpec((tq, LANES), lambda b, h, i: (b * nq + i, h)),
        out_shape=jax.ShapeDtypeStruct((batch * seq, heads * LANES), BF16),
        compiler_params=pltpu.CompilerParams(dimension_semantics=("parallel", "parallel", "arbitrary")),
        name="sb_attention_prompt",
    )(qn, knb, vb)


def _sb_sample_kernel(q_ref, k_ref, v_ref, kp_ref, vp_ref, o_ref, *, tk):
    tq, dh = q_ref.shape
    q = q_ref[...]
    carry, acc = _sb_block(q, k_ref[...], v_ref[...], jnp.zeros((tq, 1), F32), jnp.zeros((tq, dh), F32), True)
    n_past = kp_ref.shape[0] // tk

    def body(t, state):
        s = pl.multiple_of((n_past - 1 - t) * tk, tk)
        return _sb_block(q, kp_ref[pl.ds(s, tk), :].astype(BF16), vp_ref[pl.ds(s, tk), :].astype(BF16),
                         state[0], state[1], False)

    carry, acc = lax.fori_loop(0, n_past, body, (carry, acc))
    o_ref[...] = acc.astype(BF16)


def sb_attention_sample(qn, knb, vb, k_past, v_past, row0, batch, seq, tk=128):
    heads = qn.shape[1] // LANES
    past = k_past.shape[1]
    blk0 = row0 // seq
    new = pl.BlockSpec((seq, LANES), lambda b, h: (blk0 + b, h))
    old = pl.BlockSpec((None, past, LANES), lambda b, h: (b, 0, h))
    return pl.pallas_call(
        functools.partial(_sb_sample_kernel, tk=tk),
        grid=(batch, heads),
        in_specs=[new, new, new, old, old],
        out_specs=pl.BlockSpec((seq, LANES), lambda b, h: (b, h)),
        out_shape=jax.ShapeDtypeStruct((batch * seq, heads * LANES), BF16),
        compiler_params=pltpu.CompilerParams(dimension_semantics=("parallel", "parallel")),
        name="sb_attention_sample",
    )(qn, knb, vb, k_past, v_past)


def _lru_kernel(gc_ref, xc_ref, buf_ref, h0_ref, cw_ref, cb_ref, wr_ref, br_ref, wi_ref, bi_ref, lam_ref,
                o_ref, conv_ref, hl_ref, *, has_state):
    seq, bw = xc_ref.shape
    x = xc_ref[...]
    row = lax.broadcasted_iota(jnp.int32, (seq, bw), 0)
    cw = cw_ref[...]
    y = cb_ref[...] + cw[CONV_W - 1:CONV_W, :] * x
    for d in range(1, CONV_W):
        xs = pltpu.roll(x, d, 0)
        for r in range(d):
            prev = buf_ref[CONV_W - 1 - d + r:CONV_W - d + r, :] if has_state else jnp.zeros((1, bw), F32)
            xs = jnp.where(row == r, prev, xs)
        y = y + cw[CONV_W - 1 - d:CONV_W - d, :] * xs
    conv_ref[...] = x[seq - (CONV_W - 1):, :]

    r_gate = _sigmoid(_dot3(y, wr_ref[...]) + br_ref[...])
    i_gate = _sigmoid(_dot3(y, wi_ref[...]) + bi_ref[...])
    log_a = -LRU_C * r_gate * _softplus(-lam_ref[...])
    a = jnp.exp(log_a)
    u = jnp.sqrt(-jnp.tanh(log_a) * (a * a + 1.0)) * (i_gate * y)
    if has_state:
        u = jnp.where(row == 0, u + a * h0_ref[...], u)

    shift = 1
    while shift < seq:
        a_s = pltpu.roll(a, shift, 0)
        u_s = pltpu.roll(u, shift, 0)
        live = row >= shift
        u = jnp.where(live, a * u_s + u, u)
        a = jnp.where(live, a * a_s, a)
        shift *= 2

    hl_ref[...] = u[seq - 1:, :]
    gc = gc_ref[...]
    gelu = 0.5 * gc * (1.0 + jnp.tanh(math.sqrt(2.0 / math.pi) * (gc + 0.044715 * gc * gc * gc)))
    o_ref[...] = (gelu * u).astype(BF16)


def lru_mixer(proj, row0, batch, seq, conv_buf, h0, cw, cb, wr, br, wi, bi, lam):
    width = proj.shape[1] // 5
    bw = width // LRU_BLOCKS
    blk0 = row0 // seq
    has_state = conv_buf is not None
    if not has_state:
        conv_buf = jnp.zeros((1, CONV_W - 1, width), F32)
        h0 = jnp.zeros((1, width), F32)
    h0 = h0.reshape(-1, 1, width)

    def vec(k):
        return pl.BlockSpec((k, bw), lambda b, n: (0, n))

    wspec = pl.BlockSpec((None, bw, bw), lambda b, n: (n, 0, 0))
    return pl.pallas_call(
        functools.partial(_lru_kernel, has_state=has_state),
        grid=(batch, LRU_BLOCKS),
        in_specs=[
            pl.BlockSpec((seq, bw), lambda b, n: (blk0 + b, n)),
            pl.BlockSpec((seq, bw), lambda b, n: (blk0 + b, LRU_BLOCKS + n)),
            pl.BlockSpec((None, CONV_W - 1, bw), lambda b, n: (b if has_state else 0, 0, n)),
            pl.BlockSpec((None, 1, bw), lambda b, n: (b if has_state else 0, 0, n)),
            vec(CONV_W), vec(1), wspec, vec(1), wspec, vec(1), vec(1),
        ],
        out_specs=[
            pl.BlockSpec((seq, bw), lambda b, n: (b, n)),
            pl.BlockSpec((None, CONV_W - 1, bw), lambda b, n: (b, 0, n)),
            pl.BlockSpec((None, 1, bw), lambda b, n: (b, 0, n)),
        ],
        out_shape=[
            jax.ShapeDtypeStruct((batch * seq, width), BF16),
            jax.ShapeDtypeStruct((batch, CONV_W - 1, width), F32),
            jax.ShapeDtypeStruct((batch, 1, width), F32),
        ],
        compiler_params=pltpu.CompilerParams(dimension_semantics=("parallel", "parallel")),
        name="rg_lru",
    )(proj, proj, conv_buf, h0, cw, cb.reshape(1, width), wr, br.reshape(1, width), wi,
      bi.reshape(1, width), lam.reshape(1, width))


def _cb_kernel(q_ref, k_ref, v_ref, kp_ref, vp_ref, qg_ref, kg_ref, bias_ref, o_ref, ko_ref, vo_ref,
               kb_ref, vb_ref, *, has_past):
    seq, dh = q_ref.shape
    keep = ko_ref.shape[0]
    nc = seq // CHUNK
    kn = _rms(k_ref[...], kg_ref[...])
    v = v_ref[...]
    ko_ref[...] = kn[seq - keep:, :]
    vo_ref[...] = v[seq - keep:, :]
    if has_past:
        kb_ref[:CB_PAST, :] = kp_ref[...].astype(BF16)
        vb_ref[:CB_PAST, :] = vp_ref[...].astype(BF16)
    else:
        kb_ref[:CB_PAST, :] = jnp.zeros((CB_PAST, dh), BF16)
        vb_ref[:CB_PAST, :] = jnp.zeros((CB_PAST, dh), BF16)
    kb_ref[CB_PAST:, :] = kn.astype(BF16)
    vb_ref[CB_PAST:, :] = v.astype(BF16)
    bias = bias_ref[...]
    first_valid = 0 if has_past else CB_PAST
    kidx = lax.broadcasted_iota(jnp.int32, (CHUNK, CB_BAND), 1)

    def body(n, _):
        r0 = pl.multiple_of(n * CHUNK, CHUNK)
        qn = (_rms(q_ref[pl.ds(r0, CHUNK), :], qg_ref[...]) * (dh ** -0.5)).astype(BF16)
        s = _dot_nt(qn, kb_ref[pl.ds(r0, CB_BAND), :]) + bias
        s = jnp.where(kidx + n * CHUNK >= first_valid, s, NEG_BIG)
        p = jnp.exp(s - jnp.max(s, axis=1, keepdims=True))
        o = _dot(p.astype(BF16), vb_ref[pl.ds(r0, CB_BAND), :]) / jnp.sum(p, axis=1, keepdims=True)
        o_ref[pl.ds(r0, CHUNK), :] = o.astype(BF16)
        return 0

    lax.fori_loop(0, nc, body, 0)


def cb_attention(proj, row0, batch, seq, keep, k_past, v_past, qg, kg, bias):
    half_d = proj.shape[1] // 5
    heads = half_d // LANES
    blk0 = row0 // seq
    has_past = k_past is not None
    if not has_past:
        k_past = jnp.zeros((1, CB_PAST, half_d), F32)
        v_past = k_past

    def col(j):
        return pl.BlockSpec((seq, LANES), lambda b, h: (blk0 + b, j * heads + h))

    old = pl.BlockSpec((None, CB_PAST, LANES), lambda b, h: (b if has_past else 0, 0, h))
    gspec = pl.BlockSpec((1, LANES), lambda b, h: (0, 0))
    kept = pl.BlockSpec((None, keep, LANES), lambda b, h: (b, 0, h))
    return pl.pallas_call(
        functools.partial(_cb_kernel, has_past=has_past),
        grid=(batch, heads),
        in_specs=[col(2), col(3), col(4), old, old, gspec, gspec,
                  pl.BlockSpec((None, CHUNK, CB_BAND), lambda b, h: (h, 0, 0))],
        out_specs=[pl.BlockSpec((seq, LANES), lambda b, h: (b, h)), kept, kept],
        out_shape=[
            jax.ShapeDtypeStruct((batch * seq, half_d), BF16),
            jax.ShapeDtypeStruct((batch, keep, half_d), F32),
            jax.ShapeDtypeStruct((batch, keep, half_d), F32),
        ],
        scratch_shapes=[pltpu.VMEM((CB_PAST + seq, LANES), BF16), pltpu.VMEM((CB_PAST + seq, LANES), BF16)],
        compiler_params=pltpu.CompilerParams(dimension_semantics=("parallel", "parallel")),
        name="cb_attention",
    )(proj, proj, proj, k_past, v_past, qg.reshape(1, LANES), kg.reshape(1, LANES), bias)


def _rope_tables(pos0, seq, half):
    inv = ROPE_BASE ** (-jnp.arange(half, dtype=F32) / half)
    ang = (pos0 + jnp.arange(seq)).astype(F32)[:, None] * inv[None, :]
    return jnp.cos(ang), jnp.sin(ang)


def _band_bias(table):
    rel = (CB_PAST + jnp.arange(CHUNK))[:, None] - jnp.arange(CB_BAND)[None, :]
    return table[:, jnp.clip(rel, -CB_MAX_REL, CB_MAX_REL) + CB_MAX_REL]


def kernel(x_prompt, x_sample, state_ret, cache_sb_k, cache_sb_v, state_conv, state_lru, cache_cb_k, cache_cb_v, e_norm_mix, e_w_in, e_ret_gn, e_sb_qn, e_sb_kn, e_w_out, e_norm_ffn, e_w1, e_w3, e_w2, o_norm_mix, o_w_in, o_conv_w, o_conv_b, o_lru_wr, o_lru_br, o_lru_wi, o_lru_bi, o_lru_lam, o_cb_qn, o_cb_kn, o_cb_bias, o_w_out, o_norm_ffn, o_router, o_w1, o_w3, o_w2):
    bp, lp, d = x_prompt.shape
    bs, ls, _ = x_sample.shape
    half_d = d // 2
    tp = bp * lp
    past = cache_sb_k.shape[2]
    dk = half_d // RET_HEADS
    x = jnp.concatenate([x_prompt.reshape(tp, d), x_sample.reshape(bs * ls, d)], axis=0)

    qkv = norm_matmul(x, e_norm_mix[0], e_w_in[0].astype(BF16))
    cos_p, sin_p = _rope_tables(0, lp, dk // 2)
    cos_s, sin_s = _rope_tables(past, ls, dk // 2)
    a_p, p_ret = retention_mixer(qkv, 0, bp, lp, cos_p, sin_p, e_ret_gn[0], None)
    a_s, s_ret = retention_mixer(qkv, tp, bs, ls, cos_s, sin_s, e_ret_gn[0], state_ret[0])
    qn, kn, knb, vf, vb = sb_norm(qkv, e_sb_qn[0], e_sb_kn[0])
    b_p = sb_attention_prompt(qn, knb, vb, bp, lp)
    b_s = sb_attention_sample(qn, knb, vb, cache_sb_k[0].reshape(bs, past, half_d),
                              cache_sb_v[0].reshape(bs, past, half_d), tp, bs, ls)
    x = out_proj(x, jnp.concatenate([a_p, a_s], axis=0), jnp.concatenate([b_p, b_s], axis=0),
                 e_w_out[0].astype(BF16))
    x = ffn(x, e_norm_ffn[0], None, e_w1.astype(BF16), e_w3.astype(BF16), e_w2.astype(BF16))

    proj = norm_matmul(x, o_norm_mix[0], o_w_in[0].astype(BF16))
    lru_w = (o_conv_w[0], o_conv_b[0], o_lru_wr[0], o_lru_br[0], o_lru_wi[0], o_lru_bi[0], o_lru_lam[0])
    c_p, p_conv, p_lru = lru_mixer(proj, 0, bp, lp, None, None, *lru_w)
    c_s, s_conv, s_lru = lru_mixer(proj, tp, bs, ls, state_conv[0], state_lru[0], *lru_w)
    bias = _band_bias(o_cb_bias[0])
    keep_p = min(CB_PAST, lp)
    d_p, p_cb_k, p_cb_v = cb_attention(proj, 0, bp, lp, keep_p, None, None, o_cb_qn[0], o_cb_kn[0], bias)
    d_s, s_cb_k, s_cb_v = cb_attention(proj, tp, bs, ls, ls, cache_cb_k[0].reshape(bs, CB_PAST, half_d),
                                       cache_cb_v[0].reshape(bs, CB_PAST, half_d), o_cb_qn[0], o_cb_kn[0], bias)
    x = out_proj(x, jnp.concatenate([c_p, c_s], axis=0), jnp.concatenate([d_p, d_s], axis=0),
                 o_w_out[0].astype(BF16))
    gate = router_gates(x, o_norm_ffn[0], o_router[0])
    x = ffn(x, o_norm_ffn[0], gate, o_w1[0], o_w3[0], o_w2[0])

    sbh = (SB_HEADS, half_d // SB_HEADS)
    cbh = (CB_HEADS, half_d // CB_HEADS)
    return (
        x[:tp].reshape(bp, lp, d), x[tp:].reshape(bs, ls, d),
        p_ret[None], kn[:tp].reshape(1, bp, lp, *sbh), vf[:tp].reshape(1, bp, lp, *sbh),
        p_conv[None], p_lru.reshape(1, bp, half_d),
        p_cb_k.reshape(1, bp, keep_p, *cbh), p_cb_v.reshape(1, bp, keep_p, *cbh),
        s_ret[None], kn[tp:].reshape(1, bs, ls, *sbh), vf[tp:].reshape(1, bs, ls, *sbh),
        s_conv[None], s_lru.reshape(1, bs, half_d),
        s_cb_k.reshape(1, bs, ls, *cbh), s_cb_v.reshape(1, bs, ls, *cbh),
    )
```

```python
import functools
import math

import jax
import jax.numpy as jnp
from jax import lax
from jax.experimental import pallas as pl
from jax.experimental.pallas import tpu as pltpu

F32 = jnp.float32
BF16 = jnp.bfloat16

NORM_EPS = 1e-6
ROPE_BASE = 10000.0
CHUNK = 64
RET_HEADS = 4
SB_HEADS = 8
CB_HEADS = 8
LRU_BLOCKS = 8
CONV_W = 4
LRU_C = 8.0
CB_PREV_CHUNKS = 8
CB_PAST = CB_PREV_CHUNKS * CHUNK
CB_BAND = (CB_PREV_CHUNKS + 1) * CHUNK
CB_MAX_REL = 128
NEG_BIG = -1e30
TOP_K = 2
LANES = 128


def _row_tile(m, cap=512):
    t = cap
    while m % t:
        t //= 2
    return t


def _rms(x, g):
    ms = jnp.mean(x * x, axis=-1, keepdims=True)
    return x * lax.rsqrt(ms + NORM_EPS) * g


def _dot(a, b):
    return jnp.dot(a, b, preferred_element_type=F32)


def _dot_nt(a, b):
    return lax.dot_general(a, b, (((1,), (1,)), ((), ())), preferred_element_type=F32)


def _dot_tn(a, b):
    return lax.dot_general(a, b, (((0,), (0,)), ((), ())), preferred_element_type=F32)


def _split(x):
    hi = x.astype(BF16)
    lo = (x - hi.astype(F32)).astype(BF16)
    return hi, lo


def _dot3(a, b):
    a_hi, a_lo = _split(a)
    b_hi, b_lo = _split(b)
    return _dot(a_hi, b_hi) + _dot(a_hi, b_lo) + _dot(a_lo, b_hi)


def _softplus(z):
    return jnp.maximum(z, 0.0) + jnp.log(1.0 + jnp.exp(-jnp.abs(z)))


def _sigmoid(z):
    return 1.0 / (1.0 + jnp.exp(-z))


def _norm_matmul_kernel(x_ref, g_ref, w_ref, o_ref, h_ref):
    @pl.when(pl.program_id(1) == 0)
    def _():
        h_ref[...] = _rms(x_ref[...], g_ref[...]).astype(BF16)

    o_ref[...] = _dot(h_ref[...], w_ref[...])


def norm_matmul(x, g, w, tn=1024):
    m, k = x.shape
    n = w.shape[1]
    tm = _row_tile(m)
    return pl.pallas_call(
        _norm_matmul_kernel,
        grid=(m // tm, n // tn),
        in_specs=[
            pl.BlockSpec((tm, k), lambda i, j: (i, 0)),
            pl.BlockSpec((1, k), lambda i, j: (0, 0)),
            pl.BlockSpec((k, tn), lambda i, j: (0, j)),
        ],
        out_specs=pl.BlockSpec((tm, tn), lambda i, j: (i, j)),
        out_shape=jax.ShapeDtypeStruct((m, n), F32),
        scratch_shapes=[pltpu.VMEM((tm, k), BF16)],
        compiler_params=pltpu.CompilerParams(dimension_semantics=("parallel", "arbitrary")),
        name="norm_matmul",
    )(x, g.reshape(1, k), w)


def _out_proj_kernel(x_ref, a_ref, b_ref, wa_ref, wb_ref, o_ref):
    o_ref[...] = x_ref[...] + _dot(a_ref[...], wa_ref[...]) + _dot(b_ref[...], wb_ref[...])


def out_proj(x, a, b, w, tn=1024):
    m, n = x.shape
    kh = a.shape[1]
    tm = _row_tile(m)
    return pl.pallas_call(
        _out_proj_kernel,
        grid=(m // tm, n // tn),
        in_specs=[
            pl.BlockSpec((tm, tn), lambda i, j: (i, j)),
            pl.BlockSpec((tm, kh), lambda i, j: (i, 0)),
            pl.BlockSpec((tm, kh), lambda i, j: (i, 0)),
            pl.BlockSpec((kh, tn), lambda i, j: (0, j)),
            pl.BlockSpec((kh, tn), lambda i, j: (1, j)),
        ],
        out_specs=pl.BlockSpec((tm, tn), lambda i, j: (i, j)),
        out_shape=jax.ShapeDtypeStruct((m, n), F32),
        compiler_params=pltpu.CompilerParams(dimension_semantics=("parallel", "arbitrary")),
        name="out_proj",
    )(x, a, b, w, w)


def _ffn_kernel(x_ref, g_ref, w1_ref, w3_ref, w2_ref, o_ref, h_ref, acc_ref):
    f = pl.program_id(1)

    @pl.when(f == 0)
    def _():
        h_ref[...] = _rms(x_ref[...], g_ref[...]).astype(BF16)
        acc_ref[...] = jnp.zeros_like(acc_ref)

    h = h_ref[...]
    a = _dot(h, w1_ref[...])
    u = _dot(h, w3_ref[...])
    acc_ref[...] += _dot((a * _sigmoid(a) * u).astype(BF16), w2_ref[...])

    @pl.when(f == pl.num_programs(1) - 1)
    def _():
        o_ref[...] = x_ref[...] + acc_ref[...]


def ffn(x, g, w1, w3, w2, tf=512):
    m, d = x.shape
    ff = w1.shape[1]
    tm = _row_tile(m)
    return pl.pallas_call(
        _ffn_kernel,
        grid=(m // tm, ff // tf),
        in_specs=[
            pl.BlockSpec((tm, d), lambda i, f: (i, 0)),
            pl.BlockSpec((1, d), lambda i, f: (0, 0)),
            pl.BlockSpec((d, tf), lambda i, f: (0, f)),
            pl.BlockSpec((d, tf), lambda i, f: (0, f)),
            pl.BlockSpec((tf, d), lambda i, f: (f, 0)),
        ],
        out_specs=pl.BlockSpec((tm, d), lambda i, f: (i, 0)),
        out_shape=jax.ShapeDtypeStruct((m, d), F32),
        scratch_shapes=[pltpu.VMEM((tm, d), BF16), pltpu.VMEM((tm, d), F32)],
        compiler_params=pltpu.CompilerParams(dimension_semantics=("parallel", "arbitrary")),
        name="ffn",
    )(x, g.reshape(1, d), w1, w3, w2)


def _router_kernel(x_ref, g_ref, r_ref, eid_ref, p_ref, rank_ref, cnt_ref, hp_ref, run_ref):
    i = pl.program_id(0)

    @pl.when(i == 0)
    def _():
        run_ref[...] = jnp.zeros_like(run_ref)

    h = _rms(x_ref[...], g_ref[...])
    tm, d = h.shape
    logits = _dot3(h, r_ref[...])
    n_e = logits.shape[1]
    lane = lax.broadcasted_iota(jnp.int32, logits.shape, 1).astype(F32)
    m1 = jnp.max(logits, axis=1, keepdims=True)
    i1 = jnp.min(jnp.where(logits == m1, lane, float(n_e)), axis=1, keepdims=True)
    rest = jnp.where(lane == i1, -jnp.inf, logits)
    m2 = jnp.max(rest, axis=1, keepdims=True)
    i2 = jnp.min(jnp.where(rest == m2, lane, float(n_e)), axis=1, keepdims=True)
    e2 = jnp.exp(m2 - m1)
    p1 = 1.0 / (1.0 + e2)
    slot = lax.broadcasted_iota(jnp.int32, (tm, TOP_K), 1)
    eid_ref[...] = jnp.where(slot == 0, i1, i2).astype(jnp.int32)
    p_ref[...] = jnp.where(slot == 0, p1, e2 * p1)

    hit1 = lane == i1
    hit2 = lane == i2
    both = jnp.where(hit1 | hit2, 1.0, 0.0)
    earlier = (lax.broadcasted_iota(jnp.int32, (tm, tm), 1) < lax.broadcasted_iota(jnp.int32, (tm, tm), 0))
    before = _dot(jnp.where(earlier, 1.0, 0.0).astype(BF16), both.astype(BF16)) + run_ref[...]
    r1 = jnp.sum(jnp.where(hit1, before, 0.0), axis=1, keepdims=True)
    r2 = jnp.sum(jnp.where(hit2, before, 0.0), axis=1, keepdims=True)
    rank_ref[...] = jnp.where(slot == 0, r1, r2).astype(jnp.int32)
    run_ref[...] += jnp.sum(both, axis=0, keepdims=True)
    cnt_ref[...] = run_ref[...].astype(jnp.int32)

    half = d // 2
    top = lax.bitcast_convert_type(h[:, :half].astype(BF16).astype(F32), jnp.uint32)
    bot = lax.bitcast_convert_type(h[:, half:].astype(BF16).astype(F32), jnp.uint32)
    hp_ref[...] = top | (bot >> 16)


def moe_route(x, g, router):
    m, d = x.shape
    n_e = router.shape[1]
    tm = _row_tile(m)
    pair = pl.BlockSpec((tm, TOP_K), lambda i: (i, 0))
    return pl.pallas_call(
        _router_kernel,
        grid=(m // tm,),
        in_specs=[
            pl.BlockSpec((tm, d), lambda i: (i, 0)),
            pl.BlockSpec((1, d), lambda i: (0, 0)),
            pl.BlockSpec((d, n_e), lambda i: (0, 0)),
        ],
        out_specs=[pair, pair, pair, pl.BlockSpec((1, n_e), lambda i: (0, 0)),
                   pl.BlockSpec((tm, d // 2), lambda i: (i, 0))],
        out_shape=[
            jax.ShapeDtypeStruct((m, TOP_K), jnp.int32),
            jax.ShapeDtypeStruct((m, TOP_K), F32),
            jax.ShapeDtypeStruct((m, TOP_K), jnp.int32),
            jax.ShapeDtypeStruct((1, n_e), jnp.int32),
            jax.ShapeDtypeStruct((m, d // 2), jnp.uint32),
        ],
        scratch_shapes=[pltpu.VMEM((1, n_e), F32)],
        compiler_params=pltpu.CompilerParams(dimension_semantics=("arbitrary",)),
        name="moe_route",
    )(x, g.reshape(1, d), router)


MOE_DMA_WINDOW = 32


def _row_copy(src, s, dst, t, sem):
    return pltpu.make_async_copy(src.at[pl.ds(s, 1)], dst.at[pl.ds(t, 1)], sem)


def _dispatch_kernel(pos_ref, hp_ref, init_ref, hs_ref, sem):
    del init_ref
    n_tok = hp_ref.shape[0]
    n_win = n_tok // MOE_DMA_WINDOW

    def issue(w):
        def body(j, _):
            t = w * MOE_DMA_WINDOW + j
            for k in range(TOP_K):
                _row_copy(hp_ref, t, hs_ref, pos_ref[TOP_K * t + k], sem).start()
            return 0
        lax.fori_loop(0, MOE_DMA_WINDOW, body, 0)

    def drain():
        def body(j, _):
            for k in range(TOP_K):
                _row_copy(hp_ref, 0, hs_ref, 0, sem).wait()
            return 0
        lax.fori_loop(0, MOE_DMA_WINDOW, body, 0)

    issue(0)

    def step(w, _):
        @pl.when(w + 1 < n_win)
        def _():
            issue(w + 1)
        drain()
        return 0

    lax.fori_loop(0, n_win, step, 0)


def moe_dispatch(pos, hp, n_rows):
    m, dw = hp.shape
    assert m % MOE_DMA_WINDOW == 0
    return pl.pallas_call(
        _dispatch_kernel,
        grid_spec=pltpu.PrefetchScalarGridSpec(
            num_scalar_prefetch=1,
            grid=(1,),
            in_specs=[pl.BlockSpec(memory_space=pl.ANY), pl.BlockSpec(memory_space=pl.ANY)],
            out_specs=pl.BlockSpec(memory_space=pl.ANY),
            scratch_shapes=[pltpu.SemaphoreType.DMA],
        ),
        out_shape=jax.ShapeDtypeStruct((n_rows, dw), jnp.uint32),
        input_output_aliases={2: 0},
        name="moe_dispatch",
    )(pos.reshape(-1), hp, jnp.zeros((n_rows, dw), jnp.uint32))


def _new_expert(te_ref, r):
    return (r == 0) | (te_ref[r] != te_ref[jnp.maximum(r - 1, 0)])


def _moe_up_kernel(te_ref, nu_ref, hp_ref, w1_ref, w3_ref, act_ref, w1b_ref, w3b_ref):
    r = pl.program_id(1)

    @pl.when(_new_expert(te_ref, r))
    def _():
        w1b_ref[...] = w1_ref[...].astype(BF16)
        w3b_ref[...] = w3_ref[...].astype(BF16)

    @pl.when(r < nu_ref[0])
    def _():
        hp = hp_ref[...]
        half = hp.shape[1]
        top = lax.bitcast_convert_type(hp & jnp.uint32(0xFFFF0000), F32).astype(BF16)
        bot = lax.bitcast_convert_type(hp << 16, F32).astype(BF16)
        a = _dot(top, w1b_ref[:half, :]) + _dot(bot, w1b_ref[half:, :])
        u = _dot(top, w3b_ref[:half, :]) + _dot(bot, w3b_ref[half:, :])
        act_ref[...] = (a * _sigmoid(a) * u).astype(BF16)

    @pl.when(r >= nu_ref[0])
    def _():
        act_ref[...] = jnp.zeros_like(act_ref)


def _moe_down_kernel(te_ref, nu_ref, act_ref, w2_ref, y_ref, w2b_ref):
    r = pl.program_id(1)

    @pl.when(_new_expert(te_ref, r))
    def _():
        w2b_ref[...] = w2_ref[...].astype(BF16)

    @pl.when(r < nu_ref[0])
    def _():
        y_ref[...] = _dot(act_ref[...], w2b_ref[...])

    @pl.when(r >= nu_ref[0])
    def _():
        y_ref[...] = jnp.zeros_like(y_ref)


def moe_experts(tile_expert, n_used, hs, w1, w3, w2, tm, tf=512, tn=512):
    n_rows, dw = hs.shape
    d = 2 * dw
    ff = w1.shape[2]
    n_tiles = n_rows // tm

    def row(r, nu):
        return jnp.minimum(r, nu[0] - 1)

    act = pl.pallas_call(
        _moe_up_kernel,
        grid_spec=pltpu.PrefetchScalarGridSpec(
            num_scalar_prefetch=2,
            grid=(ff // tf, n_tiles),
            in_specs=[
                pl.BlockSpec((tm, dw), lambda f, r, te, nu: (row(r, nu), 0)),
                pl.BlockSpec((None, d, tf), lambda f, r, te, nu: (te[r], 0, f)),
                pl.BlockSpec((None, d, tf), lambda f, r, te, nu: (te[r], 0, f)),
            ],
            out_specs=pl.BlockSpec((tm, tf), lambda f, r, te, nu: (r, f)),
            scratch_shapes=[pltpu.VMEM((d, tf), BF16), pltpu.VMEM((d, tf), BF16)],
        ),
        out_shape=jax.ShapeDtypeStruct((n_rows, ff), BF16),
        compiler_params=pltpu.CompilerParams(dimension_semantics=("arbitrary", "arbitrary"),
                                             vmem_limit_bytes=48 * 1024 * 1024),
        name="moe_up",
    )(tile_expert, n_used, hs, w1, w3)
    return pl.pallas_call(
        _moe_down_kernel,
        grid_spec=pltpu.PrefetchScalarGridSpec(
            num_scalar_prefetch=2,
            grid=(d // tn, n_tiles),
            in_specs=[
                pl.BlockSpec((tm, ff), lambda n, r, te, nu: (row(r, nu), 0)),
                pl.BlockSpec((None, ff, tn), lambda n, r, te, nu: (te[r], 0, n)),
            ],
            out_specs=pl.BlockSpec((tm, tn), lambda n, r, te, nu: (r, n)),
            scratch_shapes=[pltpu.VMEM((ff, tn), BF16)],
        ),
        out_shape=jax.ShapeDtypeStruct((n_rows, d), F32),
        compiler_params=pltpu.CompilerParams(dimension_semantics=("arbitrary", "arbitrary"),
                                             vmem_limit_bytes=56 * 1024 * 1024),
        name="moe_down",
    )(tile_expert, n_used, act, w2)


def _combine_kernel(pos_ref, x_ref, p_ref, y_ref, o_ref, ya_ref, yb_ref, sem):
    tm = x_ref.shape[0]
    t0 = pl.program_id(0) * tm

    def issue(j, _):
        _row_copy(y_ref, pos_ref[TOP_K * (t0 + j)], ya_ref, j, sem).start()
        _row_copy(y_ref, pos_ref[TOP_K * (t0 + j) + 1], yb_ref, j, sem).start()
        return 0

    def drain(j, _):
        _row_copy(y_ref, 0, ya_ref, 0, sem).wait()
        _row_copy(y_ref, 0, yb_ref, 0, sem).wait()
        return 0

    lax.fori_loop(0, tm, issue, 0)
    lax.fori_loop(0, tm, drain, 0)
    p = p_ref[...]
    o_ref[...] = x_ref[...] + p[:, 0:1] * ya_ref[...] + p[:, 1:2] * yb_ref[...]


def moe_combine(pos, x, p, y, tm=128):
    m, d = x.shape
    return pl.pallas_call(
        _combine_kernel,
        grid_spec=pltpu.PrefetchScalarGridSpec(
            num_scalar_prefetch=1,
            grid=(m // tm,),
            in_specs=[
                pl.BlockSpec((tm, d), lambda i, pos: (i, 0)),
                pl.BlockSpec((tm, TOP_K), lambda i, pos: (i, 0)),
                pl.BlockSpec(memory_space=pl.ANY),
            ],
            out_specs=pl.BlockSpec((tm, d), lambda i, pos: (i, 0)),
            scratch_shapes=[pltpu.VMEM((tm, d), F32), pltpu.VMEM((tm, d), F32), pltpu.SemaphoreType.DMA],
        ),
        out_shape=jax.ShapeDtypeStruct((m, d), F32),
        compiler_params=pltpu.CompilerParams(dimension_semantics=("arbitrary",)),
        name="moe_combine",
    )(pos.reshape(-1), x, p, y)


def moe_layer(x, g, router, w1, w3, w2, tm=512):
    m = x.shape[0]
    n_e = router.shape[1]
    eid, p, rank, counts, hp = moe_route(x, g, router)
    padded = (counts[0] + tm - 1) // tm * tm
    ends = jnp.cumsum(padded)
    starts = ends - padded
    experts = jnp.arange(n_e, dtype=jnp.int32)
    pos = rank + jnp.sum(jnp.where(eid[:, :, None] == experts, starts, 0), axis=-1)
    n_tiles = -(-TOP_K * m // tm) + n_e
    n_used = (ends[-1] // tm).astype(jnp.int32)
    tile_row = jnp.minimum(jnp.arange(n_tiles, dtype=jnp.int32), n_used - 1) * tm
    tile_expert = jnp.sum(tile_row[:, None] >= ends[None, :], axis=1).astype(jnp.int32)
    hs = moe_dispatch(pos, hp, n_tiles * tm)
    y = moe_experts(tile_expert, n_used.reshape(1), hs, w1, w3, w2, tm)
    return moe_combine(pos, x, p, y)


def _ret_kernel(q_ref, k_ref, v_ref, g_ref, cos_ref, sin_ref, gn_ref, s0_ref, o_ref, s_ref, st_ref,
                *, has_state):
    n = pl.program_id(1)
    c = q_ref.shape[0]
    dk = q_ref.shape[1] // RET_HEADS
    half = dk // 2

    @pl.when(n == 0)
    def _():
        if has_state:
            st_ref[...] = s0_ref[...]
        else:
            st_ref[...] = jnp.zeros_like(st_ref)

    cos = cos_ref[...]
    sin = sin_ref[...]
    row = lax.broadcasted_iota(jnp.int32, (c, c), 0)
    col = lax.broadcasted_iota(jnp.int32, (c, c), 1)
    diff = (row - col).astype(F32)
    ridx = lax.broadcasted_iota(jnp.int32, (c, 1), 0).astype(F32)

    def rot(x):
        x1, x2 = x[:, :half], x[:, half:]
        return jnp.concatenate([x1 * cos - x2 * sin, x1 * sin + x2 * cos], axis=1)

    for h in range(RET_HEADS):
        log_g = math.log1p(-(2.0 ** (-5.0 - h)))
        sl = slice(h * dk, (h + 1) * dk)
        qr = rot(q_ref[:, sl]) * (dk ** -0.5)
        kr = rot(k_ref[:, sl])
        vh = v_ref[:, sl].astype(BF16)
        decay = jnp.where(diff >= 0, jnp.exp(log_g * jnp.maximum(diff, 0.0)), 0.0)
        scores = _dot_nt(qr.astype(BF16), kr.astype(BF16)) * decay
        s = st_ref[h]
        q_dec = jnp.exp(log_g * (ridx + 1.0))
        k_dec = jnp.exp(log_g * (c - 1.0 - ridx))
        o = _dot(scores.astype(BF16), vh) + _dot((qr * q_dec).astype(BF16), s.astype(BF16))
        st_ref[h] = s * math.exp(log_g * c) + _dot_tn((kr * k_dec).astype(BF16), vh)
        gh = g_ref[:, sl]
        o_ref[:, sl] = (gh * _sigmoid(gh) * _rms(o, gn_ref[h:h + 1, :])).astype(BF16)

    @pl.when(n == pl.num_programs(1) - 1)
    def _():
        s_ref[...] = st_ref[...]


def retention_mixer(qkv, row0, batch, seq, cos, sin, gn, s0):
    half_d = qkv.shape[1] // 7
    dk = half_d // RET_HEADS
    c = CHUNK if seq % CHUNK == 0 else seq
    nc = seq // c
    blk0 = row0 // c
    has_state = s0 is not None
    if not has_state:
        s0 = jnp.zeros((1, RET_HEADS, dk, dk), F32)

    def col(j):
        return pl.BlockSpec((c, half_d), lambda b, n: (blk0 + b * nc + n, j))

    return pl.pallas_call(
        functools.partial(_ret_kernel, has_state=has_state),
        grid=(batch, nc),
        in_specs=[
            col(0), col(1), col(2), col(3),
            pl.BlockSpec((c, dk // 2), lambda b, n: (n, 0)),
            pl.BlockSpec((c, dk // 2), lambda b, n: (n, 0)),
            pl.BlockSpec((RET_HEADS, dk), lambda b, n: (0, 0)),
            pl.BlockSpec((None, RET_HEADS, dk, dk), lambda b, n: (b if has_state else 0, 0, 0, 0)),
        ],
        out_specs=[
            pl.BlockSpec((c, half_d), lambda b, n: (b * nc + n, 0)),
            pl.BlockSpec((None, RET_HEADS, dk, dk), lambda b, n: (b, 0, 0, 0)),
        ],
        out_shape=[
            jax.ShapeDtypeStruct((batch * seq, half_d), BF16),
            jax.ShapeDtypeStruct((batch, RET_HEADS, dk, dk), F32),
        ],
        scratch_shapes=[pltpu.VMEM((RET_HEADS, dk, dk), F32)],
        compiler_params=pltpu.CompilerParams(dimension_semantics=("parallel", "arbitrary")),
        name="retention",
    )(qkv, qkv, qkv, qkv, cos, sin, gn, s0)


def _sb_norm_kernel(q_ref, k_ref, v_ref, qg_ref, kg_ref, qn_ref, kn_ref, knb_ref, vf_ref, vb_ref):
    dh = qg_ref.shape[1]
    for h in range(q_ref.shape[1] // dh):
        sl = slice(h * dh, (h + 1) * dh)
        qn_ref[:, sl] = (_rms(q_ref[:, sl], qg_ref[...]) * (dh ** -0.5)).astype(BF16)
        kn = _rms(k_ref[:, sl], kg_ref[...])
        kn_ref[:, sl] = kn
        knb_ref[:, sl] = kn.astype(BF16)
    v = v_ref[...]
    vf_ref[...] = v
    vb_ref[...] = v.astype(BF16)


def sb_norm(qkv, qg, kg):
    m = qkv.shape[0]
    half_d = qkv.shape[1] // 7
    dh = qg.shape[0]
    tm = _row_tile(m)

    def col(j):
        return pl.BlockSpec((tm, half_d), lambda i: (i, j))

    out = pl.BlockSpec((tm, half_d), lambda i: (i, 0))
    gspec = pl.BlockSpec((1, dh), lambda i: (0, 0))
    return pl.pallas_call(
        _sb_norm_kernel,
        grid=(m // tm,),
        in_specs=[col(4), col(5), col(6), gspec, gspec],
        out_specs=[out] * 5,
        out_shape=[jax.ShapeDtypeStruct((m, half_d), dt) for dt in (BF16, F32, BF16, F32, BF16)],
        compiler_params=pltpu.CompilerParams(dimension_semantics=("parallel",)),
        name="sb_norm",
    )(qkv, qkv, qkv, qg.reshape(1, dh), kg.reshape(1, dh))


SB_EXIT_LOG = 88.0


def _sb_block(q, kb, vb, carry, acc, strict_diag):
    tq, tk = q.shape[0], kb.shape[0]
    z = _dot_nt(q, kb)
    sp = _softplus(z)
    log_keep = -sp
    if strict_diag:
        mask = (lax.broadcasted_iota(jnp.int32, (tq, tk), 1) < lax.broadcasted_iota(jnp.int32, (tq, tk), 0))
        log_keep = jnp.where(mask, log_keep, 0.0)
    later = (lax.broadcasted_iota(jnp.int32, (tk, tk), 0) > lax.broadcasted_iota(jnp.int32, (tk, tk), 1))
    u = jnp.where(later, 1.0, 0.0).astype(BF16)
    hi, lo = _split(log_keep)
    after = _dot(hi, u) + _dot(lo, u) + carry
    w = jnp.exp(z - sp + after)
    if strict_diag:
        w = jnp.where(mask, w, 0.0)
    return jnp.sum(log_keep, axis=1, keepdims=True), acc + _dot(w.astype(BF16), vb)


def _sb_visit(q_ref, carry_ref, acc_ref, key_block, strict_diag):
    heads = q_ref.shape[1] // LANES
    worst = None
    for h in range(heads):
        sl = slice(h * LANES, (h + 1) * LANES)
        kb, vb = key_block(h, sl)
        tk = kb.shape[0]
        carry = carry_ref[h]
        block_sum, acc = _sb_block(q_ref[:, sl], kb, vb, carry[:, :tk], acc_ref[:, sl], strict_diag)
        carry = carry + block_sum
        carry_ref[h] = carry
        acc_ref[:, sl] = acc
        worst = carry if worst is None else jnp.maximum(worst, carry)
    return jnp.max(worst)


def _sb_prompt_kernel(q_ref, k_ref, v_ref, o_ref, acc_ref, carry_ref):
    i = pl.program_id(1)
    tq = q_ref.shape[0]
    acc_ref[...] = jnp.zeros_like(acc_ref)
    carry_ref[...] = jnp.zeros_like(carry_ref)

    def visit(j, strict_diag):
        s = pl.multiple_of(j * tq, tq)
        return _sb_visit(q_ref, carry_ref, acc_ref,
                         lambda h, sl: (k_ref[pl.ds(s, tq), sl], v_ref[pl.ds(s, tq), sl]), strict_diag)

    top = visit(i, True)
    lax.while_loop(lambda st: (st[0] >= 0) & (st[1] > -SB_EXIT_LOG),
                   lambda st: (st[0] - 1, visit(st[0], False)), (i - 1, top))
    o_ref[...] = acc_ref[...].astype(BF16)


def sb_attention_prompt(qn, knb, vb, batch, seq, tq=LANES):
    width = qn.shape[1]
    nq = seq // tq
    return pl.pallas_call(
        _sb_prompt_kernel,
        grid=(batch, nq),
        in_specs=[
            pl.BlockSpec((tq, width), lambda b, i: (b * nq + i, 0)),
            pl.BlockSpec((seq, width), lambda b, i: (b, 0)),
            pl.BlockSpec((seq, width), lambda b, i: (b, 0)),
        ],
        out_specs=pl.BlockSpec((tq, width), lambda b, i: (b * nq + i, 0)),
        out_shape=jax.ShapeDtypeStruct((batch * seq, width), BF16),
        scratch_shapes=[pltpu.VMEM((tq, width), F32), pltpu.VMEM((width // LANES, tq, LANES), F32)],
        compiler_params=pltpu.CompilerParams(dimension_semantics=("parallel", "arbitrary")),
        name="sb_attention_prompt",
    )(qn, knb, vb)


def _sb_sample_kernel(q_ref, k_ref, v_ref, kp_ref, vp_ref, o_ref, acc_ref, carry_ref, kbuf_ref, vbuf_ref, sem):
    b = pl.program_id(0)
    tk = kbuf_ref.shape[1]
    n_past = kp_ref.shape[1] // tk
    acc_ref[...] = jnp.zeros_like(acc_ref)
    carry_ref[...] = jnp.zeros_like(carry_ref)

    def fetch(j):
        slot = (n_past - 1 - j) % 2
        rows = pl.ds(pl.multiple_of(j * tk, tk), tk)
        return (pltpu.make_async_copy(kp_ref.at[b, rows], kbuf_ref.at[slot], sem.at[0, slot]),
                pltpu.make_async_copy(vp_ref.at[b, rows], vbuf_ref.at[slot], sem.at[1, slot]))

    def start(j):
        for c in fetch(j):
            c.start()

    def wait(j):
        for c in fetch(j):
            c.wait()

    start(n_past - 1)
    top = _sb_visit(q_ref, carry_ref, acc_ref, lambda h, sl: (k_ref[:, sl], v_ref[:, sl]), True)

    def body(st):
        j = st[0]
        slot = (n_past - 1 - j) % 2
        wait(j)

        @pl.when(j >= 1)
        def _():
            start(j - 1)

        top = _sb_visit(q_ref, carry_ref, acc_ref,
                        lambda h, sl: (kbuf_ref[slot, :, h, :].astype(BF16), vbuf_ref[slot, :, h, :].astype(BF16)),
                        False)
        return j - 1, top

    j_end, _ = lax.while_loop(lambda st: (st[0] >= 0) & (st[1] > -SB_EXIT_LOG), body, (n_past - 1, top))

    @pl.when(j_end >= 0)
    def _():
        wait(j_end)

    o_ref[...] = acc_ref[...].astype(BF16)


def sb_attention_sample(qn, knb, vb, k_past, v_past, row0, batch, seq, tk=LANES):
    width = qn.shape[1]
    heads = width // LANES
    blk0 = row0 // seq
    assert k_past.shape[1] % tk == 0 and k_past.shape[2:] == (heads, LANES)
    new = pl.BlockSpec((seq, width), lambda b: (blk0 + b, 0))
    return pl.pallas_call(
        _sb_sample_kernel,
        grid=(batch,),
        in_specs=[new, new, new, pl.BlockSpec(memory_space=pl.ANY), pl.BlockSpec(memory_space=pl.ANY)],
        out_specs=pl.BlockSpec((seq, width), lambda b: (b, 0)),
        out_shape=jax.ShapeDtypeStruct((batch * seq, width), BF16),
        scratch_shapes=[
            pltpu.VMEM((seq, width), F32), pltpu.VMEM((heads, seq, LANES), F32),
            pltpu.VMEM((2, tk, heads, LANES), F32), pltpu.VMEM((2, tk, heads, LANES), F32),
            pltpu.SemaphoreType.DMA((2, 2)),
        ],
        compiler_params=pltpu.CompilerParams(dimension_semantics=("arbitrary",)),
        name="sb_attention_sample",
    )(qn, knb, vb, k_past, v_past)


def _lru_kernel(gc_ref, xc_ref, buf_ref, h0_ref, cw_ref, cb_ref, wr_ref, br_ref, wi_ref, bi_ref, lam_ref,
                o_ref, conv_ref, hl_ref, *, has_state):
    seq, bw = xc_ref.shape
    x = xc_ref[...]
    row = lax.broadcasted_iota(jnp.int32, (seq, bw), 0)
    cw = cw_ref[...]
    y = cb_ref[...] + cw[CONV_W - 1:CONV_W, :] * x
    for d in range(1, CONV_W):
        xs = pltpu.roll(x, d, 0)
        for r in range(d):
            prev = buf_ref[CONV_W - 1 - d + r:CONV_W - d + r, :] if has_state else jnp.zeros((1, bw), F32)
            xs = jnp.where(row == r, prev, xs)
        y = y + cw[CONV_W - 1 - d:CONV_W - d, :] * xs
    conv_ref[...] = x[seq - (CONV_W - 1):, :]

    r_gate = _sigmoid(_dot3(y, wr_ref[...]) + br_ref[...])
    i_gate = _sigmoid(_dot3(y, wi_ref[...]) + bi_ref[...])
    log_a = -LRU_C * r_gate * _softplus(-lam_ref[...])
    a = jnp.exp(log_a)
    u = jnp.sqrt(-jnp.tanh(log_a) * (a * a + 1.0)) * (i_gate * y)
    if has_state:
        u = jnp.where(row == 0, u + a * h0_ref[...], u)

    shift = 1
    while shift < seq:
        a_s = pltpu.roll(a, shift, 0)
        u_s = pltpu.roll(u, shift, 0)
        live = row >= shift
        u = jnp.where(live, a * u_s + u, u)
        a = jnp.where(live, a * a_s, a)
        shift *= 2

    hl_ref[...] = u[seq - 1:, :]
    gc = gc_ref[...]
    gelu = 0.5 * gc * (1.0 + jnp.tanh(math.sqrt(2.0 / math.pi) * (gc + 0.044715 * gc * gc * gc)))
    o_ref[...] = (gelu * u).astype(BF16)


def lru_mixer(proj, row0, batch, seq, conv_buf, h0, cw, cb, wr, br, wi, bi, lam):
    width = proj.shape[1] // 5
    bw = width // LRU_BLOCKS
    blk0 = row0 // seq
    has_state = conv_buf is not None
    if not has_state:
        conv_buf = jnp.zeros((1, CONV_W - 1, width), F32)
        h0 = jnp.zeros((1, width), F32)
    h0 = h0.reshape(-1, 1, width)

    def vec(k):
        return pl.BlockSpec((k, bw), lambda b, n: (0, n))

    wspec = pl.BlockSpec((None, bw, bw), lambda b, n: (n, 0, 0))
    return pl.pallas_call(
        functools.partial(_lru_kernel, has_state=has_state),
        grid=(batch, LRU_BLOCKS),
        in_specs=[
            pl.BlockSpec((seq, bw), lambda b, n: (blk0 + b, n)),
            pl.BlockSpec((seq, bw), lambda b, n: (blk0 + b, LRU_BLOCKS + n)),
            pl.BlockSpec((None, CONV_W - 1, bw), lambda b, n: (b if has_state else 0, 0, n)),
            pl.BlockSpec((None, 1, bw), lambda b, n: (b if has_state else 0, 0, n)),
            vec(CONV_W), vec(1), wspec, vec(1), wspec, vec(1), vec(1),
        ],
        out_specs=[
            pl.BlockSpec((seq, bw), lambda b, n: (b, n)),
            pl.BlockSpec((None, CONV_W - 1, bw), lambda b, n: (b, 0, n)),
            pl.BlockSpec((None, 1, bw), lambda b, n: (b, 0, n)),
        ],
        out_shape=[
            jax.ShapeDtypeStruct((batch * seq, width), BF16),
            jax.ShapeDtypeStruct((batch, CONV_W - 1, width), F32),
            jax.ShapeDtypeStruct((batch, 1, width), F32),
        ],
        compiler_params=pltpu.CompilerParams(dimension_semantics=("parallel", "parallel")),
        name="rg_lru",
    )(proj, proj, conv_buf, h0, cw, cb.reshape(1, width), wr, br.reshape(1, width), wi,
      bi.reshape(1, width), lam.reshape(1, width))


def _cb_kernel(q_ref, k_ref, v_ref, kp_ref, vp_ref, qg_ref, kg_ref, bias_ref, o_ref, ko_ref, vo_ref,
               kb_ref, vb_ref, *, has_past):
    seq, dh = q_ref.shape
    keep = ko_ref.shape[0]
    nc = seq // CHUNK
    kn = _rms(k_ref[...], kg_ref[...])
    v = v_ref[...]
    ko_ref[...] = kn[seq - keep:, :]
    vo_ref[...] = v[seq - keep:, :]
    if has_past:
        kb_ref[:CB_PAST, :] = kp_ref[...].astype(BF16)
        vb_ref[:CB_PAST, :] = vp_ref[...].astype(BF16)
    else:
        kb_ref[:CB_PAST, :] = jnp.zeros((CB_PAST, dh), BF16)
        vb_ref[:CB_PAST, :] = jnp.zeros((CB_PAST, dh), BF16)
    kb_ref[CB_PAST:, :] = kn.astype(BF16)
    vb_ref[CB_PAST:, :] = v.astype(BF16)
    bias = bias_ref[...]
    first_valid = 0 if has_past else CB_PAST
    kidx = lax.broadcasted_iota(jnp.int32, (CHUNK, CB_BAND), 1)

    def body(n, _):
        r0 = pl.multiple_of(n * CHUNK, CHUNK)
        qn = (_rms(q_ref[pl.ds(r0, CHUNK), :], qg_ref[...]) * (dh ** -0.5)).astype(BF16)
        s = _dot_nt(qn, kb_ref[pl.ds(r0, CB_BAND), :]) + bias
        s = jnp.where(kidx + n * CHUNK >= first_valid, s, NEG_BIG)
        p = jnp.exp(s - jnp.max(s, axis=1, keepdims=True))
        o = _dot(p.astype(BF16), vb_ref[pl.ds(r0, CB_BAND), :]) / jnp.sum(p, axis=1, keepdims=True)
        o_ref[pl.ds(r0, CHUNK), :] = o.astype(BF16)
        return 0

    lax.fori_loop(0, nc, body, 0, unroll=min(nc, 4))


def cb_attention(proj, row0, batch, seq, keep, k_past, v_past, qg, kg, bias):
    half_d = proj.shape[1] // 5
    heads = half_d // LANES
    blk0 = row0 // seq
    has_past = k_past is not None
    if not has_past:
        k_past = jnp.zeros((1, CB_PAST, half_d), F32)
        v_past = k_past

    def col(j):
        return pl.BlockSpec((seq, LANES), lambda b, h: (blk0 + b, j * heads + h))

    old = pl.BlockSpec((None, CB_PAST, LANES), lambda b, h: (b if has_past else 0, 0, h))
    gspec = pl.BlockSpec((1, LANES), lambda b, h: (0, 0))
    kept = pl.BlockSpec((None, keep, LANES), lambda b, h: (b, 0, h))
    return pl.pallas_call(
        functools.partial(_cb_kernel, has_past=has_past),
        grid=(batch, heads),
        in_specs=[col(2), col(3), col(4), old, old, gspec, gspec,
                  pl.BlockSpec((None, CHUNK, CB_BAND), lambda b, h: (h, 0, 0))],
        out_specs=[pl.BlockSpec((seq, LANES), lambda b, h: (b, h)), kept, kept],
        out_shape=[
            jax.ShapeDtypeStruct((batch * seq, half_d), BF16),
            jax.ShapeDtypeStruct((batch, keep, half_d), F32),
            jax.ShapeDtypeStruct((batch, keep, half_d), F32),
        ],
        scratch_shapes=[pltpu.VMEM((CB_PAST + seq, LANES), BF16), pltpu.VMEM((CB_PAST + seq, LANES), BF16)],
        compiler_params=pltpu.CompilerParams(dimension_semantics=("parallel", "parallel")),
        name="cb_attention",
    )(proj, proj, proj, k_past, v_past, qg.reshape(1, LANES), kg.reshape(1, LANES), bias)


def _rope_tables(pos0, seq, half):
    inv = ROPE_BASE ** (-jnp.arange(half, dtype=F32) / half)
    ang = (pos0 + jnp.arange(seq)).astype(F32)[:, None] * inv[None, :]
    return jnp.cos(ang), jnp.sin(ang)


def _band_bias(table):
    lo = -(CHUNK - 1)
    n_flat = CB_PAST + CHUNK - 1 - CB_MAX_REL
    ext = jnp.concatenate([table[:, lo + CB_MAX_REL:], jnp.repeat(table[:, -1:], n_flat, axis=1)], axis=1)
    rev = ext[:, ::-1]
    return jnp.stack([rev[:, CHUNK - 1 - i:CHUNK - 1 - i + CB_BAND] for i in range(CHUNK)], axis=1)


def kernel(x_prompt, x_sample, state_ret, cache_sb_k, cache_sb_v, state_conv, state_lru, cache_cb_k, cache_cb_v, e_norm_mix, e_w_in, e_ret_gn, e_sb_qn, e_sb_kn, e_w_out, e_norm_ffn, e_w1, e_w3, e_w2, o_norm_mix, o_w_in, o_conv_w, o_conv_b, o_lru_wr, o_lru_br, o_lru_wi, o_lru_bi, o_lru_lam, o_cb_qn, o_cb_kn, o_cb_bias, o_w_out, o_norm_ffn, o_router, o_w1, o_w3, o_w2):
    bp, lp, d = x_prompt.shape
    bs, ls, _ = x_sample.shape
    half_d = d // 2
    tp = bp * lp
    past = cache_sb_k.shape[2]
    dk = half_d // RET_HEADS
    x = jnp.concatenate([x_prompt.reshape(tp, d), x_sample.reshape(bs * ls, d)], axis=0)

    qkv = norm_matmul(x, e_norm_mix[0], e_w_in[0].astype(BF16))
    cos_p, sin_p = _rope_tables(0, lp, dk // 2)
    cos_s, sin_s = _rope_tables(past, ls, dk // 2)
    a_p, p_ret = retention_mixer(qkv, 0, bp, lp, cos_p, sin_p, e_ret_gn[0], None)
    a_s, s_ret = retention_mixer(qkv, tp, bs, ls, cos_s, sin_s, e_ret_gn[0], state_ret[0])
    qn, kn, knb, vf, vb = sb_norm(qkv, e_sb_qn[0], e_sb_kn[0])
    b_p = sb_attention_prompt(qn, knb, vb, bp, lp)
    b_s = sb_attention_sample(qn, knb, vb, cache_sb_k[0], cache_sb_v[0], tp, bs, ls)
    x = out_proj(x, jnp.concatenate([a_p, a_s], axis=0), jnp.concatenate([b_p, b_s], axis=0),
                 e_w_out[0].astype(BF16))
    x = ffn(x, e_norm_ffn[0], e_w1[0].astype(BF16), e_w3[0].astype(BF16), e_w2[0].astype(BF16))

    proj = norm_matmul(x, o_norm_mix[0], o_w_in[0].astype(BF16))
    lru_w = (o_conv_w[0], o_conv_b[0], o_lru_wr[0], o_lru_br[0], o_lru_wi[0], o_lru_bi[0], o_lru_lam[0])
    c_p, p_conv, p_lru = lru_mixer(proj, 0, bp, lp, None, None, *lru_w)
    c_s, s_conv, s_lru = lru_mixer(proj, tp, bs, ls, state_conv[0], state_lru[0], *lru_w)
    bias = _band_bias(o_cb_bias[0])
    keep_p = min(CB_PAST, lp)
    d_p, p_cb_k, p_cb_v = cb_attention(proj, 0, bp, lp, keep_p, None, None, o_cb_qn[0], o_cb_kn[0], bias)
    d_s, s_cb_k, s_cb_v = cb_attention(proj, tp, bs, ls, ls, cache_cb_k[0].reshape(bs, CB_PAST, half_d),
                                       cache_cb_v[0].reshape(bs, CB_PAST, half_d), o_cb_qn[0], o_cb_kn[0], bias)
    x = out_proj(x, jnp.concatenate([c_p, c_s], axis=0), jnp.concatenate([d_p, d_s], axis=0),
                 o_w_out[0].astype(BF16))
    x = moe_layer(x, o_norm_ffn[0], o_router[0], o_w1[0], o_w3[0], o_w2[0])

    sbh = (SB_HEADS, half_d // SB_HEADS)
    cbh = (CB_HEADS, half_d // CB_HEADS)
    return (
        x[:tp].reshape(bp, lp, d), x[tp:].reshape(bs, ls, d),
        p_ret[None], kn[:tp].reshape(1, bp, lp, *sbh), vf[:tp].reshape(1, bp, lp, *sbh),
        p_conv[None], p_lru.reshape(1, bp, half_d),
        p_cb_k.reshape(1, bp, keep_p, *cbh), p_cb_v.reshape(1, bp, keep_p, *cbh),
        s_ret[None], kn[tp:].reshape(1, bs, ls, *sbh), vf[tp:].reshape(1, bs, ls, *sbh),
        s_conv[None], s_lru.reshape(1, bs, half_d),
        s_cb_k.reshape(1, bs, ls, *cbh), s_cb_v.reshape(1, bs, ls, *cbh),
    )
```

```python
import functools
import math

import jax
import jax.numpy as jnp
from jax import lax
from jax.experimental import pallas as pl
from jax.experimental.pallas import tpu as pltpu

F32 = jnp.float32
BF16 = jnp.bfloat16

NORM_EPS = 1e-6
ROPE_BASE = 10000.0
CHUNK = 64
RET_HEADS = 4
SB_HEADS = 8
CB_HEADS = 8
LRU_BLOCKS = 8
CONV_W = 4
LRU_C = 8.0
CB_PREV_CHUNKS = 8
CB_PAST = CB_PREV_CHUNKS * CHUNK
CB_BAND = (CB_PREV_CHUNKS + 1) * CHUNK
CB_MAX_REL = 128
NEG_BIG = -1e30
TOP_K = 2
LANES = 128


def _row_tile(m, cap=512):
    t = cap
    while m % t:
        t //= 2
    return t


def _rms(x, g):
    ms = jnp.mean(x * x, axis=-1, keepdims=True)
    return x * lax.rsqrt(ms + NORM_EPS) * g


def _dot(a, b):
    return jnp.dot(a, b, preferred_element_type=F32)


def _dot_nt(a, b):
    return lax.dot_general(a, b, (((1,), (1,)), ((), ())), preferred_element_type=F32)


def _dot_tn(a, b):
    return lax.dot_general(a, b, (((0,), (0,)), ((), ())), preferred_element_type=F32)


def _split(x):
    hi = x.astype(BF16)
    lo = (x - hi.astype(F32)).astype(BF16)
    return hi, lo


def _dot3(a, b):
    a_hi, a_lo = _split(a)
    b_hi, b_lo = _split(b)
    return _dot(a_hi, b_hi) + _dot(a_hi, b_lo) + _dot(a_lo, b_hi)


def _softplus(z):
    return jnp.maximum(z, 0.0) + jnp.log(1.0 + jnp.exp(-jnp.abs(z)))


def _sigmoid(z):
    return 1.0 / (1.0 + jnp.exp(-z))


def _norm_matmul_kernel(x_ref, g_ref, w_ref, o_ref, h_ref):
    @pl.when(pl.program_id(1) == 0)
    def _():
        h_ref[...] = _rms(x_ref[...], g_ref[...]).astype(BF16)

    o_ref[...] = _dot(h_ref[...], w_ref[...])


def norm_matmul(x, g, w, tn=1024):
    m, k = x.shape
    n = w.shape[1]
    tm = _row_tile(m)
    return pl.pallas_call(
        _norm_matmul_kernel,
        grid=(m // tm, n // tn),
        in_specs=[
            pl.BlockSpec((tm, k), lambda i, j: (i, 0)),
            pl.BlockSpec((1, k), lambda i, j: (0, 0)),
            pl.BlockSpec((k, tn), lambda i, j: (0, j)),
        ],
        out_specs=pl.BlockSpec((tm, tn), lambda i, j: (i, j)),
        out_shape=jax.ShapeDtypeStruct((m, n), F32),
        scratch_shapes=[pltpu.VMEM((tm, k), BF16)],
        compiler_params=pltpu.CompilerParams(dimension_semantics=("parallel", "arbitrary")),
        name="norm_matmul",
    )(x, g.reshape(1, k), w)


def _group_specs(shape, n_first):
    return (pl.BlockSpec(shape, lambda i, *_: (jnp.minimum(i, n_first - 1), 0)),
            pl.BlockSpec(shape, lambda i, *_: (jnp.maximum(i - n_first, 0), 0)))


def _out_proj_kernel(x_ref, a1_ref, a2_ref, b1_ref, b2_ref, wa_ref, wb_ref, o_ref, *, n_first):
    first = pl.program_id(0) < n_first

    @pl.when(first)
    def _():
        o_ref[...] = x_ref[...] + _dot(a1_ref[...], wa_ref[...]) + _dot(b1_ref[...], wb_ref[...])

    @pl.when(jnp.logical_not(first))
    def _():
        o_ref[...] = x_ref[...] + _dot(a2_ref[...], wa_ref[...]) + _dot(b2_ref[...], wb_ref[...])


def out_proj(x, a, b, w, tn=1024):
    m, n = x.shape
    kh = a[0].shape[1]
    tm = _row_tile(math.gcd(a[0].shape[0], a[1].shape[0]))
    n_first = a[0].shape[0] // tm
    return pl.pallas_call(
        functools.partial(_out_proj_kernel, n_first=n_first),
        grid=(m // tm, n // tn),
        in_specs=[
            pl.BlockSpec((tm, tn), lambda i, j: (i, j)),
            *_group_specs((tm, kh), n_first),
            *_group_specs((tm, kh), n_first),
            pl.BlockSpec((kh, tn), lambda i, j: (0, j)),
            pl.BlockSpec((kh, tn), lambda i, j: (1, j)),
        ],
        out_specs=pl.BlockSpec((tm, tn), lambda i, j: (i, j)),
        out_shape=jax.ShapeDtypeStruct((m, n), F32),
        compiler_params=pltpu.CompilerParams(dimension_semantics=("parallel", "arbitrary")),
        name="out_proj",
    )(x, *a, *b, w, w)


def _ffn_kernel(x_ref, g_ref, w1_ref, w3_ref, w2_ref, o_ref, h_ref, acc_ref):
    f = pl.program_id(1)

    @pl.when(f == 0)
    def _():
        h_ref[...] = _rms(x_ref[...], g_ref[...]).astype(BF16)
        acc_ref[...] = jnp.zeros_like(acc_ref)

    h = h_ref[...]
    a = _dot(h, w1_ref[...])
    u = _dot(h, w3_ref[...])
    acc_ref[...] += _dot((a * _sigmoid(a) * u).astype(BF16), w2_ref[...])

    @pl.when(f == pl.num_programs(1) - 1)
    def _():
        o_ref[...] = x_ref[...] + acc_ref[...]


def ffn(x, g, w1, w3, w2, tf=512):
    m, d = x.shape
    ff = w1.shape[1]
    tm = _row_tile(m)
    return pl.pallas_call(
        _ffn_kernel,
        grid=(m // tm, ff // tf),
        in_specs=[
            pl.BlockSpec((tm, d), lambda i, f: (i, 0)),
            pl.BlockSpec((1, d), lambda i, f: (0, 0)),
            pl.BlockSpec((d, tf), lambda i, f: (0, f)),
            pl.BlockSpec((d, tf), lambda i, f: (0, f)),
            pl.BlockSpec((tf, d), lambda i, f: (f, 0)),
        ],
        out_specs=pl.BlockSpec((tm, d), lambda i, f: (i, 0)),
        out_shape=jax.ShapeDtypeStruct((m, d), F32),
        scratch_shapes=[pltpu.VMEM((tm, d), BF16), pltpu.VMEM((tm, d), F32)],
        compiler_params=pltpu.CompilerParams(dimension_semantics=("parallel", "arbitrary")),
        name="ffn",
    )(x, g.reshape(1, d), w1, w3, w2)


def _router_kernel(x_ref, g_ref, r_ref, eid_ref, p_ref, rank_ref, cnt_ref, run_ref):
    i = pl.program_id(0)

    @pl.when(i == 0)
    def _():
        run_ref[...] = jnp.zeros_like(run_ref)

    h = _rms(x_ref[...], g_ref[...])
    tm, d = h.shape
    logits = _dot3(h, r_ref[...])
    n_e = logits.shape[1]
    lane = lax.broadcasted_iota(jnp.int32, logits.shape, 1).astype(F32)
    m1 = jnp.max(logits, axis=1, keepdims=True)
    i1 = jnp.min(jnp.where(logits == m1, lane, float(n_e)), axis=1, keepdims=True)
    rest = jnp.where(lane == i1, -jnp.inf, logits)
    m2 = jnp.max(rest, axis=1, keepdims=True)
    i2 = jnp.min(jnp.where(rest == m2, lane, float(n_e)), axis=1, keepdims=True)
    e2 = jnp.exp(m2 - m1)
    p1 = 1.0 / (1.0 + e2)
    slot = lax.broadcasted_iota(jnp.int32, (tm, TOP_K), 1)
    eid_ref[...] = jnp.where(slot == 0, i1, i2).astype(jnp.int32)
    p_ref[...] = jnp.where(slot == 0, p1, e2 * p1)

    hit1 = lane == i1
    hit2 = lane == i2
    both = jnp.where(hit1 | hit2, 1.0, 0.0)
    earlier = (lax.broadcasted_iota(jnp.int32, (tm, tm), 1) < lax.broadcasted_iota(jnp.int32, (tm, tm), 0))
    before = _dot(jnp.where(earlier, 1.0, 0.0).astype(BF16), both.astype(BF16)) + run_ref[...]
    r1 = jnp.sum(jnp.where(hit1, before, 0.0), axis=1, keepdims=True)
    r2 = jnp.sum(jnp.where(hit2, before, 0.0), axis=1, keepdims=True)
    rank_ref[...] = jnp.where(slot == 0, r1, r2).astype(jnp.int32)
    run_ref[...] += jnp.sum(both, axis=0, keepdims=True)
    cnt_ref[...] = run_ref[...].astype(jnp.int32)


def moe_route(x, g, router):
    m, d = x.shape
    n_e = router.shape[1]
    tm = _row_tile(m)
    pair = pl.BlockSpec((tm, TOP_K), lambda i: (i, 0))
    return pl.pallas_call(
        _router_kernel,
        grid=(m // tm,),
        in_specs=[
            pl.BlockSpec((tm, d), lambda i: (i, 0)),
            pl.BlockSpec((1, d), lambda i: (0, 0)),
            pl.BlockSpec((d, n_e), lambda i: (0, 0)),
        ],
        out_specs=[pair, pair, pair, pl.BlockSpec((1, n_e), lambda i: (0, 0))],
        out_shape=[
            jax.ShapeDtypeStruct((m, TOP_K), jnp.int32),
            jax.ShapeDtypeStruct((m, TOP_K), F32),
            jax.ShapeDtypeStruct((m, TOP_K), jnp.int32),
            jax.ShapeDtypeStruct((1, n_e), jnp.int32),
        ],
        scratch_shapes=[pltpu.VMEM((1, n_e), F32)],
        compiler_params=pltpu.CompilerParams(dimension_semantics=("arbitrary",)),
        name="moe_route",
    )(x, g.reshape(1, d), router)


SCALAR_UNROLL = 8


def _row_copy(src, s, dst, t, sem):
    return pltpu.make_async_copy(src.at[pl.ds(s, 1)], dst.at[pl.ds(t, 1)], sem)


def _invert_kernel(pos_ref, src_ref):
    def clear(r, _):
        src_ref[r] = 0
        return 0

    def put(t, _):
        for k in range(TOP_K):
            src_ref[pos_ref[TOP_K * t + k]] = t
        return 0

    lax.fori_loop(0, src_ref.shape[0], clear, 0, unroll=SCALAR_UNROLL)
    lax.fori_loop(0, pos_ref.shape[0] // TOP_K, put, 0, unroll=SCALAR_UNROLL)


def moe_invert(pos, n_rows):
    assert n_rows % SCALAR_UNROLL == 0 and pos.shape[0] % SCALAR_UNROLL == 0
    return pl.pallas_call(
        _invert_kernel,
        in_specs=[pl.BlockSpec(memory_space=pltpu.SMEM)],
        out_specs=pl.BlockSpec(memory_space=pltpu.SMEM),
        out_shape=jax.ShapeDtypeStruct((n_rows,), jnp.int32),
        name="moe_invert",
    )(pos.reshape(-1))


def _gather_kernel(src_ref, nu_ref, x_ref, g_ref, hs_ref, xs_ref, sem):
    r = pl.program_id(0)
    tm = hs_ref.shape[0]

    @pl.when(r < nu_ref[0])
    def _():
        def issue(j, _):
            _row_copy(x_ref, src_ref[r * tm + j], xs_ref, j, sem).start()
            return 0

        def drain(j, _):
            _row_copy(x_ref, 0, xs_ref, 0, sem).wait()
            return 0

        lax.fori_loop(0, tm, issue, 0, unroll=SCALAR_UNROLL)
        lax.fori_loop(0, tm, drain, 0, unroll=SCALAR_UNROLL)
        hs_ref[...] = _rms(xs_ref[...], g_ref[...]).astype(BF16)

    @pl.when(r >= nu_ref[0])
    def _():
        hs_ref[...] = jnp.zeros_like(hs_ref)


def moe_gather(src, n_used, x, g, tm):
    n_rows = src.shape[0]
    d = x.shape[1]
    return pl.pallas_call(
        _gather_kernel,
        grid_spec=pltpu.PrefetchScalarGridSpec(
            num_scalar_prefetch=2,
            grid=(n_rows // tm,),
            in_specs=[pl.BlockSpec(memory_space=pl.ANY), pl.BlockSpec((1, d), lambda r, src, nu: (0, 0))],
            out_specs=pl.BlockSpec((tm, d), lambda r, src, nu: (r, 0)),
            scratch_shapes=[pltpu.VMEM((tm, d), F32), pltpu.SemaphoreType.DMA],
        ),
        out_shape=jax.ShapeDtypeStruct((n_rows, d), BF16),
        compiler_params=pltpu.CompilerParams(dimension_semantics=("arbitrary",)),
        name="moe_gather",
    )(src, n_used, x, g.reshape(1, d))


def _new_expert(te_ref, r):
    return (r == 0) | (te_ref[r] != te_ref[jnp.maximum(r - 1, 0)])


def _moe_up_kernel(te_ref, nu_ref, hs_ref, w1_ref, w3_ref, act_ref, w1b_ref, w3b_ref):
    r = pl.program_id(1)

    @pl.when(_new_expert(te_ref, r))
    def _():
        w1b_ref[...] = w1_ref[...].astype(BF16)
        w3b_ref[...] = w3_ref[...].astype(BF16)

    @pl.when(r < nu_ref[0])
    def _():
        h = hs_ref[...]
        a = _dot(h, w1b_ref[...])
        u = _dot(h, w3b_ref[...])
        act_ref[...] = (a * _sigmoid(a) * u).astype(BF16)

    @pl.when(r >= nu_ref[0])
    def _():
        act_ref[...] = jnp.zeros_like(act_ref)


def _moe_down_kernel(te_ref, nu_ref, act_ref, w2_ref, y_ref, w2b_ref):
    r = pl.program_id(1)

    @pl.when(_new_expert(te_ref, r))
    def _():
        w2b_ref[...] = w2_ref[...].astype(BF16)

    @pl.when(r < nu_ref[0])
    def _():
        y_ref[...] = _dot(act_ref[...], w2b_ref[...])

    @pl.when(r >= nu_ref[0])
    def _():
        y_ref[...] = jnp.zeros_like(y_ref)


def moe_experts(tile_expert, n_used, hs, w1, w3, w2, tm, tf=512, tn=512):
    n_rows, d = hs.shape
    ff = w1.shape[2]
    n_tiles = n_rows // tm

    def row(r, nu):
        return jnp.minimum(r, nu[0] - 1)

    act = pl.pallas_call(
        _moe_up_kernel,
        grid_spec=pltpu.PrefetchScalarGridSpec(
            num_scalar_prefetch=2,
            grid=(ff // tf, n_tiles),
            in_specs=[
                pl.BlockSpec((tm, d), lambda f, r, te, nu: (row(r, nu), 0)),
                pl.BlockSpec((None, d, tf), lambda f, r, te, nu: (te[r], 0, f)),
                pl.BlockSpec((None, d, tf), lambda f, r, te, nu: (te[r], 0, f)),
            ],
            out_specs=pl.BlockSpec((tm, tf), lambda f, r, te, nu: (r, f)),
            scratch_shapes=[pltpu.VMEM((d, tf), BF16), pltpu.VMEM((d, tf), BF16)],
        ),
        out_shape=jax.ShapeDtypeStruct((n_rows, ff), BF16),
        compiler_params=pltpu.CompilerParams(dimension_semantics=("arbitrary", "arbitrary"),
                                             vmem_limit_bytes=48 * 1024 * 1024),
        name="moe_up",
    )(tile_expert, n_used, hs, w1, w3)
    return pl.pallas_call(
        _moe_down_kernel,
        grid_spec=pltpu.PrefetchScalarGridSpec(
            num_scalar_prefetch=2,
            grid=(d // tn, n_tiles),
            in_specs=[
                pl.BlockSpec((tm, ff), lambda n, r, te, nu: (row(r, nu), 0)),
                pl.BlockSpec((None, ff, tn), lambda n, r, te, nu: (te[r], 0, n)),
            ],
            out_specs=pl.BlockSpec((tm, tn), lambda n, r, te, nu: (r, n)),
            scratch_shapes=[pltpu.VMEM((ff, tn), BF16)],
        ),
        out_shape=jax.ShapeDtypeStruct((n_rows, d), F32),
        compiler_params=pltpu.CompilerParams(dimension_semantics=("arbitrary", "arbitrary"),
                                             vmem_limit_bytes=56 * 1024 * 1024),
        name="moe_down",
    )(tile_expert, n_used, act, w2)


def _combine_kernel(pos_ref, x_ref, p_ref, y_ref, o1_ref, o2_ref, ya_ref, yb_ref, sem, *, n_first):
    tm = x_ref.shape[0]
    i = pl.program_id(0)
    t0 = i * tm

    def issue(j, _):
        _row_copy(y_ref, pos_ref[TOP_K * (t0 + j)], ya_ref, j, sem).start()
        _row_copy(y_ref, pos_ref[TOP_K * (t0 + j) + 1], yb_ref, j, sem).start()
        return 0

    def drain(j, _):
        _row_copy(y_ref, 0, ya_ref, 0, sem).wait()
        _row_copy(y_ref, 0, yb_ref, 0, sem).wait()
        return 0

    lax.fori_loop(0, tm, issue, 0, unroll=SCALAR_UNROLL)
    lax.fori_loop(0, tm, drain, 0, unroll=SCALAR_UNROLL)
    p = p_ref[...]
    out = x_ref[...] + p[:, 0:1] * ya_ref[...] + p[:, 1:2] * yb_ref[...]

    @pl.when(i < n_first)
    def _():
        o1_ref[...] = out

    @pl.when(i >= n_first)
    def _():
        o2_ref[...] = out


def moe_combine(pos, x, p, y, m_first, tm=128):
    m, d = x.shape
    assert m_first % tm == 0 and m % tm == 0
    n_first = m_first // tm
    return pl.pallas_call(
        functools.partial(_combine_kernel, n_first=n_first),
        grid_spec=pltpu.PrefetchScalarGridSpec(
            num_scalar_prefetch=1,
            grid=(m // tm,),
            in_specs=[
                pl.BlockSpec((tm, d), lambda i, pos: (i, 0)),
                pl.BlockSpec((tm, TOP_K), lambda i, pos: (i, 0)),
                pl.BlockSpec(memory_space=pl.ANY),
            ],
            out_specs=list(_group_specs((tm, d), n_first)),
            scratch_shapes=[pltpu.VMEM((tm, d), F32), pltpu.VMEM((tm, d), F32), pltpu.SemaphoreType.DMA],
        ),
        out_shape=[jax.ShapeDtypeStruct((m_first, d), F32), jax.ShapeDtypeStruct((m - m_first, d), F32)],
        compiler_params=pltpu.CompilerParams(dimension_semantics=("arbitrary",)),
        name="moe_combine",
    )(pos.reshape(-1), x, p, y)


def moe_layer(x, g, router, w1, w3, w2, m_first, tm=512):
    m = x.shape[0]
    n_e = router.shape[1]
    eid, p, rank, counts = moe_route(x, g, router)
    padded = (counts[0] + tm - 1) // tm * tm
    ends = jnp.cumsum(padded)
    starts = ends - padded
    experts = jnp.arange(n_e, dtype=jnp.int32)
    pos = rank + jnp.sum(jnp.where(eid[:, :, None] == experts, starts, 0), axis=-1)
    n_tiles = -(-TOP_K * m // tm) + n_e
    n_used = (ends[-1] // tm).astype(jnp.int32)
    tile_row = jnp.minimum(jnp.arange(n_tiles, dtype=jnp.int32), n_used - 1) * tm
    tile_expert = jnp.sum(tile_row[:, None] >= ends[None, :], axis=1).astype(jnp.int32)
    n_used = n_used.reshape(1)
    hs = moe_gather(moe_invert(pos, n_tiles * tm), n_used, x, g, tm)
    y = moe_experts(tile_expert, n_used, hs, w1, w3, w2, tm)
    return moe_combine(pos, x, p, y, m_first)


def _ret_kernel(q_ref, k_ref, v_ref, g_ref, cos_ref, sin_ref, gn_ref, s0_ref, o_ref, s_ref, st_ref,
                *, has_state):
    n = pl.program_id(1)
    c = q_ref.shape[0]
    dk = q_ref.shape[1] // RET_HEADS
    half = dk // 2

    @pl.when(n == 0)
    def _():
        if has_state:
            st_ref[...] = s0_ref[...]
        else:
            st_ref[...] = jnp.zeros_like(st_ref)

    cos = cos_ref[...]
    sin = sin_ref[...]
    row = lax.broadcasted_iota(jnp.int32, (c, c), 0)
    col = lax.broadcasted_iota(jnp.int32, (c, c), 1)
    diff = (row - col).astype(F32)
    ridx = lax.broadcasted_iota(jnp.int32, (c, 1), 0).astype(F32)

    def rot(x):
        x1, x2 = x[:, :half], x[:, half:]
        return jnp.concatenate([x1 * cos - x2 * sin, x1 * sin + x2 * cos], axis=1)

    for h in range(RET_HEADS):
        log_g = math.log1p(-(2.0 ** (-5.0 - h)))
        sl = slice(h * dk, (h + 1) * dk)
        qr = rot(q_ref[:, sl]) * (dk ** -0.5)
        kr = rot(k_ref[:, sl])
        vh = v_ref[:, sl].astype(BF16)
        decay = jnp.where(diff >= 0, jnp.exp(log_g * jnp.maximum(diff, 0.0)), 0.0)
        scores = _dot_nt(qr.astype(BF16), kr.astype(BF16)) * decay
        s = st_ref[h]
        q_dec = jnp.exp(log_g * (ridx + 1.0))
        k_dec = jnp.exp(log_g * (c - 1.0 - ridx))
        o = _dot(scores.astype(BF16), vh) + _dot((qr * q_dec).astype(BF16), s.astype(BF16))
        st_ref[h] = s * math.exp(log_g * c) + _dot_tn((kr * k_dec).astype(BF16), vh)
        gh = g_ref[:, sl]
        o_ref[:, sl] = (gh * _sigmoid(gh) * _rms(o, gn_ref[h:h + 1, :])).astype(BF16)

    @pl.when(n == pl.num_programs(1) - 1)
    def _():
        s_ref[...] = st_ref[...]


def retention_mixer(qkv, row0, batch, seq, cos, sin, gn, s0):
    half_d = qkv.shape[1] // 7
    dk = half_d // RET_HEADS
    c = CHUNK if seq % CHUNK == 0 else seq
    nc = seq // c
    blk0 = row0 // c
    has_state = s0 is not None
    if not has_state:
        s0 = jnp.zeros((1, RET_HEADS, dk, dk), F32)

    def col(j):
        return pl.BlockSpec((c, half_d), lambda b, n: (blk0 + b * nc + n, j))

    return pl.pallas_call(
        functools.partial(_ret_kernel, has_state=has_state),
        grid=(batch, nc),
        in_specs=[
            col(0), col(1), col(2), col(3),
            pl.BlockSpec((c, dk // 2), lambda b, n: (n, 0)),
            pl.BlockSpec((c, dk // 2), lambda b, n: (n, 0)),
            pl.BlockSpec((RET_HEADS, dk), lambda b, n: (0, 0)),
            pl.BlockSpec((None, RET_HEADS, dk, dk), lambda b, n: (b if has_state else 0, 0, 0, 0)),
        ],
        out_specs=[
            pl.BlockSpec((c, half_d), lambda b, n: (b * nc + n, 0)),
            pl.BlockSpec((None, RET_HEADS, dk, dk), lambda b, n: (b, 0, 0, 0)),
        ],
        out_shape=[
            jax.ShapeDtypeStruct((batch * seq, half_d), BF16),
            jax.ShapeDtypeStruct((batch, RET_HEADS, dk, dk), F32),
        ],
        scratch_shapes=[pltpu.VMEM((RET_HEADS, dk, dk), F32)],
        compiler_params=pltpu.CompilerParams(dimension_semantics=("parallel", "arbitrary")),
        name="retention",
    )(qkv, qkv, qkv, qkv, cos, sin, gn, s0)


def _sb_norm_kernel(q_ref, k_ref, v_ref, qg_ref, kg_ref, qn_ref, knb_ref, vb_ref, k1_ref, k2_ref, v1_ref, v2_ref,
                    *, n_first):
    dh = qg_ref.shape[1]
    first = pl.program_id(0) < n_first
    for h in range(q_ref.shape[1] // dh):
        sl = slice(h * dh, (h + 1) * dh)
        qn_ref[:, sl] = (_rms(q_ref[:, sl], qg_ref[...]) * (dh ** -0.5)).astype(BF16)
        kn = _rms(k_ref[:, sl], kg_ref[...])
        knb_ref[:, sl] = kn.astype(BF16)

        @pl.when(first)
        def _():
            k1_ref[:, sl] = kn

        @pl.when(jnp.logical_not(first))
        def _():
            k2_ref[:, sl] = kn

    v = v_ref[...]
    vb_ref[...] = v.astype(BF16)

    @pl.when(first)
    def _():
        v1_ref[...] = v

    @pl.when(jnp.logical_not(first))
    def _():
        v2_ref[...] = v


def sb_norm(qkv, qg, kg, m_first):
    m = qkv.shape[0]
    half_d = qkv.shape[1] // 7
    dh = qg.shape[0]
    tm = _row_tile(math.gcd(m_first, m - m_first))
    n_first = m_first // tm

    def col(j):
        return pl.BlockSpec((tm, half_d), lambda i: (i, j))

    out = pl.BlockSpec((tm, half_d), lambda i: (i, 0))
    gspec = pl.BlockSpec((1, dh), lambda i: (0, 0))
    split = _group_specs((tm, half_d), n_first)
    f32_first = jax.ShapeDtypeStruct((m_first, half_d), F32)
    f32_rest = jax.ShapeDtypeStruct((m - m_first, half_d), F32)
    return pl.pallas_call(
        functools.partial(_sb_norm_kernel, n_first=n_first),
        grid=(m // tm,),
        in_specs=[col(4), col(5), col(6), gspec, gspec],
        out_specs=[out] * 3 + [*split, *split],
        out_shape=[jax.ShapeDtypeStruct((m, half_d), BF16)] * 3 + [f32_first, f32_rest, f32_first, f32_rest],
        compiler_params=pltpu.CompilerParams(dimension_semantics=("arbitrary",)),
        name="sb_norm",
    )(qkv, qkv, qkv, qg.reshape(1, dh), kg.reshape(1, dh))


SB_EXIT_LOG = 88.0


def _sb_block(q, kb, vb, carry, acc, strict_diag):
    tq, tk = q.shape[0], kb.shape[0]
    z = _dot_nt(q, kb)
    sp = _softplus(z)
    log_keep = -sp
    if strict_diag:
        mask = (lax.broadcasted_iota(jnp.int32, (tq, tk), 1) < lax.broadcasted_iota(jnp.int32, (tq, tk), 0))
        log_keep = jnp.where(mask, log_keep, 0.0)
    later = (lax.broadcasted_iota(jnp.int32, (tk, tk), 0) > lax.broadcasted_iota(jnp.int32, (tk, tk), 1))
    u = jnp.where(later, 1.0, 0.0).astype(BF16)
    hi, lo = _split(log_keep)
    after = _dot(hi, u) + _dot(lo, u) + carry
    w = jnp.exp(z - sp + after)
    if strict_diag:
        w = jnp.where(mask, w, 0.0)
    return jnp.sum(log_keep, axis=1, keepdims=True), acc + _dot(w.astype(BF16), vb)


def _sb_visit(q_ref, carry_ref, acc_ref, key_block, strict_diag):
    heads = q_ref.shape[1] // LANES
    worst = None
    for h in range(heads):
        sl = slice(h * LANES, (h + 1) * LANES)
        kb, vb = key_block(h, sl)
        tk = kb.shape[0]
        carry = carry_ref[h]
        block_sum, acc = _sb_block(q_ref[:, sl], kb, vb, carry[:, :tk], acc_ref[:, sl], strict_diag)
        carry = carry + block_sum
        carry_ref[h] = carry
        acc_ref[:, sl] = acc
        worst = carry if worst is None else jnp.maximum(worst, carry)
    return jnp.max(worst)


def _sb_prompt_kernel(q_ref, k_ref, v_ref, o_ref, acc_ref, carry_ref):
    i = pl.program_id(1)
    tq = q_ref.shape[0]
    acc_ref[...] = jnp.zeros_like(acc_ref)
    carry_ref[...] = jnp.zeros_like(carry_ref)

    def visit(j, strict_diag):
        s = pl.multiple_of(j * tq, tq)
        return _sb_visit(q_ref, carry_ref, acc_ref,
                         lambda h, sl: (k_ref[pl.ds(s, tq), sl], v_ref[pl.ds(s, tq), sl]), strict_diag)

    top = visit(i, True)
    lax.while_loop(lambda st: (st[0] >= 0) & (st[1] > -SB_EXIT_LOG),
                   lambda st: (st[0] - 1, visit(st[0], False)), (i - 1, top))
    o_ref[...] = acc_ref[...].astype(BF16)


def sb_attention_prompt(qn, knb, vb, batch, seq, tq=LANES):
    width = qn.shape[1]
    nq = seq // tq
    return pl.pallas_call(
        _sb_prompt_kernel,
        grid=(batch, nq),
        in_specs=[
            pl.BlockSpec((tq, width), lambda b, i: (b * nq + i, 0)),
            pl.BlockSpec((seq, width), lambda b, i: (b, 0)),
            pl.BlockSpec((seq, width), lambda b, i: (b, 0)),
        ],
        out_specs=pl.BlockSpec((tq, width), lambda b, i: (b * nq + i, 0)),
        out_shape=jax.ShapeDtypeStruct((batch * seq, width), BF16),
        scratch_shapes=[pltpu.VMEM((tq, width), F32), pltpu.VMEM((width // LANES, tq, LANES), F32)],
        compiler_params=pltpu.CompilerParams(dimension_semantics=("parallel", "arbitrary")),
        name="sb_attention_prompt",
    )(qn, knb, vb)


def _sb_sample_kernel(q_ref, k_ref, v_ref, kp_ref, vp_ref, o_ref, acc_ref, carry_ref, kbuf_ref, vbuf_ref, sem):
    b = pl.program_id(0)
    tk = kbuf_ref.shape[1]
    n_past = kp_ref.shape[1] // tk
    acc_ref[...] = jnp.zeros_like(acc_ref)
    carry_ref[...] = jnp.zeros_like(carry_ref)

    def fetch(j):
        slot = (n_past - 1 - j) % 2
        rows = pl.ds(pl.multiple_of(j * tk, tk), tk)
        return (pltpu.make_async_copy(kp_ref.at[b, rows], kbuf_ref.at[slot], sem.at[0, slot]),
                pltpu.make_async_copy(vp_ref.at[b, rows], vbuf_ref.at[slot], sem.at[1, slot]))

    def start(j):
        for c in fetch(j):
            c.start()

    def wait(j):
        for c in fetch(j):
            c.wait()

    start(n_past - 1)
    top = _sb_visit(q_ref, carry_ref, acc_ref, lambda h, sl: (k_ref[:, sl], v_ref[:, sl]), True)

    def body(st):
        j = st[0]
        slot = (n_past - 1 - j) % 2
        wait(j)

        @pl.when(j >= 1)
        def _():
            start(j - 1)

        top = _sb_visit(q_ref, carry_ref, acc_ref,
                        lambda h, sl: (kbuf_ref[slot, :, h, :].astype(BF16), vbuf_ref[slot, :, h, :].astype(BF16)),
                        False)
        return j - 1, top

    j_end, _ = lax.while_loop(lambda st: (st[0] >= 0) & (st[1] > -SB_EXIT_LOG), body, (n_past - 1, top))

    @pl.when(j_end >= 0)
    def _():
        wait(j_end)

    o_ref[...] = acc_ref[...].astype(BF16)


def sb_attention_sample(qn, knb, vb, k_past, v_past, row0, batch, seq, tk=LANES):
    width = qn.shape[1]
    heads = width // LANES
    blk0 = row0 // seq
    assert k_past.shape[1] % tk == 0 and k_past.shape[2:] == (heads, LANES)
    new = pl.BlockSpec((seq, width), lambda b: (blk0 + b, 0))
    return pl.pallas_call(
        _sb_sample_kernel,
        grid=(batch,),
        in_specs=[new, new, new, pl.BlockSpec(memory_space=pl.ANY), pl.BlockSpec(memory_space=pl.ANY)],
        out_specs=pl.BlockSpec((seq, width), lambda b: (b, 0)),
        out_shape=jax.ShapeDtypeStruct((batch * seq, width), BF16),
        scratch_shapes=[
            pltpu.VMEM((seq, width), F32), pltpu.VMEM((heads, seq, LANES), F32),
            pltpu.VMEM((2, tk, heads, LANES), F32), pltpu.VMEM((2, tk, heads, LANES), F32),
            pltpu.SemaphoreType.DMA((2, 2)),
        ],
        compiler_params=pltpu.CompilerParams(dimension_semantics=("arbitrary",)),
        name="sb_attention_sample",
    )(qn, knb, vb, k_past, v_past)


def _lru_kernel(gc_ref, xc_ref, buf_ref, h0_ref, cw_ref, cb_ref, wr_ref, br_ref, wi_ref, bi_ref, lam_ref,
                o_ref, conv_ref, hl_ref, *, has_state):
    seq, bw = xc_ref.shape
    x = xc_ref[...]
    row = lax.broadcasted_iota(jnp.int32, (seq, bw), 0)
    cw = cw_ref[...]
    y = cb_ref[...] + cw[CONV_W - 1:CONV_W, :] * x
    for d in range(1, CONV_W):
        xs = pltpu.roll(x, d, 0)
        for r in range(d):
            prev = buf_ref[CONV_W - 1 - d + r:CONV_W - d + r, :] if has_state else jnp.zeros((1, bw), F32)
            xs = jnp.where(row == r, prev, xs)
        y = y + cw[CONV_W - 1 - d:CONV_W - d, :] * xs
    conv_ref[...] = x[seq - (CONV_W - 1):, :]

    r_gate = _sigmoid(_dot3(y, wr_ref[...]) + br_ref[...])
    i_gate = _sigmoid(_dot3(y, wi_ref[...]) + bi_ref[...])
    log_a = -LRU_C * r_gate * _softplus(-lam_ref[...])
    a = jnp.exp(log_a)
    u = jnp.sqrt(-jnp.tanh(log_a) * (a * a + 1.0)) * (i_gate * y)
    if has_state:
        u = jnp.where(row == 0, u + a * h0_ref[...], u)

    shift = 1
    while shift < seq:
        a_s = pltpu.roll(a, shift, 0)
        u_s = pltpu.roll(u, shift, 0)
        live = row >= shift
        u = jnp.where(live, a * u_s + u, u)
        a = jnp.where(live, a * a_s, a)
        shift *= 2

    hl_ref[...] = u[seq - 1:, :]
    gc = gc_ref[...]
    gelu = 0.5 * gc * (1.0 + jnp.tanh(math.sqrt(2.0 / math.pi) * (gc + 0.044715 * gc * gc * gc)))
    o_ref[...] = (gelu * u).astype(BF16)


def lru_mixer(proj, row0, batch, seq, conv_buf, h0, cw, cb, wr, br, wi, bi, lam):
    width = proj.shape[1] // 5
    bw = width // LRU_BLOCKS
    blk0 = row0 // seq
    has_state = conv_buf is not None
    if not has_state:
        conv_buf = jnp.zeros((1, CONV_W - 1, width), F32)
        h0 = jnp.zeros((1, width), F32)
    h0 = h0.reshape(-1, 1, width)

    def vec(k):
        return pl.BlockSpec((k, bw), lambda b, n: (0, n))

    wspec = pl.BlockSpec((None, bw, bw), lambda b, n: (n, 0, 0))
    return pl.pallas_call(
        functools.partial(_lru_kernel, has_state=has_state),
        grid=(batch, LRU_BLOCKS),
        in_specs=[
            pl.BlockSpec((seq, bw), lambda b, n: (blk0 + b, n)),
            pl.BlockSpec((seq, bw), lambda b, n: (blk0 + b, LRU_BLOCKS + n)),
            pl.BlockSpec((None, CONV_W - 1, bw), lambda b, n: (b if has_state else 0, 0, n)),
            pl.BlockSpec((None, 1, bw), lambda b, n: (b if has_state else 0, 0, n)),
            vec(CONV_W), vec(1), wspec, vec(1), wspec, vec(1), vec(1),
        ],
        out_specs=[
            pl.BlockSpec((seq, bw), lambda b, n: (b, n)),
            pl.BlockSpec((None, CONV_W - 1, bw), lambda b, n: (b, 0, n)),
            pl.BlockSpec((None, 1, bw), lambda b, n: (b, 0, n)),
        ],
        out_shape=[
            jax.ShapeDtypeStruct((batch * seq, width), BF16),
            jax.ShapeDtypeStruct((batch, CONV_W - 1, width), F32),
            jax.ShapeDtypeStruct((batch, 1, width), F32),
        ],
        compiler_params=pltpu.CompilerParams(dimension_semantics=("parallel", "parallel")),
        name="rg_lru",
    )(proj, proj, conv_buf, h0, cw, cb.reshape(1, width), wr, br.reshape(1, width), wi,
      bi.reshape(1, width), lam.reshape(1, width))


def _cb_kernel(q_ref, k_ref, v_ref, kp_ref, vp_ref, qg_ref, kg_ref, bias_ref, o_ref, ko_ref, vo_ref,
               kb_ref, vb_ref, *, has_past):
    seq, dh = q_ref.shape
    keep = ko_ref.shape[0]
    nc = seq // CHUNK
    kn = _rms(k_ref[...], kg_ref[...])
    v = v_ref[...]
    ko_ref[...] = kn[seq - keep:, :]
    vo_ref[...] = v[seq - keep:, :]
    if has_past:
        kb_ref[:CB_PAST, :] = kp_ref[...].astype(BF16)
        vb_ref[:CB_PAST, :] = vp_ref[...].astype(BF16)
    else:
        kb_ref[:CB_PAST, :] = jnp.zeros((CB_PAST, dh), BF16)
        vb_ref[:CB_PAST, :] = jnp.zeros((CB_PAST, dh), BF16)
    kb_ref[CB_PAST:, :] = kn.astype(BF16)
    vb_ref[CB_PAST:, :] = v.astype(BF16)
    bias = bias_ref[...]
    first_valid = 0 if has_past else CB_PAST
    kidx = lax.broadcasted_iota(jnp.int32, (CHUNK, CB_BAND), 1)

    def body(n, _):
        r0 = pl.multiple_of(n * CHUNK, CHUNK)
        qn = (_rms(q_ref[pl.ds(r0, CHUNK), :], qg_ref[...]) * (dh ** -0.5)).astype(BF16)
        s = _dot_nt(qn, kb_ref[pl.ds(r0, CB_BAND), :]) + bias
        s = jnp.where(kidx + n * CHUNK >= first_valid, s, NEG_BIG)
        p = jnp.exp(s - jnp.max(s, axis=1, keepdims=True))
        o = _dot(p.astype(BF16), vb_ref[pl.ds(r0, CB_BAND), :]) / jnp.sum(p, axis=1, keepdims=True)
        o_ref[pl.ds(r0, CHUNK), :] = o.astype(BF16)
        return 0

    lax.fori_loop(0, nc, body, 0, unroll=min(nc, 4))


def cb_attention(proj, row0, batch, seq, keep, k_past, v_past, qg, kg, bias):
    half_d = proj.shape[1] // 5
    heads = half_d // LANES
    blk0 = row0 // seq
    has_past = k_past is not None
    if not has_past:
        k_past = jnp.zeros((1, CB_PAST, half_d), F32)
        v_past = k_past

    def col(j):
        return pl.BlockSpec((seq, LANES), lambda b, h: (blk0 + b, j * heads + h))

    old = pl.BlockSpec((None, CB_PAST, LANES), lambda b, h: (b if has_past else 0, 0, h))
    gspec = pl.BlockSpec((1, LANES), lambda b, h: (0, 0))
    kept = pl.BlockSpec((None, keep, LANES), lambda b, h: (b, 0, h))
    return pl.pallas_call(
        functools.partial(_cb_kernel, has_past=has_past),
        grid=(batch, heads),
        in_specs=[col(2), col(3), col(4), old, old, gspec, gspec,
                  pl.BlockSpec((None, CHUNK, CB_BAND), lambda b, h: (h, 0, 0))],
        out_specs=[pl.BlockSpec((seq, LANES), lambda b, h: (b, h)), kept, kept],
        out_shape=[
            jax.ShapeDtypeStruct((batch * seq, half_d), BF16),
            jax.ShapeDtypeStruct((batch, keep, half_d), F32),
            jax.ShapeDtypeStruct((batch, keep, half_d), F32),
        ],
        scratch_shapes=[pltpu.VMEM((CB_PAST + seq, LANES), BF16), pltpu.VMEM((CB_PAST + seq, LANES), BF16)],
        compiler_params=pltpu.CompilerParams(dimension_semantics=("parallel", "parallel")),
        name="cb_attention",
    )(proj, proj, proj, k_past, v_past, qg.reshape(1, LANES), kg.reshape(1, LANES), bias)


def _rope_tables(pos0, seq, half):
    inv = ROPE_BASE ** (-jnp.arange(half, dtype=F32) / half)
    ang = (pos0 + jnp.arange(seq)).astype(F32)[:, None] * inv[None, :]
    return jnp.cos(ang), jnp.sin(ang)


def _band_bias(table):
    lo = -(CHUNK - 1)
    n_flat = CB_PAST + CHUNK - 1 - CB_MAX_REL
    ext = jnp.concatenate([table[:, lo + CB_MAX_REL:], jnp.repeat(table[:, -1:], n_flat, axis=1)], axis=1)
    rev = ext[:, ::-1]
    n = rev.shape[1]
    skew = jnp.tile(rev, (1, CHUNK + 1))[:, :CHUNK * (n + 1)].reshape(-1, CHUNK, n + 1)
    return skew[:, ::-1, :CB_BAND]


def kernel(x_prompt, x_sample, state_ret, cache_sb_k, cache_sb_v, state_conv, state_lru, cache_cb_k, cache_cb_v, e_norm_mix, e_w_in, e_ret_gn, e_sb_qn, e_sb_kn, e_w_out, e_norm_ffn, e_w1, e_w3, e_w2, o_norm_mix, o_w_in, o_conv_w, o_conv_b, o_lru_wr, o_lru_br, o_lru_wi, o_lru_bi, o_lru_lam, o_cb_qn, o_cb_kn, o_cb_bias, o_w_out, o_norm_ffn, o_router, o_w1, o_w3, o_w2):
    bp, lp, d = x_prompt.shape
    bs, ls, _ = x_sample.shape
    half_d = d // 2
    tp = bp * lp
    past = cache_sb_k.shape[2]
    dk = half_d // RET_HEADS
    x = jnp.concatenate([x_prompt.reshape(tp, d), x_sample.reshape(bs * ls, d)], axis=0)

    qkv = norm_matmul(x, e_norm_mix[0], e_w_in[0].astype(BF16))
    cos_p, sin_p = _rope_tables(0, lp, dk // 2)
    cos_s, sin_s = _rope_tables(past, ls, dk // 2)
    a_p, p_ret = retention_mixer(qkv, 0, bp, lp, cos_p, sin_p, e_ret_gn[0], None)
    a_s, s_ret = retention_mixer(qkv, tp, bs, ls, cos_s, sin_s, e_ret_gn[0], state_ret[0])
    qn, knb, vb, p_sb_k, s_sb_k, p_sb_v, s_sb_v = sb_norm(qkv, e_sb_qn[0], e_sb_kn[0], tp)
    b_p = sb_attention_prompt(qn, knb, vb, bp, lp)
    b_s = sb_attention_sample(qn, knb, vb, cache_sb_k[0], cache_sb_v[0], tp, bs, ls)
    x = out_proj(x, (a_p, a_s), (b_p, b_s), e_w_out[0].astype(BF16))
    x = ffn(x, e_norm_ffn[0], e_w1[0].astype(BF16), e_w3[0].astype(BF16), e_w2[0].astype(BF16))

    proj = norm_matmul(x, o_norm_mix[0], o_w_in[0].astype(BF16))
    lru_w = (o_conv_w[0], o_conv_b[0], o_lru_wr[0], o_lru_br[0], o_lru_wi[0], o_lru_bi[0], o_lru_lam[0])
    c_p, p_conv, p_lru = lru_mixer(proj, 0, bp, lp, None, None, *lru_w)
    c_s, s_conv, s_lru = lru_mixer(proj, tp, bs, ls, state_conv[0], state_lru[0], *lru_w)
    bias = _band_bias(o_cb_bias[0])
    keep_p = min(CB_PAST, lp)
    d_p, p_cb_k, p_cb_v = cb_attention(proj, 0, bp, lp, keep_p, None, None, o_cb_qn[0], o_cb_kn[0], bias)
    d_s, s_cb_k, s_cb_v = cb_attention(proj, tp, bs, ls, ls, cache_cb_k[0].reshape(bs, CB_PAST, half_d),
                                       cache_cb_v[0].reshape(bs, CB_PAST, half_d), o_cb_qn[0], o_cb_kn[0], bias)
    x = out_proj(x, (c_p, c_s), (d_p, d_s), o_w_out[0].astype(BF16))
    y_p, y_s = moe_layer(x, o_norm_ffn[0], o_router[0], o_w1[0], o_w3[0], o_w2[0], tp)

    sbh = (SB_HEADS, half_d // SB_HEADS)
    cbh = (CB_HEADS, half_d // CB_HEADS)
    return (
        y_p.reshape(bp, lp, d), y_s.reshape(bs, ls, d),
        p_ret[None], p_sb_k.reshape(1, bp, lp, *sbh), p_sb_v.reshape(1, bp, lp, *sbh),
        p_conv[None], p_lru.reshape(1, bp, half_d),
        p_cb_k.reshape(1, bp, keep_p, *cbh), p_cb_v.reshape(1, bp, keep_p, *cbh),
        s_ret[None], s_sb_k.reshape(1, bs, ls, *sbh), s_sb_v.reshape(1, bs, ls, *sbh),
        s_conv[None], s_lru.reshape(1, bs, half_d),
        s_cb_k.reshape(1, bs, ls, *cbh), s_cb_v.reshape(1, bs, ls, *cbh),
    )
```

```python
import functools
import math

import jax
import jax.numpy as jnp
from jax import lax
from jax.experimental import pallas as pl
from jax.experimental.pallas import tpu as pltpu

F32 = jnp.float32
BF16 = jnp.bfloat16

NORM_EPS = 1e-6
ROPE_BASE = 10000.0
CHUNK = 64
RET_HEADS = 4
SB_HEADS = 8
CB_HEADS = 8
LRU_BLOCKS = 8
CONV_W = 4
LRU_C = 8.0
CB_PREV_CHUNKS = 8
CB_PAST = CB_PREV_CHUNKS * CHUNK
CB_BAND = (CB_PREV_CHUNKS + 1) * CHUNK
CB_MAX_REL = 128
NEG_BIG = -1e30
TOP_K = 2
LANES = 128


def _row_tile(m, cap=512):
    t = cap
    while m % t:
        t //= 2
    return t


def _rms(x, g):
    ms = jnp.mean(x * x, axis=-1, keepdims=True)
    return x * lax.rsqrt(ms + NORM_EPS) * g


def _dot(a, b):
    return jnp.dot(a, b, preferred_element_type=F32)


def _dot_nt(a, b):
    return lax.dot_general(a, b, (((1,), (1,)), ((), ())), preferred_element_type=F32)


def _dot_tn(a, b):
    return lax.dot_general(a, b, (((0,), (0,)), ((), ())), preferred_element_type=F32)


def _split(x):
    hi = x.astype(BF16)
    lo = (x - hi.astype(F32)).astype(BF16)
    return hi, lo


def _dot3(a, b):
    a_hi, a_lo = _split(a)
    b_hi, b_lo = _split(b)
    return _dot(a_hi, b_hi) + _dot(a_hi, b_lo) + _dot(a_lo, b_hi)


def _softplus(z):
    return jnp.maximum(z, 0.0) + jnp.log(1.0 + jnp.exp(-jnp.abs(z)))


def _sigmoid(z):
    return 1.0 / (1.0 + jnp.exp(-z))


def _group_specs(shape, n_first):
    return (pl.BlockSpec(shape, lambda i, *_: (jnp.minimum(i, n_first - 1), 0)),
            pl.BlockSpec(shape, lambda i, *_: (jnp.maximum(i - n_first, 0), 0)))


def _group_col_specs(shape, n_first):
    return (pl.BlockSpec(shape, lambda i, j: (jnp.minimum(i, n_first - 1), j)),
            pl.BlockSpec(shape, lambda i, j: (jnp.maximum(i - n_first, 0), j)))


def _as_groups(x, tm=None):
    if not isinstance(x, tuple):
        x = (x, x[:0])
    m = x[0].shape[0] + x[1].shape[0]
    if tm is None:
        tm = _row_tile(math.gcd(x[0].shape[0], x[1].shape[0]))
    rest = x[1] if x[1].shape[0] else x[0]
    return x[0], rest, m, tm, x[0].shape[0] // tm


def _norm_matmul_kernel(x1_ref, x2_ref, g_ref, w_ref, o_ref, h_ref, *, n_first):
    first = pl.program_id(0) < n_first
    start = pl.program_id(1) == 0

    @pl.when(start & first)
    def _():
        h_ref[...] = _rms(x1_ref[...], g_ref[...]).astype(BF16)

    @pl.when(start & jnp.logical_not(first))
    def _():
        h_ref[...] = _rms(x2_ref[...], g_ref[...]).astype(BF16)

    o_ref[...] = _dot(h_ref[...], w_ref[...])


def norm_matmul(x, g, w, tn=1024):
    k, n = w.shape
    x1, x2, m, tm, n_first = _as_groups(x)
    return pl.pallas_call(
        functools.partial(_norm_matmul_kernel, n_first=n_first),
        grid=(m // tm, n // tn),
        in_specs=[
            *_group_specs((tm, k), n_first),
            pl.BlockSpec((1, k), lambda i, j: (0, 0)),
            pl.BlockSpec((k, tn), lambda i, j: (0, j)),
        ],
        out_specs=pl.BlockSpec((tm, tn), lambda i, j: (i, j)),
        out_shape=jax.ShapeDtypeStruct((m, n), F32),
        scratch_shapes=[pltpu.VMEM((tm, k), BF16)],
        compiler_params=pltpu.CompilerParams(dimension_semantics=("parallel", "arbitrary")),
        name="norm_matmul",
    )(x1, x2, g.reshape(1, k), w)


def _out_proj_kernel(x1_ref, x2_ref, a1_ref, a2_ref, b1_ref, b2_ref, wa_ref, wb_ref, o_ref, *, n_first, x_split):
    i = pl.program_id(0)

    @pl.when(i < n_first)
    def _():
        o_ref[...] = x1_ref[...] + _dot(a1_ref[...], wa_ref[...]) + _dot(b1_ref[...], wb_ref[...])

    @pl.when(i >= n_first)
    def _():
        x_ref = x2_ref if x_split else x1_ref
        o_ref[...] = x_ref[...] + _dot(a2_ref[...], wa_ref[...]) + _dot(b2_ref[...], wb_ref[...])


def out_proj(x, a, b, w, tn=1024):
    n = w.shape[1]
    kh = a[0].shape[1]
    a1, a2, m, tm, n_first = _as_groups(a)
    b1, b2 = b
    x1, x2, _, _, x_first = _as_groups(x, tm)
    assert x_first in (n_first, m // tm)
    return pl.pallas_call(
        functools.partial(_out_proj_kernel, n_first=n_first, x_split=x_first == n_first),
        grid=(m // tm, n // tn),
        in_specs=[
            *_group_col_specs((tm, tn), x_first),
            *_group_specs((tm, kh), n_first),
            *_group_specs((tm, kh), n_first),
            pl.BlockSpec((kh, tn), lambda i, j: (0, j)),
            pl.BlockSpec((kh, tn), lambda i, j: (1, j)),
        ],
        out_specs=pl.BlockSpec((tm, tn), lambda i, j: (i, j)),
        out_shape=jax.ShapeDtypeStruct((m, n), F32),
        compiler_params=pltpu.CompilerParams(dimension_semantics=("parallel", "arbitrary")),
        name="out_proj",
    )(x1, x2, a1, a2, b1, b2, w, w)


def _ffn_kernel(x_ref, g_ref, w1_ref, w3_ref, w2_ref, o_ref, h_ref, acc_ref):
    f = pl.program_id(1)

    @pl.when(f == 0)
    def _():
        h_ref[...] = _rms(x_ref[...], g_ref[...]).astype(BF16)
        acc_ref[...] = jnp.zeros_like(acc_ref)

    h = h_ref[...]
    a = _dot(h, w1_ref[...])
    u = _dot(h, w3_ref[...])
    acc_ref[...] += _dot((a * _sigmoid(a) * u).astype(BF16), w2_ref[...])

    @pl.when(f == pl.num_programs(1) - 1)
    def _():
        o_ref[...] = x_ref[...] + acc_ref[...]


def ffn(x, g, w1, w3, w2, tf=512):
    m, d = x.shape
    ff = w1.shape[1]
    tm = _row_tile(m)
    return pl.pallas_call(
        _ffn_kernel,
        grid=(m // tm, ff // tf),
        in_specs=[
            pl.BlockSpec((tm, d), lambda i, f: (i, 0)),
            pl.BlockSpec((1, d), lambda i, f: (0, 0)),
            pl.BlockSpec((d, tf), lambda i, f: (0, f)),
            pl.BlockSpec((d, tf), lambda i, f: (0, f)),
            pl.BlockSpec((tf, d), lambda i, f: (f, 0)),
        ],
        out_specs=pl.BlockSpec((tm, d), lambda i, f: (i, 0)),
        out_shape=jax.ShapeDtypeStruct((m, d), F32),
        scratch_shapes=[pltpu.VMEM((tm, d), BF16), pltpu.VMEM((tm, d), F32)],
        compiler_params=pltpu.CompilerParams(dimension_semantics=("parallel", "arbitrary")),
        name="ffn",
    )(x, g.reshape(1, d), w1, w3, w2)


def _router_kernel(x_ref, g_ref, r_ref, eid_ref, p_ref, rank_ref, cnt_ref, run_ref):
    i = pl.program_id(0)

    @pl.when(i == 0)
    def _():
        run_ref[...] = jnp.zeros_like(run_ref)

    h = _rms(x_ref[...], g_ref[...])
    tm, d = h.shape
    logits = _dot3(h, r_ref[...])
    n_e = logits.shape[1]
    lane = lax.broadcasted_iota(jnp.int32, logits.shape, 1).astype(F32)
    m1 = jnp.max(logits, axis=1, keepdims=True)
    i1 = jnp.min(jnp.where(logits == m1, lane, float(n_e)), axis=1, keepdims=True)
    rest = jnp.where(lane == i1, -jnp.inf, logits)
    m2 = jnp.max(rest, axis=1, keepdims=True)
    i2 = jnp.min(jnp.where(rest == m2, lane, float(n_e)), axis=1, keepdims=True)
    e2 = jnp.exp(m2 - m1)
    p1 = 1.0 / (1.0 + e2)
    slot = lax.broadcasted_iota(jnp.int32, (tm, TOP_K), 1)
    eid_ref[...] = jnp.where(slot == 0, i1, i2).astype(jnp.int32)
    p_ref[...] = jnp.where(slot == 0, p1, e2 * p1)

    hit1 = lane == i1
    hit2 = lane == i2
    both = jnp.where(hit1 | hit2, 1.0, 0.0)
    earlier = (lax.broadcasted_iota(jnp.int32, (tm, tm), 1) < lax.broadcasted_iota(jnp.int32, (tm, tm), 0))
    before = _dot(jnp.where(earlier, 1.0, 0.0).astype(BF16), both.astype(BF16)) + run_ref[...]
    r1 = jnp.sum(jnp.where(hit1, before, 0.0), axis=1, keepdims=True)
    r2 = jnp.sum(jnp.where(hit2, before, 0.0), axis=1, keepdims=True)
    rank_ref[...] = jnp.where(slot == 0, r1, r2).astype(jnp.int32)
    run_ref[...] += jnp.sum(both, axis=0, keepdims=True)
    cnt_ref[...] = run_ref[...].astype(jnp.int32)


def moe_route(x, g, router):
    m, d = x.shape
    n_e = router.shape[1]
    tm = _row_tile(m)
    pair = pl.BlockSpec((tm, TOP_K), lambda i: (i, 0))
    return pl.pallas_call(
        _router_kernel,
        grid=(m // tm,),
        in_specs=[
            pl.BlockSpec((tm, d), lambda i: (i, 0)),
            pl.BlockSpec((1, d), lambda i: (0, 0)),
            pl.BlockSpec((d, n_e), lambda i: (0, 0)),
        ],
        out_specs=[pair, pair, pair, pl.BlockSpec((1, n_e), lambda i: (0, 0))],
        out_shape=[
            jax.ShapeDtypeStruct((m, TOP_K), jnp.int32),
            jax.ShapeDtypeStruct((m, TOP_K), F32),
            jax.ShapeDtypeStruct((m, TOP_K), jnp.int32),
            jax.ShapeDtypeStruct((1, n_e), jnp.int32),
        ],
        scratch_shapes=[pltpu.VMEM((1, n_e), F32)],
        compiler_params=pltpu.CompilerParams(dimension_semantics=("arbitrary",)),
        name="moe_route",
    )(x, g.reshape(1, d), router)


SCALAR_UNROLL = 8


def _row_copy(src, s, dst, t, sem):
    return pltpu.make_async_copy(src.at[pl.ds(s, 1)], dst.at[pl.ds(t, 1)], sem)


def _invert_kernel(pos_ref, src_ref):
    def clear(r, _):
        src_ref[r] = 0
        return 0

    def put(t, _):
        for k in range(TOP_K):
            src_ref[pos_ref[TOP_K * t + k]] = t
        return 0

    lax.fori_loop(0, src_ref.shape[0], clear, 0, unroll=SCALAR_UNROLL)
    lax.fori_loop(0, pos_ref.shape[0] // TOP_K, put, 0, unroll=SCALAR_UNROLL)


def moe_invert(pos, n_rows):
    assert n_rows % SCALAR_UNROLL == 0 and pos.shape[0] % SCALAR_UNROLL == 0
    return pl.pallas_call(
        _invert_kernel,
        in_specs=[pl.BlockSpec(memory_space=pltpu.SMEM)],
        out_specs=pl.BlockSpec(memory_space=pltpu.SMEM),
        out_shape=jax.ShapeDtypeStruct((n_rows,), jnp.int32),
        name="moe_invert",
    )(pos.reshape(-1))


def _gather_kernel(src_ref, nu_ref, x_ref, g_ref, hs_ref, xs_ref, sem):
    r = pl.program_id(0)
    tm = hs_ref.shape[0]
    slot = r % 2

    def issue(tile, into):
        def body(j, _):
            _row_copy(x_ref, src_ref[tile * tm + j], xs_ref.at[into], j, sem.at[into]).start()
            return 0
        lax.fori_loop(0, tm, body, 0, unroll=SCALAR_UNROLL)

    @pl.when(r == 0)
    def _():
        issue(0, 0)

    @pl.when(r + 1 < nu_ref[0])
    def _():
        issue(r + 1, 1 - slot)

    @pl.when(r < nu_ref[0])
    def _():
        def drain(j, _):
            _row_copy(x_ref, 0, xs_ref.at[slot], 0, sem.at[slot]).wait()
            return 0

        lax.fori_loop(0, tm, drain, 0, unroll=SCALAR_UNROLL)
        hs_ref[...] = _rms(xs_ref[slot], g_ref[...]).astype(BF16)

    @pl.when(r >= nu_ref[0])
    def _():
        hs_ref[...] = jnp.zeros_like(hs_ref)


def moe_gather(src, n_used, x, g, tm):
    n_rows = src.shape[0]
    d = x.shape[1]
    return pl.pallas_call(
        _gather_kernel,
        grid_spec=pltpu.PrefetchScalarGridSpec(
            num_scalar_prefetch=2,
            grid=(n_rows // tm,),
            in_specs=[pl.BlockSpec(memory_space=pl.ANY), pl.BlockSpec((1, d), lambda r, src, nu: (0, 0))],
            out_specs=pl.BlockSpec((tm, d), lambda r, src, nu: (r, 0)),
            scratch_shapes=[pltpu.VMEM((2, tm, d), F32), pltpu.SemaphoreType.DMA((2,))],
        ),
        out_shape=jax.ShapeDtypeStruct((n_rows, d), BF16),
        compiler_params=pltpu.CompilerParams(dimension_semantics=("arbitrary",)),
        name="moe_gather",
    )(src, n_used, x, g.reshape(1, d))


def _new_expert(te_ref, r):
    return (r == 0) | (te_ref[r] != te_ref[jnp.maximum(r - 1, 0)])


def _moe_up_kernel(te_ref, nu_ref, hs_ref, w1_ref, w3_ref, act_ref, w1b_ref, w3b_ref):
    r = pl.program_id(1)

    @pl.when(_new_expert(te_ref, r))
    def _():
        w1b_ref[...] = w1_ref[...].astype(BF16)
        w3b_ref[...] = w3_ref[...].astype(BF16)

    @pl.when(r < nu_ref[0])
    def _():
        h = hs_ref[...]
        a = _dot(h, w1b_ref[...])
        u = _dot(h, w3b_ref[...])
        act_ref[...] = (a * _sigmoid(a) * u).astype(BF16)

    @pl.when(r >= nu_ref[0])
    def _():
        act_ref[...] = jnp.zeros_like(act_ref)


def _moe_down_kernel(te_ref, nu_ref, act_ref, w2_ref, y_ref, w2b_ref):
    r = pl.program_id(1)

    @pl.when(_new_expert(te_ref, r))
    def _():
        w2b_ref[...] = w2_ref[...].astype(BF16)

    @pl.when(r < nu_ref[0])
    def _():
        y_ref[...] = _dot(act_ref[...], w2b_ref[...])

    @pl.when(r >= nu_ref[0])
    def _():
        y_ref[...] = jnp.zeros_like(y_ref)


def moe_experts(tile_expert, n_used, hs, w1, w3, w2, tm, tf=512, tn=512):
    n_rows, d = hs.shape
    ff = w1.shape[2]
    n_tiles = n_rows // tm

    def row(r, nu):
        return jnp.minimum(r, nu[0] - 1)

    act = pl.pallas_call(
        _moe_up_kernel,
        grid_spec=pltpu.PrefetchScalarGridSpec(
            num_scalar_prefetch=2,
            grid=(ff // tf, n_tiles),
            in_specs=[
                pl.BlockSpec((tm, d), lambda f, r, te, nu: (row(r, nu), 0)),
                pl.BlockSpec((None, d, tf), lambda f, r, te, nu: (te[r], 0, f)),
                pl.BlockSpec((None, d, tf), lambda f, r, te, nu: (te[r], 0, f)),
            ],
            out_specs=pl.BlockSpec((tm, tf), lambda f, r, te, nu: (r, f)),
            scratch_shapes=[pltpu.VMEM((d, tf), BF16), pltpu.VMEM((d, tf), BF16)],
        ),
        out_shape=jax.ShapeDtypeStruct((n_rows, ff), BF16),
        compiler_params=pltpu.CompilerParams(dimension_semantics=("arbitrary", "arbitrary"),
                                             vmem_limit_bytes=48 * 1024 * 1024),
        name="moe_up",
    )(tile_expert, n_used, hs, w1, w3)
    return pl.pallas_call(
        _moe_down_kernel,
        grid_spec=pltpu.PrefetchScalarGridSpec(
            num_scalar_prefetch=2,
            grid=(d // tn, n_tiles),
            in_specs=[
                pl.BlockSpec((tm, ff), lambda n, r, te, nu: (row(r, nu), 0)),
                pl.BlockSpec((None, ff, tn), lambda n, r, te, nu: (te[r], 0, n)),
            ],
            out_specs=pl.BlockSpec((tm, tn), lambda n, r, te, nu: (r, n)),
            scratch_shapes=[pltpu.VMEM((ff, tn), BF16)],
        ),
        out_shape=jax.ShapeDtypeStruct((n_rows, d), F32),
        compiler_params=pltpu.CompilerParams(dimension_semantics=("arbitrary", "arbitrary"),
                                             vmem_limit_bytes=56 * 1024 * 1024),
        name="moe_down",
    )(tile_expert, n_used, act, w2)


def _combine_kernel(pos_ref, x_ref, p_ref, y_ref, o1_ref, o2_ref, ya_ref, yb_ref, sem, *, n_first):
    tm = x_ref.shape[0]
    i = pl.program_id(0)
    slot = i % 2

    def issue(tile, into):
        def body(j, _):
            t = tile * tm + j
            _row_copy(y_ref, pos_ref[TOP_K * t], ya_ref.at[into], j, sem.at[into]).start()
            _row_copy(y_ref, pos_ref[TOP_K * t + 1], yb_ref.at[into], j, sem.at[into]).start()
            return 0
        lax.fori_loop(0, tm, body, 0, unroll=SCALAR_UNROLL)

    def drain(j, _):
        _row_copy(y_ref, 0, ya_ref.at[slot], 0, sem.at[slot]).wait()
        _row_copy(y_ref, 0, yb_ref.at[slot], 0, sem.at[slot]).wait()
        return 0

    @pl.when(i == 0)
    def _():
        issue(0, 0)

    @pl.when(i + 1 < pl.num_programs(0))
    def _():
        issue(i + 1, 1 - slot)

    lax.fori_loop(0, tm, drain, 0, unroll=SCALAR_UNROLL)
    p = p_ref[...]
    out = x_ref[...] + p[:, 0:1] * ya_ref[slot] + p[:, 1:2] * yb_ref[slot]

    @pl.when(i < n_first)
    def _():
        o1_ref[...] = out

    @pl.when(i >= n_first)
    def _():
        o2_ref[...] = out


def moe_combine(pos, x, p, y, m_first, tm=128):
    m, d = x.shape
    assert m_first % tm == 0 and m % tm == 0
    n_first = m_first // tm
    return pl.pallas_call(
        functools.partial(_combine_kernel, n_first=n_first),
        grid_spec=pltpu.PrefetchScalarGridSpec(
            num_scalar_prefetch=1,
            grid=(m // tm,),
            in_specs=[
                pl.BlockSpec((tm, d), lambda i, pos: (i, 0)),
                pl.BlockSpec((tm, TOP_K), lambda i, pos: (i, 0)),
                pl.BlockSpec(memory_space=pl.ANY),
            ],
            out_specs=list(_group_specs((tm, d), n_first)),
            scratch_shapes=[pltpu.VMEM((2, tm, d), F32), pltpu.VMEM((2, tm, d), F32),
                            pltpu.SemaphoreType.DMA((2,))],
        ),
        out_shape=[jax.ShapeDtypeStruct((m_first, d), F32), jax.ShapeDtypeStruct((m - m_first, d), F32)],
        compiler_params=pltpu.CompilerParams(dimension_semantics=("arbitrary",)),
        name="moe_combine",
    )(pos.reshape(-1), x, p, y)


def moe_layer(x, g, router, w1, w3, w2, m_first, tm=512):
    m = x.shape[0]
    n_e = router.shape[1]
    eid, p, rank, counts = moe_route(x, g, router)
    padded = (counts[0] + tm - 1) // tm * tm
    ends = jnp.cumsum(padded)
    starts = ends - padded
    experts = jnp.arange(n_e, dtype=jnp.int32)
    pos = rank + jnp.sum(jnp.where(eid[:, :, None] == experts, starts, 0), axis=-1)
    n_tiles = -(-TOP_K * m // tm) + n_e
    n_used = (ends[-1] // tm).astype(jnp.int32)
    tile_row = jnp.minimum(jnp.arange(n_tiles, dtype=jnp.int32), n_used - 1) * tm
    tile_expert = jnp.sum(tile_row[:, None] >= ends[None, :], axis=1).astype(jnp.int32)
    n_used = n_used.reshape(1)
    hs = moe_gather(moe_invert(pos, n_tiles * tm), n_used, x, g, tm)
    y = moe_experts(tile_expert, n_used, hs, w1, w3, w2, tm)
    return moe_combine(pos, x, p, y, m_first)


def _ret_kernel(q_ref, k_ref, v_ref, g_ref, cos_ref, sin_ref, gn_ref, s0_ref, o_ref, s_ref, st_ref,
                *, has_state):
    n = pl.program_id(1)
    c = q_ref.shape[0]
    dk = q_ref.shape[1] // RET_HEADS
    half = dk // 2

    @pl.when(n == 0)
    def _():
        if has_state:
            st_ref[...] = s0_ref[...]
        else:
            st_ref[...] = jnp.zeros_like(st_ref)

    cos = cos_ref[...]
    sin = sin_ref[...]
    row = lax.broadcasted_iota(jnp.int32, (c, c), 0)
    col = lax.broadcasted_iota(jnp.int32, (c, c), 1)
    diff = (row - col).astype(F32)
    ridx = lax.broadcasted_iota(jnp.int32, (c, 1), 0).astype(F32)

    def rot(x):
        x1, x2 = x[:, :half], x[:, half:]
        return jnp.concatenate([x1 * cos - x2 * sin, x1 * sin + x2 * cos], axis=1)

    for h in range(RET_HEADS):
        log_g = math.log1p(-(2.0 ** (-5.0 - h)))
        sl = slice(h * dk, (h + 1) * dk)
        qr = rot(q_ref[:, sl]) * (dk ** -0.5)
        kr = rot(k_ref[:, sl])
        vh = v_ref[:, sl].astype(BF16)
        decay = jnp.where(diff >= 0, jnp.exp(log_g * jnp.maximum(diff, 0.0)), 0.0)
        scores = _dot_nt(qr.astype(BF16), kr.astype(BF16)) * decay
        s = st_ref[h]
        q_dec = jnp.exp(log_g * (ridx + 1.0))
        k_dec = jnp.exp(log_g * (c - 1.0 - ridx))
        o = _dot(scores.astype(BF16), vh) + _dot((qr * q_dec).astype(BF16), s.astype(BF16))
        st_ref[h] = s * math.exp(log_g * c) + _dot_tn((kr * k_dec).astype(BF16), vh)
        gh = g_ref[:, sl]
        o_ref[:, sl] = (gh * _sigmoid(gh) * _rms(o, gn_ref[h:h + 1, :])).astype(BF16)

    @pl.when(n == pl.num_programs(1) - 1)
    def _():
        s_ref[...] = st_ref[...]


def retention_mixer(qkv, row0, batch, seq, cos, sin, gn, s0):
    half_d = qkv.shape[1] // 7
    dk = half_d // RET_HEADS
    c = CHUNK if seq % CHUNK == 0 else seq
    nc = seq // c
    blk0 = row0 // c
    has_state = s0 is not None
    if not has_state:
        s0 = jnp.zeros((1, RET_HEADS, dk, dk), F32)

    def col(j):
        return pl.BlockSpec((c, half_d), lambda b, n: (blk0 + b * nc + n, j))

    return pl.pallas_call(
        functools.partial(_ret_kernel, has_state=has_state),
        grid=(batch, nc),
        in_specs=[
            col(0), col(1), col(2), col(3),
            pl.BlockSpec((c, dk // 2), lambda b, n: (n, 0)),
            pl.BlockSpec((c, dk // 2), lambda b, n: (n, 0)),
            pl.BlockSpec((RET_HEADS, dk), lambda b, n: (0, 0)),
            pl.BlockSpec((None, RET_HEADS, dk, dk), lambda b, n: (b if has_state else 0, 0, 0, 0)),
        ],
        out_specs=[
            pl.BlockSpec((c, half_d), lambda b, n: (b * nc + n, 0)),
            pl.BlockSpec((None, RET_HEADS, dk, dk), lambda b, n: (b, 0, 0, 0)),
        ],
        out_shape=[
            jax.ShapeDtypeStruct((batch * seq, half_d), BF16),
            jax.ShapeDtypeStruct((batch, RET_HEADS, dk, dk), F32),
        ],
        scratch_shapes=[pltpu.VMEM((RET_HEADS, dk, dk), F32)],
        compiler_params=pltpu.CompilerParams(dimension_semantics=("parallel", "arbitrary")),
        name="retention",
    )(qkv, qkv, qkv, qkv, cos, sin, gn, s0)


def _sb_norm_kernel(q_ref, k_ref, v_ref, qg_ref, kg_ref, qn_ref, knb_ref, vb_ref, k1_ref, k2_ref, v1_ref, v2_ref,
                    *, n_first):
    dh = qg_ref.shape[1]
    first = pl.program_id(0) < n_first
    for h in range(q_ref.shape[1] // dh):
        sl = slice(h * dh, (h + 1) * dh)
        qn_ref[:, sl] = (_rms(q_ref[:, sl], qg_ref[...]) * (dh ** -0.5)).astype(BF16)
        kn = _rms(k_ref[:, sl], kg_ref[...])
        knb_ref[:, sl] = kn.astype(BF16)

        @pl.when(first)
        def _():
            k1_ref[:, sl] = kn

        @pl.when(jnp.logical_not(first))
        def _():
            k2_ref[:, sl] = kn

    v = v_ref[...]
    vb_ref[...] = v.astype(BF16)

    @pl.when(first)
    def _():
        v1_ref[...] = v

    @pl.when(jnp.logical_not(first))
    def _():
        v2_ref[...] = v


def sb_norm(qkv, qg, kg, m_first):
    m = qkv.shape[0]
    half_d = qkv.shape[1] // 7
    dh = qg.shape[0]
    tm = _row_tile(math.gcd(m_first, m - m_first))
    n_first = m_first // tm

    def col(j):
        return pl.BlockSpec((tm, half_d), lambda i: (i, j))

    out = pl.BlockSpec((tm, half_d), lambda i: (i, 0))
    gspec = pl.BlockSpec((1, dh), lambda i: (0, 0))
    split = _group_specs((tm, half_d), n_first)
    f32_first = jax.ShapeDtypeStruct((m_first, half_d), F32)
    f32_rest = jax.ShapeDtypeStruct((m - m_first, half_d), F32)
    return pl.pallas_call(
        functools.partial(_sb_norm_kernel, n_first=n_first),
        grid=(m // tm,),
        in_specs=[col(4), col(5), col(6), gspec, gspec],
        out_specs=[out] * 3 + [*split, *split],
        out_shape=[jax.ShapeDtypeStruct((m, half_d), BF16)] * 3 + [f32_first, f32_rest, f32_first, f32_rest],
        compiler_params=pltpu.CompilerParams(dimension_semantics=("arbitrary",)),
        name="sb_norm",
    )(qkv, qkv, qkv, qg.reshape(1, dh), kg.reshape(1, dh))


SB_EXIT_LOG = 88.0


def _sb_block(q, kb, vb, carry, acc, strict_diag):
    tq, tk = q.shape[0], kb.shape[0]
    z = _dot_nt(q, kb)
    sp = _softplus(z)
    log_keep = -sp
    if strict_diag:
        mask = (lax.broadcasted_iota(jnp.int32, (tq, tk), 1) < lax.broadcasted_iota(jnp.int32, (tq, tk), 0))
        log_keep = jnp.where(mask, log_keep, 0.0)
    later = (lax.broadcasted_iota(jnp.int32, (tk, tk), 0) > lax.broadcasted_iota(jnp.int32, (tk, tk), 1))
    u = jnp.where(later, 1.0, 0.0).astype(BF16)
    hi, lo = _split(log_keep)
    after = _dot(hi, u) + _dot(lo, u) + carry
    w = jnp.exp(z - sp + after)
    if strict_diag:
        w = jnp.where(mask, w, 0.0)
    return jnp.sum(log_keep, axis=1, keepdims=True), acc + _dot(w.astype(BF16), vb)


def _sb_visit(q_ref, carry_ref, acc_ref, key_block, strict_diag):
    heads = q_ref.shape[1] // LANES
    worst = None
    for h in range(heads):
        sl = slice(h * LANES, (h + 1) * LANES)
        kb, vb = key_block(h, sl)
        tk = kb.shape[0]
        carry = carry_ref[h]
        block_sum, acc = _sb_block(q_ref[:, sl], kb, vb, carry[:, :tk], acc_ref[:, sl], strict_diag)
        carry = carry + block_sum
        carry_ref[h] = carry
        acc_ref[:, sl] = acc
        worst = carry if worst is None else jnp.maximum(worst, carry)
    return jnp.max(worst)


def _sb_prompt_kernel(q_ref, k_ref, v_ref, o_ref, acc_ref, carry_ref):
    i = pl.program_id(1)
    tq = q_ref.shape[0]
    acc_ref[...] = jnp.zeros_like(acc_ref)
    carry_ref[...] = jnp.zeros_like(carry_ref)

    def visit(j, strict_diag):
        s = pl.multiple_of(j * tq, tq)
        return _sb_visit(q_ref, carry_ref, acc_ref,
                         lambda h, sl: (k_ref[pl.ds(s, tq), sl], v_ref[pl.ds(s, tq), sl]), strict_diag)

    top = visit(i, True)
    lax.while_loop(lambda st: (st[0] >= 0) & (st[1] > -SB_EXIT_LOG),
                   lambda st: (st[0] - 1, visit(st[0], False)), (i - 1, top))
    o_ref[...] = acc_ref[...].astype(BF16)


def sb_attention_prompt(qn, knb, vb, batch, seq, tq=LANES):
    width = qn.shape[1]
    nq = seq // tq
    return pl.pallas_call(
        _sb_prompt_kernel,
        grid=(batch, nq),
        in_specs=[
            pl.BlockSpec((tq, width), lambda b, i: (b * nq + i, 0)),
            pl.BlockSpec((seq, width), lambda b, i: (b, 0)),
            pl.BlockSpec((seq, width), lambda b, i: (b, 0)),
        ],
        out_specs=pl.BlockSpec((tq, width), lambda b, i: (b * nq + i, 0)),
        out_shape=jax.ShapeDtypeStruct((batch * seq, width), BF16),
        scratch_shapes=[pltpu.VMEM((tq, width), F32), pltpu.VMEM((width // LANES, tq, LANES), F32)],
        compiler_params=pltpu.CompilerParams(dimension_semantics=("parallel", "arbitrary")),
        name="sb_attention_prompt",
    )(qn, knb, vb)


def _sb_sample_kernel(q_ref, k_ref, v_ref, kp_ref, vp_ref, o_ref, acc_ref, carry_ref, kbuf_ref, vbuf_ref, sem):
    b = pl.program_id(0)
    tk = kbuf_ref.shape[1]
    n_past = kp_ref.shape[1] // tk
    acc_ref[...] = jnp.zeros_like(acc_ref)
    carry_ref[...] = jnp.zeros_like(carry_ref)

    def fetch(j):
        slot = (n_past - 1 - j) % 2
        rows = pl.ds(pl.multiple_of(j * tk, tk), tk)
        return (pltpu.make_async_copy(kp_ref.at[b, rows], kbuf_ref.at[slot], sem.at[0, slot]),
                pltpu.make_async_copy(vp_ref.at[b, rows], vbuf_ref.at[slot], sem.at[1, slot]))

    def start(j):
        for c in fetch(j):
            c.start()

    def wait(j):
        for c in fetch(j):
            c.wait()

    start(n_past - 1)
    top = _sb_visit(q_ref, carry_ref, acc_ref, lambda h, sl: (k_ref[:, sl], v_ref[:, sl]), True)

    def body(st):
        j = st[0]
        slot = (n_past - 1 - j) % 2
        wait(j)

        @pl.when(j >= 1)
        def _():
            start(j - 1)

        top = _sb_visit(q_ref, carry_ref, acc_ref,
                        lambda h, sl: (kbuf_ref[slot, :, h, :].astype(BF16), vbuf_ref[slot, :, h, :].astype(BF16)),
                        False)
        return j - 1, top

    j_end, _ = lax.while_loop(lambda st: (st[0] >= 0) & (st[1] > -SB_EXIT_LOG), body, (n_past - 1, top))

    @pl.when(j_end >= 0)
    def _():
        wait(j_end)

    o_ref[...] = acc_ref[...].astype(BF16)


def sb_attention_sample(qn, knb, vb, k_past, v_past, row0, batch, seq, tk=LANES):
    width = qn.shape[1]
    heads = width // LANES
    blk0 = row0 // seq
    assert k_past.shape[1] % tk == 0 and k_past.shape[2:] == (heads, LANES)
    new = pl.BlockSpec((seq, width), lambda b: (blk0 + b, 0))
    return pl.pallas_call(
        _sb_sample_kernel,
        grid=(batch,),
        in_specs=[new, new, new, pl.BlockSpec(memory_space=pl.ANY), pl.BlockSpec(memory_space=pl.ANY)],
        out_specs=pl.BlockSpec((seq, width), lambda b: (b, 0)),
        out_shape=jax.ShapeDtypeStruct((batch * seq, width), BF16),
        scratch_shapes=[
            pltpu.VMEM((seq, width), F32), pltpu.VMEM((heads, seq, LANES), F32),
            pltpu.VMEM((2, tk, heads, LANES), F32), pltpu.VMEM((2, tk, heads, LANES), F32),
            pltpu.SemaphoreType.DMA((2, 2)),
        ],
        compiler_params=pltpu.CompilerParams(dimension_semantics=("arbitrary",)),
        name="sb_attention_sample",
    )(qn, knb, vb, k_past, v_past)


def _lru_kernel(gc_ref, xc_ref, buf_ref, h0_ref, cw_ref, cb_ref, wr_ref, br_ref, wi_ref, bi_ref, lam_ref,
                o_ref, conv_ref, hl_ref, *, has_state):
    seq, bw = xc_ref.shape
    x = xc_ref[...]
    row = lax.broadcasted_iota(jnp.int32, (seq, bw), 0)
    cw = cw_ref[...]
    y = cb_ref[...] + cw[CONV_W - 1:CONV_W, :] * x
    for d in range(1, CONV_W):
        xs = pltpu.roll(x, d, 0)
        for r in range(d):
            prev = buf_ref[CONV_W - 1 - d + r:CONV_W - d + r, :] if has_state else jnp.zeros((1, bw), F32)
            xs = jnp.where(row == r, prev, xs)
        y = y + cw[CONV_W - 1 - d:CONV_W - d, :] * xs
    conv_ref[...] = x[seq - (CONV_W - 1):, :]

    r_gate = _sigmoid(_dot3(y, wr_ref[...]) + br_ref[...])
    i_gate = _sigmoid(_dot3(y, wi_ref[...]) + bi_ref[...])
    log_a = -LRU_C * r_gate * _softplus(-lam_ref[...])
    a = jnp.exp(log_a)
    u = jnp.sqrt(-jnp.tanh(log_a) * (a * a + 1.0)) * (i_gate * y)
    if has_state:
        u = jnp.where(row == 0, u + a * h0_ref[...], u)

    shift = 1
    while shift < seq:
        a_s = pltpu.roll(a, shift, 0)
        u_s = pltpu.roll(u, shift, 0)
        live = row >= shift
        u = jnp.where(live, a * u_s + u, u)
        a = jnp.where(live, a * a_s, a)
        shift *= 2

    hl_ref[...] = u[seq - 1:, :]
    gc = gc_ref[...]
    gelu = 0.5 * gc * (1.0 + jnp.tanh(math.sqrt(2.0 / math.pi) * (gc + 0.044715 * gc * gc * gc)))
    o_ref[...] = (gelu * u).astype(BF16)


def lru_mixer(proj, row0, batch, seq, conv_buf, h0, cw, cb, wr, br, wi, bi, lam):
    width = proj.shape[1] // 5
    bw = width // LRU_BLOCKS
    blk0 = row0 // seq
    has_state = conv_buf is not None
    if not has_state:
        conv_buf = jnp.zeros((1, CONV_W - 1, width), F32)
        h0 = jnp.zeros((1, width), F32)
    h0 = h0.reshape(-1, 1, width)

    def vec(k):
        return pl.BlockSpec((k, bw), lambda b, n: (0, n))

    wspec = pl.BlockSpec((None, bw, bw), lambda b, n: (n, 0, 0))
    return pl.pallas_call(
        functools.partial(_lru_kernel, has_state=has_state),
        grid=(batch, LRU_BLOCKS),
        in_specs=[
            pl.BlockSpec((seq, bw), lambda b, n: (blk0 + b, n)),
            pl.BlockSpec((seq, bw), lambda b, n: (blk0 + b, LRU_BLOCKS + n)),
            pl.BlockSpec((None, CONV_W - 1, bw), lambda b, n: (b if has_state else 0, 0, n)),
            pl.BlockSpec((None, 1, bw), lambda b, n: (b if has_state else 0, 0, n)),
            vec(CONV_W), vec(1), wspec, vec(1), wspec, vec(1), vec(1),
        ],
        out_specs=[
            pl.BlockSpec((seq, bw), lambda b, n: (b, n)),
            pl.BlockSpec((None, CONV_W - 1, bw), lambda b, n: (b, 0, n)),
            pl.BlockSpec((None, 1, bw), lambda b, n: (b, 0, n)),
        ],
        out_shape=[
            jax.ShapeDtypeStruct((batch * seq, width), BF16),
            jax.ShapeDtypeStruct((batch, CONV_W - 1, width), F32),
            jax.ShapeDtypeStruct((batch, 1, width), F32),
        ],
        compiler_params=pltpu.CompilerParams(dimension_semantics=("parallel", "parallel")),
        name="rg_lru",
    )(proj, proj, conv_buf, h0, cw, cb.reshape(1, width), wr, br.reshape(1, width), wi,
      bi.reshape(1, width), lam.reshape(1, width))


def _cb_kernel(q_ref, k_ref, v_ref, kp_ref, vp_ref, qg_ref, kg_ref, bias_ref, o_ref, ko_ref, vo_ref,
               kb_ref, vb_ref, *, has_past):
    seq, dh = q_ref.shape
    keep = ko_ref.shape[0]
    kn = _rms(k_ref[...], kg_ref[...])
    v = v_ref[...]
    ko_ref[...] = kn[seq - keep:, :]
    vo_ref[...] = v[seq - keep:, :]
    if has_past:
        kb_ref[:CB_PAST, :] = kp_ref[...].astype(BF16)
        vb_ref[:CB_PAST, :] = vp_ref[...].astype(BF16)
    else:
        kb_ref[:CB_PAST, :] = jnp.zeros((CB_PAST, dh), BF16)
        vb_ref[:CB_PAST, :] = jnp.zeros((CB_PAST, dh), BF16)
    kb_ref[CB_PAST:, :] = kn.astype(BF16)
    vb_ref[CB_PAST:, :] = v.astype(BF16)
    tq, span = bias_ref.shape
    bias = bias_ref[...]
    kidx = lax.broadcasted_iota(jnp.int32, (tq, span), 1)

    def body(n, _):
        r0 = pl.multiple_of(n * tq, tq)
        qn = (_rms(q_ref[pl.ds(r0, tq), :], qg_ref[...]) * (dh ** -0.5)).astype(BF16)
        s = _dot_nt(qn, kb_ref[pl.ds(r0, span), :]) + bias
        if not has_past:
            s = jnp.where(kidx + r0 >= CB_PAST, s, NEG_BIG)
        p = jnp.exp(s - jnp.max(s, axis=1, keepdims=True))
        o = _dot(p.astype(BF16), vb_ref[pl.ds(r0, span), :]) / jnp.sum(p, axis=1, keepdims=True)
        o_ref[pl.ds(r0, tq), :] = o.astype(BF16)
        return 0

    steps = seq // tq
    lax.fori_loop(0, steps, body, 0, unroll=min(steps, 2))


def _block_bias(bias, g):
    rows = [jnp.pad(bias, ((0, 0), (0, 0), (c * CHUNK, (g - 1 - c) * CHUNK)), constant_values=NEG_BIG)
            for c in range(g)]
    return jnp.concatenate(rows, axis=1)


def cb_attention(proj, row0, batch, seq, keep, k_past, v_past, qg, kg, bias):
    assert seq % CHUNK == 0
    bias = _block_bias(bias, 2 if (seq // CHUNK) % 2 == 0 else 1)
    half_d = proj.shape[1] // 5
    heads = half_d // LANES
    blk0 = row0 // seq
    has_past = k_past is not None
    if not has_past:
        k_past = jnp.zeros((1, CB_PAST, half_d), F32)
        v_past = k_past

    def col(j):
        return pl.BlockSpec((seq, LANES), lambda b, h: (blk0 + b, j * heads + h))

    old = pl.BlockSpec((None, CB_PAST, LANES), lambda b, h: (b if has_past else 0, 0, h))
    gspec = pl.BlockSpec((1, LANES), lambda b, h: (0, 0))
    kept = pl.BlockSpec((None, keep, LANES), lambda b, h: (b, 0, h))
    return pl.pallas_call(
        functools.partial(_cb_kernel, has_past=has_past),
        grid=(batch, heads),
        in_specs=[col(2), col(3), col(4), old, old, gspec, gspec,
                  pl.BlockSpec((None,) + bias.shape[1:], lambda b, h: (h, 0, 0))],
        out_specs=[pl.BlockSpec((seq, LANES), lambda b, h: (b, h)), kept, kept],
        out_shape=[
            jax.ShapeDtypeStruct((batch * seq, half_d), BF16),
            jax.ShapeDtypeStruct((batch, keep, half_d), F32),
            jax.ShapeDtypeStruct((batch, keep, half_d), F32),
        ],
        scratch_shapes=[pltpu.VMEM((CB_PAST + seq, LANES), BF16), pltpu.VMEM((CB_PAST + seq, LANES), BF16)],
        compiler_params=pltpu.CompilerParams(dimension_semantics=("parallel", "parallel")),
        name="cb_attention",
    )(proj, proj, proj, k_past, v_past, qg.reshape(1, LANES), kg.reshape(1, LANES), bias)


def _rope_tables(pos0, seq, half):
    inv = ROPE_BASE ** (-jnp.arange(half, dtype=F32) / half)
    ang = (pos0 + jnp.arange(seq)).astype(F32)[:, None] * inv[None, :]
    return jnp.cos(ang), jnp.sin(ang)


def _band_bias(table):
    lo = -(CHUNK - 1)
    n_flat = CB_PAST + CHUNK - 1 - CB_MAX_REL
    ext = jnp.concatenate([table[:, lo + CB_MAX_REL:], jnp.repeat(table[:, -1:], n_flat, axis=1)], axis=1)
    rev = ext[:, ::-1]
    n = rev.shape[1]
    skew = jnp.tile(rev, (1, CHUNK + 1))[:, :CHUNK * (n + 1)].reshape(-1, CHUNK, n + 1)
    return skew[:, ::-1, :CB_BAND]


def kernel(x_prompt, x_sample, state_ret, cache_sb_k, cache_sb_v, state_conv, state_lru, cache_cb_k, cache_cb_v, e_norm_mix, e_w_in, e_ret_gn, e_sb_qn, e_sb_kn, e_w_out, e_norm_ffn, e_w1, e_w3, e_w2, o_norm_mix, o_w_in, o_conv_w, o_conv_b, o_lru_wr, o_lru_br, o_lru_wi, o_lru_bi, o_lru_lam, o_cb_qn, o_cb_kn, o_cb_bias, o_w_out, o_norm_ffn, o_router, o_w1, o_w3, o_w2):
    bp, lp, d = x_prompt.shape
    bs, ls, _ = x_sample.shape
    half_d = d // 2
    tp = bp * lp
    past = cache_sb_k.shape[2]
    dk = half_d // RET_HEADS
    x = (x_prompt.reshape(tp, d), x_sample.reshape(bs * ls, d))

    qkv = norm_matmul(x, e_norm_mix[0], e_w_in[0].astype(BF16))
    cos_p, sin_p = _rope_tables(0, lp, dk // 2)
    cos_s, sin_s = _rope_tables(past, ls, dk // 2)
    a_p, p_ret = retention_mixer(qkv, 0, bp, lp, cos_p, sin_p, e_ret_gn[0], None)
    a_s, s_ret = retention_mixer(qkv, tp, bs, ls, cos_s, sin_s, e_ret_gn[0], state_ret[0])
    qn, knb, vb, p_sb_k, s_sb_k, p_sb_v, s_sb_v = sb_norm(qkv, e_sb_qn[0], e_sb_kn[0], tp)
    b_p = sb_attention_prompt(qn, knb, vb, bp, lp)
    b_s = sb_attention_sample(qn, knb, vb, cache_sb_k[0], cache_sb_v[0], tp, bs, ls)
    x = out_proj(x, (a_p, a_s), (b_p, b_s), e_w_out[0].astype(BF16))
    x = ffn(x, e_norm_ffn[0], e_w1[0].astype(BF16), e_w3[0].astype(BF16), e_w2[0].astype(BF16))

    proj = norm_matmul(x, o_norm_mix[0], o_w_in[0].astype(BF16))
    lru_w = (o_conv_w[0], o_conv_b[0], o_lru_wr[0], o_lru_br[0], o_lru_wi[0], o_lru_bi[0], o_lru_lam[0])
    c_p, p_conv, p_lru = lru_mixer(proj, 0, bp, lp, None, None, *lru_w)
    c_s, s_conv, s_lru = lru_mixer(proj, tp, bs, ls, state_conv[0], state_lru[0], *lru_w)
    bias = _band_bias(o_cb_bias[0])
    keep_p = min(CB_PAST, lp)
    d_p, p_cb_k, p_cb_v = cb_attention(proj, 0, bp, lp, keep_p, None, None, o_cb_qn[0], o_cb_kn[0], bias)
    d_s, s_cb_k, s_cb_v = cb_attention(proj, tp, bs, ls, ls, cache_cb_k[0].reshape(bs, CB_PAST, half_d),
                                       cache_cb_v[0].reshape(bs, CB_PAST, half_d), o_cb_qn[0], o_cb_kn[0], bias)
    x = out_proj(x, (c_p, c_s), (d_p, d_s), o_w_out[0].astype(BF16))
    y_p, y_s = moe_layer(x, o_norm_ffn[0], o_router[0], o_w1[0], o_w3[0], o_w2[0], tp)

    sbh = (SB_HEADS, half_d // SB_HEADS)
    cbh = (CB_HEADS, half_d // CB_HEADS)
    return (
        y_p.reshape(bp, lp, d), y_s.reshape(bs, ls, d),
        p_ret[None], p_sb_k.reshape(1, bp, lp, *sbh), p_sb_v.reshape(1, bp, lp, *sbh),
        p_conv[None], p_lru.reshape(1, bp, half_d),
        p_cb_k.reshape(1, bp, keep_p, *cbh), p_cb_v.reshape(1, bp, keep_p, *cbh),
        s_ret[None], s_sb_k.reshape(1, bs, ls, *sbh), s_sb_v.reshape(1, bs, ls, *sbh),
        s_conv[None], s_lru.reshape(1, bs, half_d),
        s_cb_k.reshape(1, bs, ls, *cbh), s_cb_v.reshape(1, bs, ls, *cbh),
    )
```

```python
import functools
import math

import jax
import jax.numpy as jnp
from jax import lax
from jax.experimental import pallas as pl
from jax.experimental.pallas import tpu as pltpu

F32 = jnp.float32
BF16 = jnp.bfloat16

NORM_EPS = 1e-6
ROPE_BASE = 10000.0
CHUNK = 64
RET_HEADS = 4
SB_HEADS = 8
CB_HEADS = 8
LRU_BLOCKS = 8
CONV_W = 4
LRU_C = 8.0
CB_PREV_CHUNKS = 8
CB_PAST = CB_PREV_CHUNKS * CHUNK
CB_BAND = (CB_PREV_CHUNKS + 1) * CHUNK
CB_MAX_REL = 128
NEG_BIG = -1e30
TOP_K = 2
LANES = 128


def _row_tile(m, cap=512):
    t = cap
    while m % t:
        t //= 2
    return t


def _rms(x, g):
    ms = jnp.mean(x * x, axis=-1, keepdims=True)
    return x * lax.rsqrt(ms + NORM_EPS) * g


def _dot(a, b):
    return jnp.dot(a, b, preferred_element_type=F32)


def _dot_nt(a, b):
    return lax.dot_general(a, b, (((1,), (1,)), ((), ())), preferred_element_type=F32)


def _dot_tn(a, b):
    return lax.dot_general(a, b, (((0,), (0,)), ((), ())), preferred_element_type=F32)


def _split(x):
    hi = x.astype(BF16)
    lo = (x - hi.astype(F32)).astype(BF16)
    return hi, lo


def _dot3(a, b):
    a_hi, a_lo = _split(a)
    b_hi, b_lo = _split(b)
    return _dot(a_hi, b_hi) + _dot(a_hi, b_lo) + _dot(a_lo, b_hi)


def _softplus(z):
    return jnp.maximum(z, 0.0) + jnp.log(1.0 + jnp.exp(-jnp.abs(z)))


def _sigmoid(z):
    return 1.0 / (1.0 + jnp.exp(-z))


def _group_specs(shape, n_first):
    return (pl.BlockSpec(shape, lambda i, *_: (jnp.minimum(i, n_first - 1), 0)),
            pl.BlockSpec(shape, lambda i, *_: (jnp.maximum(i - n_first, 0), 0)))


def _group_col_specs(shape, n_first):
    return (pl.BlockSpec(shape, lambda i, j: (jnp.minimum(i, n_first - 1), j)),
            pl.BlockSpec(shape, lambda i, j: (jnp.maximum(i - n_first, 0), j)))


def _as_groups(x, tm=None):
    if not isinstance(x, tuple):
        x = (x, x[:0])
    m = x[0].shape[0] + x[1].shape[0]
    if tm is None:
        tm = _row_tile(math.gcd(x[0].shape[0], x[1].shape[0]))
    rest = x[1] if x[1].shape[0] else x[0]
    return x[0], rest, m, tm, x[0].shape[0] // tm


def _norm_matmul_kernel(x1_ref, x2_ref, g_ref, w_ref, o_ref, h_ref, *, n_first):
    first = pl.program_id(0) < n_first
    start = pl.program_id(1) == 0

    @pl.when(start & first)
    def _():
        h_ref[...] = _rms(x1_ref[...], g_ref[...]).astype(BF16)

    @pl.when(start & jnp.logical_not(first))
    def _():
        h_ref[...] = _rms(x2_ref[...], g_ref[...]).astype(BF16)

    o_ref[...] = _dot(h_ref[...], w_ref[...])


def norm_matmul(x, g, w, tn=1024):
    k, n = w.shape
    x1, x2, m, tm, n_first = _as_groups(x)
    return pl.pallas_call(
        functools.partial(_norm_matmul_kernel, n_first=n_first),
        grid=(m // tm, n // tn),
        in_specs=[
            *_group_specs((tm, k), n_first),
            pl.BlockSpec((1, k), lambda i, j: (0, 0)),
            pl.BlockSpec((k, tn), lambda i, j: (0, j)),
        ],
        out_specs=pl.BlockSpec((tm, tn), lambda i, j: (i, j)),
        out_shape=jax.ShapeDtypeStruct((m, n), F32),
        scratch_shapes=[pltpu.VMEM((tm, k), BF16)],
        compiler_params=pltpu.CompilerParams(dimension_semantics=("parallel", "arbitrary")),
        name="norm_matmul",
    )(x1, x2, g.reshape(1, k), w)


def _out_proj_kernel(x1_ref, x2_ref, a1_ref, a2_ref, b1_ref, b2_ref, wa_ref, wb_ref, o_ref, *, n_first, x_split):
    i = pl.program_id(0)

    @pl.when(i < n_first)
    def _():
        o_ref[...] = x1_ref[...] + _dot(a1_ref[...], wa_ref[...]) + _dot(b1_ref[...], wb_ref[...])

    @pl.when(i >= n_first)
    def _():
        x_ref = x2_ref if x_split else x1_ref
        o_ref[...] = x_ref[...] + _dot(a2_ref[...], wa_ref[...]) + _dot(b2_ref[...], wb_ref[...])


def out_proj(x, a, b, w, tn=1024):
    n = w.shape[1]
    kh = a[0].shape[1]
    a1, a2, m, tm, n_first = _as_groups(a)
    b1, b2 = b
    x1, x2, _, _, x_first = _as_groups(x, tm)
    assert x_first in (n_first, m // tm)
    return pl.pallas_call(
        functools.partial(_out_proj_kernel, n_first=n_first, x_split=x_first == n_first),
        grid=(m // tm, n // tn),
        in_specs=[
            *_group_col_specs((tm, tn), x_first),
            *_group_specs((tm, kh), n_first),
            *_group_specs((tm, kh), n_first),
            pl.BlockSpec((kh, tn), lambda i, j: (0, j)),
            pl.BlockSpec((kh, tn), lambda i, j: (1, j)),
        ],
        out_specs=pl.BlockSpec((tm, tn), lambda i, j: (i, j)),
        out_shape=jax.ShapeDtypeStruct((m, n), F32),
        compiler_params=pltpu.CompilerParams(dimension_semantics=("parallel", "arbitrary")),
        name="out_proj",
    )(x1, x2, a1, a2, b1, b2, w, w)


def _ffn_kernel(x_ref, g_ref, w1_ref, w3_ref, w2_ref, o_ref, h_ref, acc_ref):
    f = pl.program_id(1)

    @pl.when(f == 0)
    def _():
        h_ref[...] = _rms(x_ref[...], g_ref[...]).astype(BF16)
        acc_ref[...] = jnp.zeros_like(acc_ref)

    h = h_ref[...]
    a = _dot(h, w1_ref[...])
    u = _dot(h, w3_ref[...])
    acc_ref[...] += _dot((a * _sigmoid(a) * u).astype(BF16), w2_ref[...])

    @pl.when(f == pl.num_programs(1) - 1)
    def _():
        o_ref[...] = x_ref[...] + acc_ref[...]


def ffn(x, g, w1, w3, w2, tf=512):
    m, d = x.shape
    ff = w1.shape[1]
    tm = _row_tile(m)
    return pl.pallas_call(
        _ffn_kernel,
        grid=(m // tm, ff // tf),
        in_specs=[
            pl.BlockSpec((tm, d), lambda i, f: (i, 0)),
            pl.BlockSpec((1, d), lambda i, f: (0, 0)),
            pl.BlockSpec((d, tf), lambda i, f: (0, f)),
            pl.BlockSpec((d, tf), lambda i, f: (0, f)),
            pl.BlockSpec((tf, d), lambda i, f: (f, 0)),
        ],
        out_specs=pl.BlockSpec((tm, d), lambda i, f: (i, 0)),
        out_shape=jax.ShapeDtypeStruct((m, d), F32),
        scratch_shapes=[pltpu.VMEM((tm, d), BF16), pltpu.VMEM((tm, d), F32)],
        compiler_params=pltpu.CompilerParams(dimension_semantics=("parallel", "arbitrary")),
        name="ffn",
    )(x, g.reshape(1, d), w1, w3, w2)


def _router_kernel(x_ref, g_ref, r_ref, eid_ref, p_ref, rank_ref, cnt_ref, run_ref):
    i = pl.program_id(0)

    @pl.when(i == 0)
    def _():
        run_ref[...] = jnp.zeros_like(run_ref)

    h = _rms(x_ref[...], g_ref[...])
    tm, d = h.shape
    logits = _dot3(h, r_ref[...])
    n_e = logits.shape[1]
    lane = lax.broadcasted_iota(jnp.int32, logits.shape, 1).astype(F32)
    m1 = jnp.max(logits, axis=1, keepdims=True)
    i1 = jnp.min(jnp.where(logits == m1, lane, float(n_e)), axis=1, keepdims=True)
    rest = jnp.where(lane == i1, -jnp.inf, logits)
    m2 = jnp.max(rest, axis=1, keepdims=True)
    i2 = jnp.min(jnp.where(rest == m2, lane, float(n_e)), axis=1, keepdims=True)
    e2 = jnp.exp(m2 - m1)
    p1 = 1.0 / (1.0 + e2)
    slot = lax.broadcasted_iota(jnp.int32, (tm, TOP_K), 1)
    eid_ref[...] = jnp.where(slot == 0, i1, i2).astype(jnp.int32)
    p_ref[...] = jnp.where(slot == 0, p1, e2 * p1)

    hit1 = lane == i1
    hit2 = lane == i2
    both = jnp.where(hit1 | hit2, 1.0, 0.0)
    earlier = (lax.broadcasted_iota(jnp.int32, (tm, tm), 1) < lax.broadcasted_iota(jnp.int32, (tm, tm), 0))
    before = _dot(jnp.where(earlier, 1.0, 0.0).astype(BF16), both.astype(BF16)) + run_ref[...]
    r1 = jnp.sum(jnp.where(hit1, before, 0.0), axis=1, keepdims=True)
    r2 = jnp.sum(jnp.where(hit2, before, 0.0), axis=1, keepdims=True)
    rank_ref[...] = jnp.where(slot == 0, r1, r2).astype(jnp.int32)
    run_ref[...] += jnp.sum(both, axis=0, keepdims=True)
    cnt_ref[...] = run_ref[...].astype(jnp.int32)


def moe_route(x, g, router):
    m, d = x.shape
    n_e = router.shape[1]
    tm = _row_tile(m)
    pair = pl.BlockSpec((tm, TOP_K), lambda i: (i, 0))
    return pl.pallas_call(
        _router_kernel,
        grid=(m // tm,),
        in_specs=[
            pl.BlockSpec((tm, d), lambda i: (i, 0)),
            pl.BlockSpec((1, d), lambda i: (0, 0)),
            pl.BlockSpec((d, n_e), lambda i: (0, 0)),
        ],
        out_specs=[pair, pair, pair, pl.BlockSpec((1, n_e), lambda i: (0, 0))],
        out_shape=[
            jax.ShapeDtypeStruct((m, TOP_K), jnp.int32),
            jax.ShapeDtypeStruct((m, TOP_K), F32),
            jax.ShapeDtypeStruct((m, TOP_K), jnp.int32),
            jax.ShapeDtypeStruct((1, n_e), jnp.int32),
        ],
        scratch_shapes=[pltpu.VMEM((1, n_e), F32)],
        compiler_params=pltpu.CompilerParams(dimension_semantics=("arbitrary",)),
        name="moe_route",
    )(x, g.reshape(1, d), router)


SCALAR_UNROLL = 8


def _row_copy(src, s, dst, t, sem):
    return pltpu.make_async_copy(src.at[pl.ds(s, 1)], dst.at[pl.ds(t, 1)], sem)


def _invert_kernel(pos_ref, src_ref):
    def clear(r, _):
        src_ref[r] = 0
        return 0

    def put(t, _):
        for k in range(TOP_K):
            src_ref[pos_ref[TOP_K * t + k]] = t
        return 0

    lax.fori_loop(0, src_ref.shape[0], clear, 0, unroll=SCALAR_UNROLL)
    lax.fori_loop(0, pos_ref.shape[0] // TOP_K, put, 0, unroll=SCALAR_UNROLL)


def moe_invert(pos, n_rows):
    assert n_rows % SCALAR_UNROLL == 0 and pos.shape[0] % SCALAR_UNROLL == 0
    return pl.pallas_call(
        _invert_kernel,
        in_specs=[pl.BlockSpec(memory_space=pltpu.SMEM)],
        out_specs=pl.BlockSpec(memory_space=pltpu.SMEM),
        out_shape=jax.ShapeDtypeStruct((n_rows,), jnp.int32),
        name="moe_invert",
    )(pos.reshape(-1))


def _gather_kernel(src_ref, nu_ref, x_ref, g_ref, hs_ref, xs_ref, sem):
    r = pl.program_id(0)
    tm = hs_ref.shape[0]
    slot = r % 2

    def issue(tile, into):
        def body(j, _):
            _row_copy(x_ref, src_ref[tile * tm + j], xs_ref.at[into], j, sem.at[into]).start()
            return 0
        lax.fori_loop(0, tm, body, 0, unroll=SCALAR_UNROLL)

    @pl.when(r == 0)
    def _():
        issue(0, 0)

    @pl.when(r + 1 < nu_ref[0])
    def _():
        issue(r + 1, 1 - slot)

    @pl.when(r < nu_ref[0])
    def _():
        def drain(j, _):
            _row_copy(x_ref, 0, xs_ref.at[slot], 0, sem.at[slot]).wait()
            return 0

        lax.fori_loop(0, tm, drain, 0, unroll=SCALAR_UNROLL)
        hs_ref[...] = _rms(xs_ref[slot], g_ref[...]).astype(BF16)

    @pl.when(r >= nu_ref[0])
    def _():
        hs_ref[...] = jnp.zeros_like(hs_ref)


def moe_gather(src, n_used, x, g, tm):
    n_rows = src.shape[0]
    d = x.shape[1]
    return pl.pallas_call(
        _gather_kernel,
        grid_spec=pltpu.PrefetchScalarGridSpec(
            num_scalar_prefetch=2,
            grid=(n_rows // tm,),
            in_specs=[pl.BlockSpec(memory_space=pl.ANY), pl.BlockSpec((1, d), lambda r, src, nu: (0, 0))],
            out_specs=pl.BlockSpec((tm, d), lambda r, src, nu: (r, 0)),
            scratch_shapes=[pltpu.VMEM((2, tm, d), F32), pltpu.SemaphoreType.DMA((2,))],
        ),
        out_shape=jax.ShapeDtypeStruct((n_rows, d), BF16),
        compiler_params=pltpu.CompilerParams(dimension_semantics=("arbitrary",)),
        name="moe_gather",
    )(src, n_used, x, g.reshape(1, d))


def _expert_rows(first_ref, count_ref, src_ref, dst_ref, ibuf_ref, obuf_ref, sem_in, sem_out, compute):
    e = pl.program_id(0)
    c = pl.program_id(1)
    tm = ibuf_ref.shape[1]
    width = obuf_ref.shape[2]
    first = first_ref[e]
    n_real = count_ref[e]
    n_all = jnp.where(e == pl.num_programs(0) - 1, dst_ref.shape[0] // tm - first, n_real)
    cols = pl.ds(pl.multiple_of(c * width, width), width)

    def rows(k):
        return pl.ds(pl.multiple_of((first + k) * tm, tm), tm)

    def load(k, slot):
        return pltpu.make_async_copy(src_ref.at[rows(k)], ibuf_ref.at[slot], sem_in.at[slot])

    def store(k, slot):
        return pltpu.make_async_copy(obuf_ref.at[slot], dst_ref.at[rows(k), cols], sem_out.at[slot])

    @pl.when(n_real > 0)
    def _():
        load(0, 0).start()

    def body(k, _):
        slot = k % 2

        @pl.when(k >= 2)
        def _():
            store(k - 2, slot).wait()

        @pl.when(k < n_real)
        def _():
            load(k, slot).wait()

            @pl.when(k + 1 < n_real)
            def _():
                load(k + 1, 1 - slot).start()

            obuf_ref[slot] = compute(ibuf_ref[slot])

        @pl.when(k >= n_real)
        def _():
            obuf_ref[slot] = jnp.zeros(obuf_ref.shape[1:], obuf_ref.dtype)

        store(k, slot).start()
        return 0

    lax.fori_loop(0, n_all, body, 0)
    for back in (2, 1):
        @pl.when(n_all >= back)
        def _():
            store(n_all - back, (n_all - back) % 2).wait()


def _moe_up_kernel(first_ref, count_ref, hs_ref, w1_ref, w3_ref, act_ref, w1b_ref, w3b_ref, ibuf_ref, obuf_ref,
                   sem_in, sem_out):
    w1b_ref[...] = w1_ref[...].astype(BF16)
    w3b_ref[...] = w3_ref[...].astype(BF16)

    def compute(h):
        a = _dot(h, w1b_ref[...])
        u = _dot(h, w3b_ref[...])
        return (a * _sigmoid(a) * u).astype(BF16)

    _expert_rows(first_ref, count_ref, hs_ref, act_ref, ibuf_ref, obuf_ref, sem_in, sem_out, compute)


def _moe_down_kernel(first_ref, count_ref, act_ref, w2_ref, y_ref, w2b_ref, ibuf_ref, obuf_ref, sem_in, sem_out):
    w2b_ref[...] = w2_ref[...].astype(BF16)
    _expert_rows(first_ref, count_ref, act_ref, y_ref, ibuf_ref, obuf_ref, sem_in, sem_out,
                 lambda a: _dot(a, w2b_ref[...]))


def moe_experts(tile_first, tile_count, hs, w1, w3, w2, tm, tf=512, tn=512):
    n_rows, d = hs.shape
    n_e, _, ff = w1.shape
    any_spec = pl.BlockSpec(memory_space=pl.ANY)
    sems = [pltpu.SemaphoreType.DMA((2,)), pltpu.SemaphoreType.DMA((2,))]
    act = pl.pallas_call(
        _moe_up_kernel,
        grid_spec=pltpu.PrefetchScalarGridSpec(
            num_scalar_prefetch=2,
            grid=(n_e, ff // tf),
            in_specs=[
                any_spec,
                pl.BlockSpec((None, d, tf), lambda e, f, *_: (e, 0, f)),
                pl.BlockSpec((None, d, tf), lambda e, f, *_: (e, 0, f)),
            ],
            out_specs=any_spec,
            scratch_shapes=[pltpu.VMEM((d, tf), BF16), pltpu.VMEM((d, tf), BF16),
                            pltpu.VMEM((2, tm, d), BF16), pltpu.VMEM((2, tm, tf), BF16), *sems],
        ),
        out_shape=jax.ShapeDtypeStruct((n_rows, ff), BF16),
        compiler_params=pltpu.CompilerParams(dimension_semantics=("arbitrary", "arbitrary"),
                                             vmem_limit_bytes=48 * 1024 * 1024),
        name="moe_up",
    )(tile_first, tile_count, hs, w1, w3)
    return pl.pallas_call(
        _moe_down_kernel,
        grid_spec=pltpu.PrefetchScalarGridSpec(
            num_scalar_prefetch=2,
            grid=(n_e, d // tn),
            in_specs=[any_spec, pl.BlockSpec((None, ff, tn), lambda e, n, *_: (e, 0, n))],
            out_specs=any_spec,
            scratch_shapes=[pltpu.VMEM((ff, tn), BF16),
                            pltpu.VMEM((2, tm, ff), BF16), pltpu.VMEM((2, tm, tn), F32), *sems],
        ),
        out_shape=jax.ShapeDtypeStruct((n_rows, d), F32),
        compiler_params=pltpu.CompilerParams(dimension_semantics=("arbitrary", "arbitrary"),
                                             vmem_limit_bytes=56 * 1024 * 1024),
        name="moe_down",
    )(tile_first, tile_count, act, w2)


def _combine_kernel(pos_ref, x_ref, p_ref, y_ref, o1_ref, o2_ref, ya_ref, yb_ref, sem, *, n_first):
    tm = x_ref.shape[0]
    i = pl.program_id(0)
    slot = i % 2

    def issue(tile, into):
        def body(j, _):
            t = tile * tm + j
            _row_copy(y_ref, pos_ref[TOP_K * t], ya_ref.at[into], j, sem.at[into]).start()
            _row_copy(y_ref, pos_ref[TOP_K * t + 1], yb_ref.at[into], j, sem.at[into]).start()
            return 0
        lax.fori_loop(0, tm, body, 0, unroll=SCALAR_UNROLL)

    def drain(j, _):
        _row_copy(y_ref, 0, ya_ref.at[slot], 0, sem.at[slot]).wait()
        _row_copy(y_ref, 0, yb_ref.at[slot], 0, sem.at[slot]).wait()
        return 0

    @pl.when(i == 0)
    def _():
        issue(0, 0)

    @pl.when(i + 1 < pl.num_programs(0))
    def _():
        issue(i + 1, 1 - slot)

    lax.fori_loop(0, tm, drain, 0, unroll=SCALAR_UNROLL)
    p = p_ref[...]
    out = x_ref[...] + p[:, 0:1] * ya_ref[slot] + p[:, 1:2] * yb_ref[slot]

    @pl.when(i < n_first)
    def _():
        o1_ref[...] = out

    @pl.when(i >= n_first)
    def _():
        o2_ref[...] = out


def moe_combine(pos, x, p, y, m_first, tm=128):
    m, d = x.shape
    assert m_first % tm == 0 and m % tm == 0
    n_first = m_first // tm
    return pl.pallas_call(
        functools.partial(_combine_kernel, n_first=n_first),
        grid_spec=pltpu.PrefetchScalarGridSpec(
            num_scalar_prefetch=1,
            grid=(m // tm,),
            in_specs=[
                pl.BlockSpec((tm, d), lambda i, pos: (i, 0)),
                pl.BlockSpec((tm, TOP_K), lambda i, pos: (i, 0)),
                pl.BlockSpec(memory_space=pl.ANY),
            ],
            out_specs=list(_group_specs((tm, d), n_first)),
            scratch_shapes=[pltpu.VMEM((2, tm, d), F32), pltpu.VMEM((2, tm, d), F32),
                            pltpu.SemaphoreType.DMA((2,))],
        ),
        out_shape=[jax.ShapeDtypeStruct((m_first, d), F32), jax.ShapeDtypeStruct((m - m_first, d), F32)],
        compiler_params=pltpu.CompilerParams(dimension_semantics=("arbitrary",)),
        name="moe_combine",
    )(pos.reshape(-1), x, p, y)


def moe_layer(x, g, router, w1, w3, w2, m_first, tm=512):
    m = x.shape[0]
    n_e = router.shape[1]
    eid, p, rank, counts = moe_route(x, g, router)
    padded = (counts[0] + tm - 1) // tm * tm
    ends = jnp.cumsum(padded)
    starts = ends - padded
    experts = jnp.arange(n_e, dtype=jnp.int32)
    pos = rank + jnp.sum(jnp.where(eid[:, :, None] == experts, starts, 0), axis=-1)
    n_tiles = -(-TOP_K * m // tm) + n_e
    n_used = (ends[-1:] // tm).astype(jnp.int32)
    hs = moe_gather(moe_invert(pos, n_tiles * tm), n_used, x, g, tm)
    y = moe_experts((starts // tm).astype(jnp.int32), (padded // tm).astype(jnp.int32), hs, w1, w3, w2, tm)
    return moe_combine(pos, x, p, y, m_first)


def _ret_kernel(q_ref, k_ref, v_ref, g_ref, cos_ref, sin_ref, gn_ref, s0_ref, o_ref, s_ref, st_ref,
                *, has_state):
    n = pl.program_id(1)
    c = q_ref.shape[0]
    dk = q_ref.shape[1] // RET_HEADS
    half = dk // 2

    @pl.when(n == 0)
    def _():
        if has_state:
            st_ref[...] = s0_ref[...]
        else:
            st_ref[...] = jnp.zeros_like(st_ref)

    cos = cos_ref[...]
    sin = sin_ref[...]
    row = lax.broadcasted_iota(jnp.int32, (c, c), 0)
    col = lax.broadcasted_iota(jnp.int32, (c, c), 1)
    diff = (row - col).astype(F32)
    ridx = lax.broadcasted_iota(jnp.int32, (c, 1), 0).astype(F32)

    def rot(x):
        x1, x2 = x[:, :half], x[:, half:]
        return jnp.concatenate([x1 * cos - x2 * sin, x1 * sin + x2 * cos], axis=1)

    for h in range(RET_HEADS):
        log_g = math.log1p(-(2.0 ** (-5.0 - h)))
        sl = slice(h * dk, (h + 1) * dk)
        qr = rot(q_ref[:, sl]) * (dk ** -0.5)
        kr = rot(k_ref[:, sl])
        vh = v_ref[:, sl].astype(BF16)
        decay = jnp.where(diff >= 0, jnp.exp(log_g * jnp.maximum(diff, 0.0)), 0.0)
        scores = _dot_nt(qr.astype(BF16), kr.astype(BF16)) * decay
        s = st_ref[h]
        q_dec = jnp.exp(log_g * (ridx + 1.0))
        k_dec = jnp.exp(log_g * (c - 1.0 - ridx))
        o = _dot(scores.astype(BF16), vh) + _dot((qr * q_dec).astype(BF16), s.astype(BF16))
        st_ref[h] = s * math.exp(log_g * c) + _dot_tn((kr * k_dec).astype(BF16), vh)
        gh = g_ref[:, sl]
        o_ref[:, sl] = (gh * _sigmoid(gh) * _rms(o, gn_ref[h:h + 1, :])).astype(BF16)

    @pl.when(n == pl.num_programs(1) - 1)
    def _():
        s_ref[...] = st_ref[...]


def retention_mixer(qkv, row0, batch, seq, cos, sin, gn, s0):
    half_d = qkv.shape[1] // 7
    dk = half_d // RET_HEADS
    c = CHUNK if seq % CHUNK == 0 else seq
    nc = seq // c
    blk0 = row0 // c
    has_state = s0 is not None
    if not has_state:
        s0 = jnp.zeros((1, RET_HEADS, dk, dk), F32)

    def col(j):
        return pl.BlockSpec((c, half_d), lambda b, n: (blk0 + b * nc + n, j))

    return pl.pallas_call(
        functools.partial(_ret_kernel, has_state=has_state),
        grid=(batch, nc),
        in_specs=[
            col(0), col(1), col(2), col(3),
            pl.BlockSpec((c, dk // 2), lambda b, n: (n, 0)),
            pl.BlockSpec((c, dk // 2), lambda b, n: (n, 0)),
            pl.BlockSpec((RET_HEADS, dk), lambda b, n: (0, 0)),
            pl.BlockSpec((None, RET_HEADS, dk, dk), lambda b, n: (b if has_state else 0, 0, 0, 0)),
        ],
        out_specs=[
            pl.BlockSpec((c, half_d), lambda b, n: (b * nc + n, 0)),
            pl.BlockSpec((None, RET_HEADS, dk, dk), lambda b, n: (b, 0, 0, 0)),
        ],
        out_shape=[
            jax.ShapeDtypeStruct((batch * seq, half_d), BF16),
            jax.ShapeDtypeStruct((batch, RET_HEADS, dk, dk), F32),
        ],
        scratch_shapes=[pltpu.VMEM((RET_HEADS, dk, dk), F32)],
        compiler_params=pltpu.CompilerParams(dimension_semantics=("parallel", "arbitrary")),
        name="retention",
    )(qkv, qkv, qkv, qkv, cos, sin, gn, s0)


def _sb_norm_kernel(q_ref, k_ref, v_ref, qg_ref, kg_ref, qn_ref, knb_ref, vb_ref, k1_ref, k2_ref, v1_ref, v2_ref,
                    *, n_first):
    dh = qg_ref.shape[1]
    first = pl.program_id(0) < n_first
    for h in range(q_ref.shape[1] // dh):
        sl = slice(h * dh, (h + 1) * dh)
        qn_ref[:, sl] = (_rms(q_ref[:, sl], qg_ref[...]) * (dh ** -0.5)).astype(BF16)
        kn = _rms(k_ref[:, sl], kg_ref[...])
        knb_ref[:, sl] = kn.astype(BF16)

        @pl.when(first)
        def _():
            k1_ref[:, sl] = kn

        @pl.when(jnp.logical_not(first))
        def _():
            k2_ref[:, sl] = kn

    v = v_ref[...]
    vb_ref[...] = v.astype(BF16)

    @pl.when(first)
    def _():
        v1_ref[...] = v

    @pl.when(jnp.logical_not(first))
    def _():
        v2_ref[...] = v


def sb_norm(qkv, qg, kg, m_first):
    m = qkv.shape[0]
    half_d = qkv.shape[1] // 7
    dh = qg.shape[0]
    tm = _row_tile(math.gcd(m_first, m - m_first))
    n_first = m_first // tm

    def col(j):
        return pl.BlockSpec((tm, half_d), lambda i: (i, j))

    out = pl.BlockSpec((tm, half_d), lambda i: (i, 0))
    gspec = pl.BlockSpec((1, dh), lambda i: (0, 0))
    split = _group_specs((tm, half_d), n_first)
    f32_first = jax.ShapeDtypeStruct((m_first, half_d), F32)
    f32_rest = jax.ShapeDtypeStruct((m - m_first, half_d), F32)
    return pl.pallas_call(
        functools.partial(_sb_norm_kernel, n_first=n_first),
        grid=(m // tm,),
        in_specs=[col(4), col(5), col(6), gspec, gspec],
        out_specs=[out] * 3 + [*split, *split],
        out_shape=[jax.ShapeDtypeStruct((m, half_d), BF16)] * 3 + [f32_first, f32_rest, f32_first, f32_rest],
        compiler_params=pltpu.CompilerParams(dimension_semantics=("arbitrary",)),
        name="sb_norm",
    )(qkv, qkv, qkv, qg.reshape(1, dh), kg.reshape(1, dh))


SB_EXIT_LOG = 88.0


def _sb_block(q, kb, vb, carry, acc, strict_diag):
    tq, tk = q.shape[0], kb.shape[0]
    z = _dot_nt(q, kb)
    sp = _softplus(z)
    log_keep = -sp
    if strict_diag:
        mask = (lax.broadcasted_iota(jnp.int32, (tq, tk), 1) < lax.broadcasted_iota(jnp.int32, (tq, tk), 0))
        log_keep = jnp.where(mask, log_keep, 0.0)
    later = (lax.broadcasted_iota(jnp.int32, (tk, tk), 0) > lax.broadcasted_iota(jnp.int32, (tk, tk), 1))
    u = jnp.where(later, 1.0, 0.0).astype(BF16)
    hi, lo = _split(log_keep)
    after = _dot(hi, u) + _dot(lo, u) + carry
    w = jnp.exp(z - sp + after)
    if strict_diag:
        w = jnp.where(mask, w, 0.0)
    return jnp.sum(log_keep, axis=1, keepdims=True), acc + _dot(w.astype(BF16), vb)


def _sb_visit(q_ref, carry_ref, acc_ref, key_block, strict_diag):
    heads = q_ref.shape[1] // LANES
    worst = None
    for h in range(heads):
        sl = slice(h * LANES, (h + 1) * LANES)
        kb, vb = key_block(h, sl)
        tk = kb.shape[0]
        carry = carry_ref[h]
        block_sum, acc = _sb_block(q_ref[:, sl], kb, vb, carry[:, :tk], acc_ref[:, sl], strict_diag)
        carry = carry + block_sum
        carry_ref[h] = carry
        acc_ref[:, sl] = acc
        worst = carry if worst is None else jnp.maximum(worst, carry)
    return jnp.max(worst)


def _sb_prompt_kernel(q_ref, k_ref, v_ref, o_ref, acc_ref, carry_ref):
    i = pl.program_id(1)
    tq = q_ref.shape[0]
    acc_ref[...] = jnp.zeros_like(acc_ref)
    carry_ref[...] = jnp.zeros_like(carry_ref)

    def visit(j, strict_diag):
        s = pl.multiple_of(j * tq, tq)
        return _sb_visit(q_ref, carry_ref, acc_ref,
                         lambda h, sl: (k_ref[pl.ds(s, tq), sl], v_ref[pl.ds(s, tq), sl]), strict_diag)

    top = visit(i, True)
    lax.while_loop(lambda st: (st[0] >= 0) & (st[1] > -SB_EXIT_LOG),
                   lambda st: (st[0] - 1, visit(st[0], False)), (i - 1, top))
    o_ref[...] = acc_ref[...].astype(BF16)


def sb_attention_prompt(qn, knb, vb, batch, seq, tq=LANES):
    width = qn.shape[1]
    nq = seq // tq
    return pl.pallas_call(
        _sb_prompt_kernel,
        grid=(batch, nq),
        in_specs=[
            pl.BlockSpec((tq, width), lambda b, i: (b * nq + i, 0)),
            pl.BlockSpec((seq, width), lambda b, i: (b, 0)),
            pl.BlockSpec((seq, width), lambda b, i: (b, 0)),
        ],
        out_specs=pl.BlockSpec((tq, width), lambda b, i: (b * nq + i, 0)),
        out_shape=jax.ShapeDtypeStruct((batch * seq, width), BF16),
        scratch_shapes=[pltpu.VMEM((tq, width), F32), pltpu.VMEM((width // LANES, tq, LANES), F32)],
        compiler_params=pltpu.CompilerParams(dimension_semantics=("parallel", "arbitrary")),
        name="sb_attention_prompt",
    )(qn, knb, vb)


def _sb_sample_kernel(q_ref, k_ref, v_ref, kp_ref, vp_ref, o_ref, acc_ref, carry_ref, kbuf_ref, vbuf_ref, sem):
    b = pl.program_id(0)
    tk = kbuf_ref.shape[1]
    n_past = kp_ref.shape[1] // tk
    acc_ref[...] = jnp.zeros_like(acc_ref)
    carry_ref[...] = jnp.zeros_like(carry_ref)

    def fetch(j):
        slot = (n_past - 1 - j) % 2
        rows = pl.ds(pl.multiple_of(j * tk, tk), tk)
        return (pltpu.make_async_copy(kp_ref.at[b, rows], kbuf_ref.at[slot], sem.at[0, slot]),
                pltpu.make_async_copy(vp_ref.at[b, rows], vbuf_ref.at[slot], sem.at[1, slot]))

    def start(j):
        for c in fetch(j):
            c.start()

    def wait(j):
        for c in fetch(j):
            c.wait()

    start(n_past - 1)
    top = _sb_visit(q_ref, carry_ref, acc_ref, lambda h, sl: (k_ref[:, sl], v_ref[:, sl]), True)

    def body(st):
        j = st[0]
        slot = (n_past - 1 - j) % 2
        wait(j)

        @pl.when(j >= 1)
        def _():
            start(j - 1)

        top = _sb_visit(q_ref, carry_ref, acc_ref,
                        lambda h, sl: (kbuf_ref[slot, :, h, :].astype(BF16), vbuf_ref[slot, :, h, :].astype(BF16)),
                        False)
        return j - 1, top

    j_end, _ = lax.while_loop(lambda st: (st[0] >= 0) & (st[1] > -SB_EXIT_LOG), body, (n_past - 1, top))

    @pl.when(j_end >= 0)
    def _():
        wait(j_end)

    o_ref[...] = acc_ref[...].astype(BF16)


def sb_attention_sample(qn, knb, vb, k_past, v_past, row0, batch, seq, tk=LANES):
    width = qn.shape[1]
    heads = width // LANES
    blk0 = row0 // seq
    assert k_past.shape[1] % tk == 0 and k_past.shape[2:] == (heads, LANES)
    new = pl.BlockSpec((seq, width), lambda b: (blk0 + b, 0))
    return pl.pallas_call(
        _sb_sample_kernel,
        grid=(batch,),
        in_specs=[new, new, new, pl.BlockSpec(memory_space=pl.ANY), pl.BlockSpec(memory_space=pl.ANY)],
        out_specs=pl.BlockSpec((seq, width), lambda b: (b, 0)),
        out_shape=jax.ShapeDtypeStruct((batch * seq, width), BF16),
        scratch_shapes=[
            pltpu.VMEM((seq, width), F32), pltpu.VMEM((heads, seq, LANES), F32),
            pltpu.VMEM((2, tk, heads, LANES), F32), pltpu.VMEM((2, tk, heads, LANES), F32),
            pltpu.SemaphoreType.DMA((2, 2)),
        ],
        compiler_params=pltpu.CompilerParams(dimension_semantics=("arbitrary",)),
        name="sb_attention_sample",
    )(qn, knb, vb, k_past, v_past)


def _lru_kernel(gc_ref, xc_ref, buf_ref, h0_ref, cw_ref, cb_ref, wr_ref, br_ref, wi_ref, bi_ref, lam_ref,
                o_ref, conv_ref, hl_ref, *, has_state):
    seq, bw = xc_ref.shape
    x = xc_ref[...]
    row = lax.broadcasted_iota(jnp.int32, (seq, bw), 0)
    cw = cw_ref[...]
    y = cb_ref[...] + cw[CONV_W - 1:CONV_W, :] * x
    for d in range(1, CONV_W):
        xs = pltpu.roll(x, d, 0)
        for r in range(d):
            prev = buf_ref[CONV_W - 1 - d + r:CONV_W - d + r, :] if has_state else jnp.zeros((1, bw), F32)
            xs = jnp.where(row == r, prev, xs)
        y = y + cw[CONV_W - 1 - d:CONV_W - d, :] * xs
    conv_ref[...] = x[seq - (CONV_W - 1):, :]

    r_gate = _sigmoid(_dot3(y, wr_ref[...]) + br_ref[...])
    i_gate = _sigmoid(_dot3(y, wi_ref[...]) + bi_ref[...])
    log_a = -LRU_C * r_gate * _softplus(-lam_ref[...])
    a = jnp.exp(log_a)
    u = jnp.sqrt(-jnp.tanh(log_a) * (a * a + 1.0)) * (i_gate * y)
    if has_state:
        u = jnp.where(row == 0, u + a * h0_ref[...], u)

    shift = 1
    while shift < seq:
        a_s = pltpu.roll(a, shift, 0)
        u_s = pltpu.roll(u, shift, 0)
        live = row >= shift
        u = jnp.where(live, a * u_s + u, u)
        a = jnp.where(live, a * a_s, a)
        shift *= 2

    hl_ref[...] = u[seq - 1:, :]
    gc = gc_ref[...]
    gelu = 0.5 * gc * (1.0 + jnp.tanh(math.sqrt(2.0 / math.pi) * (gc + 0.044715 * gc * gc * gc)))
    o_ref[...] = (gelu * u).astype(BF16)


def lru_mixer(proj, row0, batch, seq, conv_buf, h0, cw, cb, wr, br, wi, bi, lam):
    width = proj.shape[1] // 5
    bw = width // LRU_BLOCKS
    blk0 = row0 // seq
    has_state = conv_buf is not None
    if not has_state:
        conv_buf = jnp.zeros((1, CONV_W - 1, width), F32)
        h0 = jnp.zeros((1, width), F32)
    h0 = h0.reshape(-1, 1, width)

    def vec(k):
        return pl.BlockSpec((k, bw), lambda b, n: (0, n))

    wspec = pl.BlockSpec((None, bw, bw), lambda b, n: (n, 0, 0))
    return pl.pallas_call(
        functools.partial(_lru_kernel, has_state=has_state),
        grid=(batch, LRU_BLOCKS),
        in_specs=[
            pl.BlockSpec((seq, bw), lambda b, n: (blk0 + b, n)),
            pl.BlockSpec((seq, bw), lambda b, n: (blk0 + b, LRU_BLOCKS + n)),
            pl.BlockSpec((None, CONV_W - 1, bw), lambda b, n: (b if has_state else 0, 0, n)),
            pl.BlockSpec((None, 1, bw), lambda b, n: (b if has_state else 0, 0, n)),
            vec(CONV_W), vec(1), wspec, vec(1), wspec, vec(1), vec(1),
        ],
        out_specs=[
            pl.BlockSpec((seq, bw), lambda b, n: (b, n)),
            pl.BlockSpec((None, CONV_W - 1, bw), lambda b, n: (b, 0, n)),
            pl.BlockSpec((None, 1, bw), lambda b, n: (b, 0, n)),
        ],
        out_shape=[
            jax.ShapeDtypeStruct((batch * seq, width), BF16),
            jax.ShapeDtypeStruct((batch, CONV_W - 1, width), F32),
            jax.ShapeDtypeStruct((batch, 1, width), F32),
        ],
        compiler_params=pltpu.CompilerParams(dimension_semantics=("parallel", "parallel")),
        name="rg_lru",
    )(proj, proj, conv_buf, h0, cw, cb.reshape(1, width), wr, br.reshape(1, width), wi,
      bi.reshape(1, width), lam.reshape(1, width))


def _cb_kernel(q_ref, k_ref, v_ref, kp_ref, vp_ref, qg_ref, kg_ref, bias_ref, o_ref, ko_ref, vo_ref,
               kb_ref, vb_ref, *, has_past):
    seq, dh = q_ref.shape
    keep = ko_ref.shape[0]
    kn = _rms(k_ref[...], kg_ref[...])
    v = v_ref[...]
    ko_ref[...] = kn[seq - keep:, :]
    vo_ref[...] = v[seq - keep:, :]
    if has_past:
        kb_ref[:CB_PAST, :] = kp_ref[...].astype(BF16)
        vb_ref[:CB_PAST, :] = vp_ref[...].astype(BF16)
    else:
        kb_ref[:CB_PAST, :] = jnp.zeros((CB_PAST, dh), BF16)
        vb_ref[:CB_PAST, :] = jnp.zeros((CB_PAST, dh), BF16)
    kb_ref[CB_PAST:, :] = kn.astype(BF16)
    vb_ref[CB_PAST:, :] = v.astype(BF16)
    tq, span = bias_ref.shape
    bias = bias_ref[...]
    kidx = lax.broadcasted_iota(jnp.int32, (tq, span), 1)

    def body(n, _):
        r0 = pl.multiple_of(n * tq, tq)
        qn = (_rms(q_ref[pl.ds(r0, tq), :], qg_ref[...]) * (dh ** -0.5)).astype(BF16)
        s = _dot_nt(qn, kb_ref[pl.ds(r0, span), :]) + bias
        if not has_past:
            s = jnp.where(kidx + r0 >= CB_PAST, s, NEG_BIG)
        p = jnp.exp(s - jnp.max(s, axis=1, keepdims=True))
        o = _dot(p.astype(BF16), vb_ref[pl.ds(r0, span), :]) / jnp.sum(p, axis=1, keepdims=True)
        o_ref[pl.ds(r0, tq), :] = o.astype(BF16)
        return 0

    steps = seq // tq
    lax.fori_loop(0, steps, body, 0, unroll=min(steps, 2))


def _block_bias(bias, g):
    rows = [jnp.pad(bias, ((0, 0), (0, 0), (c * CHUNK, (g - 1 - c) * CHUNK)), constant_values=NEG_BIG)
            for c in range(g)]
    return jnp.concatenate(rows, axis=1)


def cb_attention(proj, row0, batch, seq, keep, k_past, v_past, qg, kg, bias):
    assert seq % CHUNK == 0
    bias = _block_bias(bias, 2 if (seq // CHUNK) % 2 == 0 else 1)
    half_d = proj.shape[1] // 5
    heads = half_d // LANES
    blk0 = row0 // seq
    has_past = k_past is not None
    if not has_past:
        k_past = jnp.zeros((1, CB_PAST, half_d), F32)
        v_past = k_past

    def col(j):
        return pl.BlockSpec((seq, LANES), lambda b, h: (blk0 + b, j * heads + h))

    old = pl.BlockSpec((None, CB_PAST, LANES), lambda b, h: (b if has_past else 0, 0, h))
    gspec = pl.BlockSpec((1, LANES), lambda b, h: (0, 0))
    kept = pl.BlockSpec((None, keep, LANES), lambda b, h: (b, 0, h))
    return pl.pallas_call(
        functools.partial(_cb_kernel, has_past=has_past),
        grid=(batch, heads),
        in_specs=[col(2), col(3), col(4), old, old, gspec, gspec,
                  pl.BlockSpec((None,) + bias.shape[1:], lambda b, h: (h, 0, 0))],
        out_specs=[pl.BlockSpec((seq, LANES), lambda b, h: (b, h)), kept, kept],
        out_shape=[
            jax.ShapeDtypeStruct((batch * seq, half_d), BF16),
            jax.ShapeDtypeStruct((batch, keep, half_d), F32),
            jax.ShapeDtypeStruct((batch, keep, half_d), F32),
        ],
        scratch_shapes=[pltpu.VMEM((CB_PAST + seq, LANES), BF16), pltpu.VMEM((CB_PAST + seq, LANES), BF16)],
        compiler_params=pltpu.CompilerParams(dimension_semantics=("parallel", "parallel")),
        name="cb_attention",
    )(proj, proj, proj, k_past, v_past, qg.reshape(1, LANES), kg.reshape(1, LANES), bias)


def _rope_tables(pos0, seq, half):
    inv = ROPE_BASE ** (-jnp.arange(half, dtype=F32) / half)
    ang = (pos0 + jnp.arange(seq)).astype(F32)[:, None] * inv[None, :]
    return jnp.cos(ang), jnp.sin(ang)


def _band_bias(table):
    lo = -(CHUNK - 1)
    n_flat = CB_PAST + CHUNK - 1 - CB_MAX_REL
    ext = jnp.concatenate([table[:, lo + CB_MAX_REL:], jnp.repeat(table[:, -1:], n_flat, axis=1)], axis=1)
    rev = ext[:, ::-1]
    n = rev.shape[1]
    skew = jnp.tile(rev, (1, CHUNK + 1))[:, :CHUNK * (n + 1)].reshape(-1, CHUNK, n + 1)
    return skew[:, ::-1, :CB_BAND]


def kernel(x_prompt, x_sample, state_ret, cache_sb_k, cache_sb_v, state_conv, state_lru, cache_cb_k, cache_cb_v, e_norm_mix, e_w_in, e_ret_gn, e_sb_qn, e_sb_kn, e_w_out, e_norm_ffn, e_w1, e_w3, e_w2, o_norm_mix, o_w_in, o_conv_w, o_conv_b, o_lru_wr, o_lru_br, o_lru_wi, o_lru_bi, o_lru_lam, o_cb_qn, o_cb_kn, o_cb_bias, o_w_out, o_norm_ffn, o_router, o_w1, o_w3, o_w2):
    bp, lp, d = x_prompt.shape
    bs, ls, _ = x_sample.shape
    half_d = d // 2
    tp = bp * lp
    past = cache_sb_k.shape[2]
    dk = half_d // RET_HEADS
    x = (x_prompt.reshape(tp, d), x_sample.reshape(bs * ls, d))

    qkv = norm_matmul(x, e_norm_mix[0], e_w_in[0].astype(BF16))
    cos_p, sin_p = _rope_tables(0, lp, dk // 2)
    cos_s, sin_s = _rope_tables(past, ls, dk // 2)
    a_p, p_ret = retention_mixer(qkv, 0, bp, lp, cos_p, sin_p, e_ret_gn[0], None)
    a_s, s_ret = retention_mixer(qkv, tp, bs, ls, cos_s, sin_s, e_ret_gn[0], state_ret[0])
    qn, knb, vb, p_sb_k, s_sb_k, p_sb_v, s_sb_v = sb_norm(qkv, e_sb_qn[0], e_sb_kn[0], tp)
    b_p = sb_attention_prompt(qn, knb, vb, bp, lp)
    b_s = sb_attention_sample(qn, knb, vb, cache_sb_k[0], cache_sb_v[0], tp, bs, ls)
    x = out_proj(x, (a_p, a_s), (b_p, b_s), e_w_out[0].astype(BF16))
    x = ffn(x, e_norm_ffn[0], e_w1[0].astype(BF16), e_w3[0].astype(BF16), e_w2[0].astype(BF16))

    proj = norm_matmul(x, o_norm_mix[0], o_w_in[0].astype(BF16))
    lru_w = (o_conv_w[0], o_conv_b[0], o_lru_wr[0], o_lru_br[0], o_lru_wi[0], o_lru_bi[0], o_lru_lam[0])
    c_p, p_conv, p_lru = lru_mixer(proj, 0, bp, lp, None, None, *lru_w)
    c_s, s_conv, s_lru = lru_mixer(proj, tp, bs, ls, state_conv[0], state_lru[0], *lru_w)
    bias = _band_bias(o_cb_bias[0])
    keep_p = min(CB_PAST, lp)
    d_p, p_cb_k, p_cb_v = cb_attention(proj, 0, bp, lp, keep_p, None, None, o_cb_qn[0], o_cb_kn[0], bias)
    d_s, s_cb_k, s_cb_v = cb_attention(proj, tp, bs, ls, ls, cache_cb_k[0].reshape(bs, CB_PAST, half_d),
                                       cache_cb_v[0].reshape(bs, CB_PAST, half_d), o_cb_qn[0], o_cb_kn[0], bias)
    x = out_proj(x, (c_p, c_s), (d_p, d_s), o_w_out[0].astype(BF16))
    y_p, y_s = moe_layer(x, o_norm_ffn[0], o_router[0], o_w1[0], o_w3[0], o_w2[0], tp)

    sbh = (SB_HEADS, half_d // SB_HEADS)
    cbh = (CB_HEADS, half_d // CB_HEADS)
    return (
        y_p.reshape(bp, lp, d), y_s.reshape(bs, ls, d),
        p_ret[None], p_sb_k.reshape(1, bp, lp, *sbh), p_sb_v.reshape(1, bp, lp, *sbh),
        p_conv[None], p_lru.reshape(1, bp, half_d),
        p_cb_k.reshape(1, bp, keep_p, *cbh), p_cb_v.reshape(1, bp, keep_p, *cbh),
        s_ret[None], s_sb_k.reshape(1, bs, ls, *sbh), s_sb_v.reshape(1, bs, ls, *sbh),
        s_conv[None], s_lru.reshape(1, bs, half_d),
        s_cb_k.reshape(1, bs, ls, *cbh), s_cb_v.reshape(1, bs, ls, *cbh),
    )
```

```python
import functools
import math

import jax
import jax.numpy as jnp
from jax import lax
from jax.experimental import pallas as pl
from jax.experimental.pallas import tpu as pltpu

F32 = jnp.float32
BF16 = jnp.bfloat16

NORM_EPS = 1e-6
ROPE_BASE = 10000.0
CHUNK = 64
RET_HEADS = 4
SB_HEADS = 8
CB_HEADS = 8
LRU_BLOCKS = 8
CONV_W = 4
LRU_C = 8.0
CB_PREV_CHUNKS = 8
CB_PAST = CB_PREV_CHUNKS * CHUNK
CB_BAND = (CB_PREV_CHUNKS + 1) * CHUNK
CB_MAX_REL = 128
NEG_BIG = -1e30
TOP_K = 2
LANES = 128


def _row_tile(m, cap=512):
    t = cap
    while m % t:
        t //= 2
    return t


def _rms(x, g):
    ms = jnp.mean(x * x, axis=-1, keepdims=True)
    return x * lax.rsqrt(ms + NORM_EPS) * g


def _dot(a, b):
    return jnp.dot(a, b, preferred_element_type=F32)


def _dot_nt(a, b):
    return lax.dot_general(a, b, (((1,), (1,)), ((), ())), preferred_element_type=F32)


def _dot_tn(a, b):
    return lax.dot_general(a, b, (((0,), (0,)), ((), ())), preferred_element_type=F32)


def _split(x):
    hi = x.astype(BF16)
    lo = (x - hi.astype(F32)).astype(BF16)
    return hi, lo


def _dot3(a, b):
    a_hi, a_lo = _split(a)
    b_hi, b_lo = _split(b)
    return _dot(a_hi, b_hi) + _dot(a_hi, b_lo) + _dot(a_lo, b_hi)


def _softplus(z):
    return jnp.maximum(z, 0.0) + jnp.log(1.0 + jnp.exp(-jnp.abs(z)))


def _sigmoid(z):
    return 1.0 / (1.0 + jnp.exp(-z))


def _group_specs(shape, n_first):
    return (pl.BlockSpec(shape, lambda i, *_: (jnp.minimum(i, n_first - 1), 0)),
            pl.BlockSpec(shape, lambda i, *_: (jnp.maximum(i - n_first, 0), 0)))


def _group_col_specs(shape, n_first):
    return (pl.BlockSpec(shape, lambda i, j: (jnp.minimum(i, n_first - 1), j)),
            pl.BlockSpec(shape, lambda i, j: (jnp.maximum(i - n_first, 0), j)))


def _as_groups(x, tm=None):
    if not isinstance(x, tuple):
        x = (x, x[:0])
    m = x[0].shape[0] + x[1].shape[0]
    if tm is None:
        tm = _row_tile(math.gcd(x[0].shape[0], x[1].shape[0]))
    rest = x[1] if x[1].shape[0] else x[0]
    return x[0], rest, m, tm, x[0].shape[0] // tm


def _norm_matmul_kernel(x1_ref, x2_ref, g_ref, w_ref, o_ref, h_ref, *, n_first):
    first = pl.program_id(0) < n_first
    start = pl.program_id(1) == 0

    @pl.when(start & first)
    def _():
        h_ref[...] = _rms(x1_ref[...], g_ref[...]).astype(BF16)

    @pl.when(start & jnp.logical_not(first))
    def _():
        h_ref[...] = _rms(x2_ref[...], g_ref[...]).astype(BF16)

    o_ref[...] = _dot(h_ref[...], w_ref[...])


def norm_matmul(x, g, w, tn=1024):
    k, n = w.shape
    x1, x2, m, tm, n_first = _as_groups(x)
    return pl.pallas_call(
        functools.partial(_norm_matmul_kernel, n_first=n_first),
        grid=(m // tm, n // tn),
        in_specs=[
            *_group_specs((tm, k), n_first),
            pl.BlockSpec((1, k), lambda i, j: (0, 0)),
            pl.BlockSpec((k, tn), lambda i, j: (0, j)),
        ],
        out_specs=pl.BlockSpec((tm, tn), lambda i, j: (i, j)),
        out_shape=jax.ShapeDtypeStruct((m, n), F32),
        scratch_shapes=[pltpu.VMEM((tm, k), BF16)],
        compiler_params=pltpu.CompilerParams(dimension_semantics=("parallel", "arbitrary")),
        name="norm_matmul",
    )(x1, x2, g.reshape(1, k), w)


def _out_proj_kernel(x1_ref, x2_ref, a1_ref, a2_ref, b1_ref, b2_ref, wa_ref, wb_ref, o_ref, *, n_first, x_split):
    i = pl.program_id(0)

    @pl.when(i < n_first)
    def _():
        o_ref[...] = x1_ref[...] + _dot(a1_ref[...], wa_ref[...]) + _dot(b1_ref[...], wb_ref[...])

    @pl.when(i >= n_first)
    def _():
        x_ref = x2_ref if x_split else x1_ref
        o_ref[...] = x_ref[...] + _dot(a2_ref[...], wa_ref[...]) + _dot(b2_ref[...], wb_ref[...])


def out_proj(x, a, b, w, tn=1024):
    n = w.shape[1]
    kh = a[0].shape[1]
    a1, a2, m, tm, n_first = _as_groups(a)
    b1, b2 = b
    x1, x2, _, _, x_first = _as_groups(x, tm)
    assert x_first in (n_first, m // tm)
    return pl.pallas_call(
        functools.partial(_out_proj_kernel, n_first=n_first, x_split=x_first == n_first),
        grid=(m // tm, n // tn),
        in_specs=[
            *_group_col_specs((tm, tn), x_first),
            *_group_specs((tm, kh), n_first),
            *_group_specs((tm, kh), n_first),
            pl.BlockSpec((kh, tn), lambda i, j: (0, j)),
            pl.BlockSpec((kh, tn), lambda i, j: (1, j)),
        ],
        out_specs=pl.BlockSpec((tm, tn), lambda i, j: (i, j)),
        out_shape=jax.ShapeDtypeStruct((m, n), F32),
        compiler_params=pltpu.CompilerParams(dimension_semantics=("parallel", "arbitrary")),
        name="out_proj",
    )(x1, x2, a1, a2, b1, b2, w, w)


def _ffn_kernel(x_ref, g_ref, w1_ref, w3_ref, w2_ref, o_ref, h_ref, acc_ref):
    f = pl.program_id(1)

    @pl.when(f == 0)
    def _():
        h_ref[...] = _rms(x_ref[...], g_ref[...]).astype(BF16)
        acc_ref[...] = jnp.zeros_like(acc_ref)

    h = h_ref[...]
    a = _dot(h, w1_ref[...])
    u = _dot(h, w3_ref[...])
    acc_ref[...] += _dot((a * _sigmoid(a) * u).astype(BF16), w2_ref[...])

    @pl.when(f == pl.num_programs(1) - 1)
    def _():
        o_ref[...] = x_ref[...] + acc_ref[...]


def ffn(x, g, w1, w3, w2, tf=512):
    m, d = x.shape
    ff = w1.shape[1]
    tm = _row_tile(m)
    return pl.pallas_call(
        _ffn_kernel,
        grid=(m // tm, ff // tf),
        in_specs=[
            pl.BlockSpec((tm, d), lambda i, f: (i, 0)),
            pl.BlockSpec((1, d), lambda i, f: (0, 0)),
            pl.BlockSpec((d, tf), lambda i, f: (0, f)),
            pl.BlockSpec((d, tf), lambda i, f: (0, f)),
            pl.BlockSpec((tf, d), lambda i, f: (f, 0)),
        ],
        out_specs=pl.BlockSpec((tm, d), lambda i, f: (i, 0)),
        out_shape=jax.ShapeDtypeStruct((m, d), F32),
        scratch_shapes=[pltpu.VMEM((tm, d), BF16), pltpu.VMEM((tm, d), F32)],
        compiler_params=pltpu.CompilerParams(dimension_semantics=("parallel", "arbitrary")),
        name="ffn",
    )(x, g.reshape(1, d), w1, w3, w2)


def _router_kernel(x_ref, g_ref, r_ref, eid_ref, p_ref, rank_ref, cnt_ref, run_ref):
    i = pl.program_id(0)

    @pl.when(i == 0)
    def _():
        run_ref[...] = jnp.zeros_like(run_ref)

    h = _rms(x_ref[...], g_ref[...])
    tm, d = h.shape
    logits = _dot3(h, r_ref[...])
    n_e = logits.shape[1]
    lane = lax.broadcasted_iota(jnp.int32, logits.shape, 1).astype(F32)
    m1 = jnp.max(logits, axis=1, keepdims=True)
    i1 = jnp.min(jnp.where(logits == m1, lane, float(n_e)), axis=1, keepdims=True)
    rest = jnp.where(lane == i1, -jnp.inf, logits)
    m2 = jnp.max(rest, axis=1, keepdims=True)
    i2 = jnp.min(jnp.where(rest == m2, lane, float(n_e)), axis=1, keepdims=True)
    e2 = jnp.exp(m2 - m1)
    p1 = 1.0 / (1.0 + e2)
    slot = lax.broadcasted_iota(jnp.int32, (tm, TOP_K), 1)
    eid_ref[...] = jnp.where(slot == 0, i1, i2).astype(jnp.int32)
    p_ref[...] = jnp.where(slot == 0, p1, e2 * p1)

    hit1 = lane == i1
    hit2 = lane == i2
    both = jnp.where(hit1 | hit2, 1.0, 0.0)
    earlier = (lax.broadcasted_iota(jnp.int32, (tm, tm), 1) < lax.broadcasted_iota(jnp.int32, (tm, tm), 0))
    before = _dot(jnp.where(earlier, 1.0, 0.0).astype(BF16), both.astype(BF16)) + run_ref[...]
    r1 = jnp.sum(jnp.where(hit1, before, 0.0), axis=1, keepdims=True)
    r2 = jnp.sum(jnp.where(hit2, before, 0.0), axis=1, keepdims=True)
    rank_ref[...] = jnp.where(slot == 0, r1, r2).astype(jnp.int32)
    run_ref[...] += jnp.sum(both, axis=0, keepdims=True)
    cnt_ref[...] = run_ref[...].astype(jnp.int32)


def moe_route(x, g, router):
    m, d = x.shape
    n_e = router.shape[1]
    tm = _row_tile(m)
    pair = pl.BlockSpec((tm, TOP_K), lambda i: (i, 0))
    return pl.pallas_call(
        _router_kernel,
        grid=(m // tm,),
        in_specs=[
            pl.BlockSpec((tm, d), lambda i: (i, 0)),
            pl.BlockSpec((1, d), lambda i: (0, 0)),
            pl.BlockSpec((d, n_e), lambda i: (0, 0)),
        ],
        out_specs=[pair, pair, pair, pl.BlockSpec((1, n_e), lambda i: (0, 0))],
        out_shape=[
            jax.ShapeDtypeStruct((m, TOP_K), jnp.int32),
            jax.ShapeDtypeStruct((m, TOP_K), F32),
            jax.ShapeDtypeStruct((m, TOP_K), jnp.int32),
            jax.ShapeDtypeStruct((1, n_e), jnp.int32),
        ],
        scratch_shapes=[pltpu.VMEM((1, n_e), F32)],
        compiler_params=pltpu.CompilerParams(dimension_semantics=("arbitrary",)),
        name="moe_route",
    )(x, g.reshape(1, d), router)


SCALAR_UNROLL = 8


def _row_copy(src, s, dst, t, sem):
    return pltpu.make_async_copy(src.at[pl.ds(s, 1)], dst.at[pl.ds(t, 1)], sem)


def _invert_kernel(pos_ref, src_ref):
    def clear(r, _):
        src_ref[r] = 0
        return 0

    def put(t, _):
        for k in range(TOP_K):
            src_ref[pos_ref[TOP_K * t + k]] = t
        return 0

    lax.fori_loop(0, src_ref.shape[0], clear, 0, unroll=SCALAR_UNROLL)
    lax.fori_loop(0, pos_ref.shape[0] // TOP_K, put, 0, unroll=SCALAR_UNROLL)


def moe_invert(pos, n_rows):
    assert n_rows % SCALAR_UNROLL == 0 and pos.shape[0] % SCALAR_UNROLL == 0
    return pl.pallas_call(
        _invert_kernel,
        in_specs=[pl.BlockSpec(memory_space=pltpu.SMEM)],
        out_specs=pl.BlockSpec(memory_space=pltpu.SMEM),
        out_shape=jax.ShapeDtypeStruct((n_rows,), jnp.int32),
        name="moe_invert",
    )(pos.reshape(-1))


def _gather_kernel(src_ref, nu_ref, x_ref, g_ref, hs_ref, xs_ref, sem):
    r = pl.program_id(0)
    tm = hs_ref.shape[0]
    slot = r % 2

    def issue(tile, into):
        def body(j, _):
            _row_copy(x_ref, src_ref[tile * tm + j], xs_ref.at[into], j, sem.at[into]).start()
            return 0
        lax.fori_loop(0, tm, body, 0, unroll=SCALAR_UNROLL)

    @pl.when(r == 0)
    def _():
        issue(0, 0)

    @pl.when(r + 1 < nu_ref[0])
    def _():
        issue(r + 1, 1 - slot)

    @pl.when(r < nu_ref[0])
    def _():
        def drain(j, _):
            _row_copy(x_ref, 0, xs_ref.at[slot], 0, sem.at[slot]).wait()
            return 0

        lax.fori_loop(0, tm, drain, 0, unroll=SCALAR_UNROLL)
        hs_ref[...] = _rms(xs_ref[slot], g_ref[...]).astype(BF16)

    @pl.when(r >= nu_ref[0])
    def _():
        hs_ref[...] = jnp.zeros_like(hs_ref)


def moe_gather(src, n_used, x, g, tm):
    n_rows = src.shape[0]
    d = x.shape[1]
    return pl.pallas_call(
        _gather_kernel,
        grid_spec=pltpu.PrefetchScalarGridSpec(
            num_scalar_prefetch=2,
            grid=(n_rows // tm,),
            in_specs=[pl.BlockSpec(memory_space=pl.ANY), pl.BlockSpec((1, d), lambda r, src, nu: (0, 0))],
            out_specs=pl.BlockSpec((tm, d), lambda r, src, nu: (r, 0)),
            scratch_shapes=[pltpu.VMEM((2, tm, d), F32), pltpu.SemaphoreType.DMA((2,))],
        ),
        out_shape=jax.ShapeDtypeStruct((n_rows, d), BF16),
        compiler_params=pltpu.CompilerParams(dimension_semantics=("arbitrary",)),
        name="moe_gather",
    )(src, n_used, x, g.reshape(1, d))


def _expert_rows(first_ref, count_ref, src_ref, dst_ref, ibuf_ref, obuf_ref, sem_in, sem_out, compute):
    e = pl.program_id(0)
    c = pl.program_id(1)
    tm = ibuf_ref.shape[1]
    width = obuf_ref.shape[2]
    first = first_ref[e]
    n_real = count_ref[e]
    n_all = jnp.where(e == pl.num_programs(0) - 1, dst_ref.shape[0] // tm - first, n_real)
    cols = pl.ds(pl.multiple_of(c * width, width), width)

    def rows(k, base=first):
        return pl.ds(pl.multiple_of((base + k) * tm, tm), tm)

    def load(k, slot, base=first):
        return pltpu.make_async_copy(src_ref.at[rows(k, base)], ibuf_ref.at[slot], sem_in.at[slot])

    def store(k, slot):
        return pltpu.make_async_copy(obuf_ref.at[slot], dst_ref.at[rows(k), cols], sem_out.at[slot])

    @pl.when((e == 0) & (c == 0) & (n_real > 0))
    def _():
        load(0, 0).start()

    def body(k, _):
        slot = k % 2

        @pl.when(k >= 2)
        def _():
            store(k - 2, slot).wait()

        @pl.when(k < n_real)
        def _():
            load(k, slot).wait()

            @pl.when(k + 1 < n_real)
            def _():
                load(k + 1, 1 - slot).start()

            obuf_ref[slot] = compute(ibuf_ref[slot])

        @pl.when(k >= n_real)
        def _():
            obuf_ref[slot] = jnp.zeros(obuf_ref.shape[1:], obuf_ref.dtype)

        store(k, slot).start()
        return 0

    lax.fori_loop(0, n_all, body, 0)

    last_c = c == pl.num_programs(1) - 1
    e_next = jnp.where(last_c, jnp.minimum(e + 1, pl.num_programs(0) - 1), e)
    more = jnp.logical_not(last_c & (e == pl.num_programs(0) - 1))

    @pl.when(more & (count_ref[e_next] > 0))
    def _():
        load(0, 0, first_ref[e_next]).start()

    for back in (2, 1):
        @pl.when(n_all >= back)
        def _():
            store(n_all - back, (n_all - back) % 2).wait()


def _moe_up_kernel(first_ref, count_ref, hs_ref, w1_ref, w3_ref, act_ref, w1b_ref, w3b_ref, ibuf_ref, obuf_ref,
                   sem_in, sem_out):
    w1b_ref[...] = w1_ref[...].astype(BF16)
    w3b_ref[...] = w3_ref[...].astype(BF16)

    def compute(h):
        a = _dot(h, w1b_ref[...])
        u = _dot(h, w3b_ref[...])
        return (a * _sigmoid(a) * u).astype(BF16)

    _expert_rows(first_ref, count_ref, hs_ref, act_ref, ibuf_ref, obuf_ref, sem_in, sem_out, compute)


def _moe_down_kernel(first_ref, count_ref, act_ref, w2_ref, y_ref, w2b_ref, ibuf_ref, obuf_ref, sem_in, sem_out):
    w2b_ref[...] = w2_ref[...].astype(BF16)
    _expert_rows(first_ref, count_ref, act_ref, y_ref, ibuf_ref, obuf_ref, sem_in, sem_out,
                 lambda a: _dot(a, w2b_ref[...]))


def moe_experts(tile_first, tile_count, hs, w1, w3, w2, tm, tf=512, tn=512):
    n_rows, d = hs.shape
    n_e, _, ff = w1.shape
    any_spec = pl.BlockSpec(memory_space=pl.ANY)
    sems = [pltpu.SemaphoreType.DMA((2,)), pltpu.SemaphoreType.DMA((2,))]
    act = pl.pallas_call(
        _moe_up_kernel,
        grid_spec=pltpu.PrefetchScalarGridSpec(
            num_scalar_prefetch=2,
            grid=(n_e, ff // tf),
            in_specs=[
                any_spec,
                pl.BlockSpec((None, d, tf), lambda e, f, *_: (e, 0, f)),
                pl.BlockSpec((None, d, tf), lambda e, f, *_: (e, 0, f)),
            ],
            out_specs=any_spec,
            scratch_shapes=[pltpu.VMEM((d, tf), BF16), pltpu.VMEM((d, tf), BF16),
                            pltpu.VMEM((2, tm, d), BF16), pltpu.VMEM((2, tm, tf), BF16), *sems],
        ),
        out_shape=jax.ShapeDtypeStruct((n_rows, ff), BF16),
        compiler_params=pltpu.CompilerParams(dimension_semantics=("arbitrary", "arbitrary"),
                                             vmem_limit_bytes=48 * 1024 * 1024),
        name="moe_up",
    )(tile_first, tile_count, hs, w1, w3)
    return pl.pallas_call(
        _moe_down_kernel,
        grid_spec=pltpu.PrefetchScalarGridSpec(
            num_scalar_prefetch=2,
            grid=(n_e, d // tn),
            in_specs=[any_spec, pl.BlockSpec((None, ff, tn), lambda e, n, *_: (e, 0, n))],
            out_specs=any_spec,
            scratch_shapes=[pltpu.VMEM((ff, tn), BF16),
                            pltpu.VMEM((2, tm, ff), BF16), pltpu.VMEM((2, tm, tn), F32), *sems],
        ),
        out_shape=jax.ShapeDtypeStruct((n_rows, d), F32),
        compiler_params=pltpu.CompilerParams(dimension_semantics=("arbitrary", "arbitrary"),
                                             vmem_limit_bytes=56 * 1024 * 1024),
        name="moe_down",
    )(tile_first, tile_count, act, w2)


def _combine_kernel(pos_ref, x_ref, p_ref, y_ref, o1_ref, o2_ref, ya_ref, yb_ref, sem, *, n_first):
    tm = x_ref.shape[0]
    i = pl.program_id(0)
    slot = i % 2

    def issue(tile, into):
        def body(j, _):
            t = tile * tm + j
            _row_copy(y_ref, pos_ref[TOP_K * t], ya_ref.at[into], j, sem.at[into]).start()
            _row_copy(y_ref, pos_ref[TOP_K * t + 1], yb_ref.at[into], j, sem.at[into]).start()
            return 0
        lax.fori_loop(0, tm, body, 0, unroll=SCALAR_UNROLL)

    def drain(j, _):
        _row_copy(y_ref, 0, ya_ref.at[slot], 0, sem.at[slot]).wait()
        _row_copy(y_ref, 0, yb_ref.at[slot], 0, sem.at[slot]).wait()
        return 0

    @pl.when(i == 0)
    def _():
        issue(0, 0)

    @pl.when(i + 1 < pl.num_programs(0))
    def _():
        issue(i + 1, 1 - slot)

    lax.fori_loop(0, tm, drain, 0, unroll=SCALAR_UNROLL)
    p = p_ref[...]
    out = x_ref[...] + p[:, 0:1] * ya_ref[slot] + p[:, 1:2] * yb_ref[slot]

    @pl.when(i < n_first)
    def _():
        o1_ref[...] = out

    @pl.when(i >= n_first)
    def _():
        o2_ref[...] = out


def moe_combine(pos, x, p, y, m_first, tm=128):
    m, d = x.shape
    assert m_first % tm == 0 and m % tm == 0
    n_first = m_first // tm
    return pl.pallas_call(
        functools.partial(_combine_kernel, n_first=n_first),
        grid_spec=pltpu.PrefetchScalarGridSpec(
            num_scalar_prefetch=1,
            grid=(m // tm,),
            in_specs=[
                pl.BlockSpec((tm, d), lambda i, pos: (i, 0)),
                pl.BlockSpec((tm, TOP_K), lambda i, pos: (i, 0)),
                pl.BlockSpec(memory_space=pl.ANY),
            ],
            out_specs=list(_group_specs((tm, d), n_first)),
            scratch_shapes=[pltpu.VMEM((2, tm, d), F32), pltpu.VMEM((2, tm, d), F32),
                            pltpu.SemaphoreType.DMA((2,))],
        ),
        out_shape=[jax.ShapeDtypeStruct((m_first, d), F32), jax.ShapeDtypeStruct((m - m_first, d), F32)],
        compiler_params=pltpu.CompilerParams(dimension_semantics=("arbitrary",)),
        name="moe_combine",
    )(pos.reshape(-1), x, p, y)


def moe_layer(x, g, router, w1, w3, w2, m_first, tm=512):
    m = x.shape[0]
    n_e = router.shape[1]
    eid, p, rank, counts = moe_route(x, g, router)
    padded = (counts[0] + tm - 1) // tm * tm
    ends = jnp.cumsum(padded)
    starts = ends - padded
    experts = jnp.arange(n_e, dtype=jnp.int32)
    pos = rank + jnp.sum(jnp.where(eid[:, :, None] == experts, starts, 0), axis=-1)
    n_tiles = -(-TOP_K * m // tm) + n_e
    n_used = (ends[-1:] // tm).astype(jnp.int32)
    hs = moe_gather(moe_invert(pos, n_tiles * tm), n_used, x, g, tm)
    y = moe_experts((starts // tm).astype(jnp.int32), (padded // tm).astype(jnp.int32), hs, w1, w3, w2, tm)
    return moe_combine(pos, x, p, y, m_first)


def _ret_kernel(q_ref, k_ref, v_ref, g_ref, cos_ref, sin_ref, gn_ref, s0_ref, o_ref, s_ref, st_ref,
                *, has_state):
    n = pl.program_id(1)
    c = q_ref.shape[0]
    dk = q_ref.shape[1] // RET_HEADS
    half = dk // 2

    @pl.when(n == 0)
    def _():
        if has_state:
            st_ref[...] = s0_ref[...]
        else:
            st_ref[...] = jnp.zeros_like(st_ref)

    cos = cos_ref[...]
    sin = sin_ref[...]
    row = lax.broadcasted_iota(jnp.int32, (c, c), 0)
    col = lax.broadcasted_iota(jnp.int32, (c, c), 1)
    diff = (row - col).astype(F32)
    ridx = lax.broadcasted_iota(jnp.int32, (c, 1), 0).astype(F32)

    def rot(x):
        x1, x2 = x[:, :half], x[:, half:]
        return jnp.concatenate([x1 * cos - x2 * sin, x1 * sin + x2 * cos], axis=1)

    for h in range(RET_HEADS):
        log_g = math.log1p(-(2.0 ** (-5.0 - h)))
        sl = slice(h * dk, (h + 1) * dk)
        qr = rot(q_ref[:, sl]) * (dk ** -0.5)
        kr = rot(k_ref[:, sl])
        vh = v_ref[:, sl].astype(BF16)
        decay = jnp.where(diff >= 0, jnp.exp(log_g * jnp.maximum(diff, 0.0)), 0.0)
        scores = _dot_nt(qr.astype(BF16), kr.astype(BF16)) * decay
        s = st_ref[h]
        q_dec = jnp.exp(log_g * (ridx + 1.0))
        k_dec = jnp.exp(log_g * (c - 1.0 - ridx))
        o = _dot(scores.astype(BF16), vh) + _dot((qr * q_dec).astype(BF16), s.astype(BF16))
        st_ref[h] = s * math.exp(log_g * c) + _dot_tn((kr * k_dec).astype(BF16), vh)
        gh = g_ref[:, sl]
        o_ref[:, sl] = (gh * _sigmoid(gh) * _rms(o, gn_ref[h:h + 1, :])).astype(BF16)

    @pl.when(n == pl.num_programs(1) - 1)
    def _():
        s_ref[...] = st_ref[...]


def retention_mixer(qkv, row0, batch, seq, cos, sin, gn, s0):
    half_d = qkv.shape[1] // 7
    dk = half_d // RET_HEADS
    c = CHUNK if seq % CHUNK == 0 else seq
    nc = seq // c
    blk0 = row0 // c
    has_state = s0 is not None
    if not has_state:
        s0 = jnp.zeros((1, RET_HEADS, dk, dk), F32)

    def col(j):
        return pl.BlockSpec((c, half_d), lambda b, n: (blk0 + b * nc + n, j))

    return pl.pallas_call(
        functools.partial(_ret_kernel, has_state=has_state),
        grid=(batch, nc),
        in_specs=[
            col(0), col(1), col(2), col(3),
            pl.BlockSpec((c, dk // 2), lambda b, n: (n, 0)),
            pl.BlockSpec((c, dk // 2), lambda b, n: (n, 0)),
            pl.BlockSpec((RET_HEADS, dk), lambda b, n: (0, 0)),
            pl.BlockSpec((None, RET_HEADS, dk, dk), lambda b, n: (b if has_state else 0, 0, 0, 0)),
        ],
        out_specs=[
            pl.BlockSpec((c, half_d), lambda b, n: (b * nc + n, 0)),
            pl.BlockSpec((None, RET_HEADS, dk, dk), lambda b, n: (b, 0, 0, 0)),
        ],
        out_shape=[
            jax.ShapeDtypeStruct((batch * seq, half_d), BF16),
            jax.ShapeDtypeStruct((batch, RET_HEADS, dk, dk), F32),
        ],
        scratch_shapes=[pltpu.VMEM((RET_HEADS, dk, dk), F32)],
        compiler_params=pltpu.CompilerParams(dimension_semantics=("parallel", "arbitrary")),
        name="retention",
    )(qkv, qkv, qkv, qkv, cos, sin, gn, s0)


def _sb_norm_kernel(q_ref, k_ref, v_ref, qg_ref, kg_ref, qn_ref, knb_ref, vb_ref, k1_ref, k2_ref, v1_ref, v2_ref,
                    *, n_first):
    dh = qg_ref.shape[1]
    first = pl.program_id(0) < n_first
    for h in range(q_ref.shape[1] // dh):
        sl = slice(h * dh, (h + 1) * dh)
        qn_ref[:, sl] = (_rms(q_ref[:, sl], qg_ref[...]) * (dh ** -0.5)).astype(BF16)
        kn = _rms(k_ref[:, sl], kg_ref[...])
        knb_ref[:, sl] = kn.astype(BF16)

        @pl.when(first)
        def _():
            k1_ref[:, sl] = kn

        @pl.when(jnp.logical_not(first))
        def _():
            k2_ref[:, sl] = kn

    v = v_ref[...]
    vb_ref[...] = v.astype(BF16)

    @pl.when(first)
    def _():
        v1_ref[...] = v

    @pl.when(jnp.logical_not(first))
    def _():
        v2_ref[...] = v


def sb_norm(qkv, qg, kg, m_first):
    m = qkv.shape[0]
    half_d = qkv.shape[1] // 7
    dh = qg.shape[0]
    tm = _row_tile(math.gcd(m_first, m - m_first))
    n_first = m_first // tm

    def col(j):
        return pl.BlockSpec((tm, half_d), lambda i: (i, j))

    out = pl.BlockSpec((tm, half_d), lambda i: (i, 0))
    gspec = pl.BlockSpec((1, dh), lambda i: (0, 0))
    split = _group_specs((tm, half_d), n_first)
    f32_first = jax.ShapeDtypeStruct((m_first, half_d), F32)
    f32_rest = jax.ShapeDtypeStruct((m - m_first, half_d), F32)
    return pl.pallas_call(
        functools.partial(_sb_norm_kernel, n_first=n_first),
        grid=(m // tm,),
        in_specs=[col(4), col(5), col(6), gspec, gspec],
        out_specs=[out] * 3 + [*split, *split],
        out_shape=[jax.ShapeDtypeStruct((m, half_d), BF16)] * 3 + [f32_first, f32_rest, f32_first, f32_rest],
        compiler_params=pltpu.CompilerParams(dimension_semantics=("arbitrary",)),
        name="sb_norm",
    )(qkv, qkv, qkv, qg.reshape(1, dh), kg.reshape(1, dh))


SB_EXIT_LOG = 88.0


def _sb_block(q, kb, vb, carry, acc, strict_diag):
    tq, tk = q.shape[0], kb.shape[0]
    z = _dot_nt(q, kb)
    sp = _softplus(z)
    log_keep = -sp
    if strict_diag:
        mask = (lax.broadcasted_iota(jnp.int32, (tq, tk), 1) < lax.broadcasted_iota(jnp.int32, (tq, tk), 0))
        log_keep = jnp.where(mask, log_keep, 0.0)
    later = (lax.broadcasted_iota(jnp.int32, (tk, tk), 0) > lax.broadcasted_iota(jnp.int32, (tk, tk), 1))
    u = jnp.where(later, 1.0, 0.0).astype(BF16)
    hi, lo = _split(log_keep)
    after = _dot(hi, u) + _dot(lo, u) + carry
    w = jnp.exp(z - sp + after)
    if strict_diag:
        w = jnp.where(mask, w, 0.0)
    return jnp.sum(log_keep, axis=1, keepdims=True), acc + _dot(w.astype(BF16), vb)


def _sb_visit(q_ref, carry_ref, acc_ref, key_block, strict_diag):
    heads = q_ref.shape[1] // LANES
    worst = None
    for h in range(heads):
        sl = slice(h * LANES, (h + 1) * LANES)
        kb, vb = key_block(h, sl)
        tk = kb.shape[0]
        carry = carry_ref[h]
        block_sum, acc = _sb_block(q_ref[:, sl], kb, vb, carry[:, :tk], acc_ref[:, sl], strict_diag)
        carry = carry + block_sum
        carry_ref[h] = carry
        acc_ref[:, sl] = acc
        worst = carry if worst is None else jnp.maximum(worst, carry)
    return jnp.max(worst)


def _sb_prompt_kernel(q_ref, k_ref, v_ref, o_ref, acc_ref, carry_ref):
    i = pl.program_id(1)
    tq = q_ref.shape[0]
    acc_ref[...] = jnp.zeros_like(acc_ref)
    carry_ref[...] = jnp.zeros_like(carry_ref)

    def visit(j, strict_diag):
        s = pl.multiple_of(j * tq, tq)
        return _sb_visit(q_ref, carry_ref, acc_ref,
                         lambda h, sl: (k_ref[pl.ds(s, tq), sl], v_ref[pl.ds(s, tq), sl]), strict_diag)

    top = visit(i, True)
    lax.while_loop(lambda st: (st[0] >= 0) & (st[1] > -SB_EXIT_LOG),
                   lambda st: (st[0] - 1, visit(st[0], False)), (i - 1, top))
    o_ref[...] = acc_ref[...].astype(BF16)


def sb_attention_prompt(qn, knb, vb, batch, seq, tq=LANES):
    width = qn.shape[1]
    nq = seq // tq
    return pl.pallas_call(
        _sb_prompt_kernel,
        grid=(batch, nq),
        in_specs=[
            pl.BlockSpec((tq, width), lambda b, i: (b * nq + i, 0)),
            pl.BlockSpec((seq, width), lambda b, i: (b, 0)),
            pl.BlockSpec((seq, width), lambda b, i: (b, 0)),
        ],
        out_specs=pl.BlockSpec((tq, width), lambda b, i: (b * nq + i, 0)),
        out_shape=jax.ShapeDtypeStruct((batch * seq, width), BF16),
        scratch_shapes=[pltpu.VMEM((tq, width), F32), pltpu.VMEM((width // LANES, tq, LANES), F32)],
        compiler_params=pltpu.CompilerParams(dimension_semantics=("parallel", "arbitrary")),
        name="sb_attention_prompt",
    )(qn, knb, vb)


def _sb_sample_kernel(q_ref, k_ref, v_ref, kp_ref, vp_ref, o_ref, acc_ref, carry_ref, kbuf_ref, vbuf_ref, sem):
    b = pl.program_id(0)
    tk = kbuf_ref.shape[1]
    n_past = kp_ref.shape[1] // tk
    acc_ref[...] = jnp.zeros_like(acc_ref)
    carry_ref[...] = jnp.zeros_like(carry_ref)

    def fetch(j):
        slot = (n_past - 1 - j) % 2
        rows = pl.ds(pl.multiple_of(j * tk, tk), tk)
        return (pltpu.make_async_copy(kp_ref.at[b, rows], kbuf_ref.at[slot], sem.at[0, slot]),
                pltpu.make_async_copy(vp_ref.at[b, rows], vbuf_ref.at[slot], sem.at[1, slot]))

    def start(j):
        for c in fetch(j):
            c.start()

    def wait(j):
        for c in fetch(j):
            c.wait()

    start(n_past - 1)
    top = _sb_visit(q_ref, carry_ref, acc_ref, lambda h, sl: (k_ref[:, sl], v_ref[:, sl]), True)

    def body(st):
        j = st[0]
        slot = (n_past - 1 - j) % 2
        wait(j)

        @pl.when(j >= 1)
        def _():
            start(j - 1)

        top = _sb_visit(q_ref, carry_ref, acc_ref,
                        lambda h, sl: (kbuf_ref[slot, :, h, :].astype(BF16), vbuf_ref[slot, :, h, :].astype(BF16)),
                        False)
        return j - 1, top

    j_end, _ = lax.while_loop(lambda st: (st[0] >= 0) & (st[1] > -SB_EXIT_LOG), body, (n_past - 1, top))

    @pl.when(j_end >= 0)
    def _():
        wait(j_end)

    o_ref[...] = acc_ref[...].astype(BF16)


def sb_attention_sample(qn, knb, vb, k_past, v_past, row0, batch, seq, tk=LANES):
    width = qn.shape[1]
    heads = width // LANES
    blk0 = row0 // seq
    assert k_past.shape[1] % tk == 0 and k_past.shape[2:] == (heads, LANES)
    new = pl.BlockSpec((seq, width), lambda b: (blk0 + b, 0))
    return pl.pallas_call(
        _sb_sample_kernel,
        grid=(batch,),
        in_specs=[new, new, new, pl.BlockSpec(memory_space=pl.ANY), pl.BlockSpec(memory_space=pl.ANY)],
        out_specs=pl.BlockSpec((seq, width), lambda b: (b, 0)),
        out_shape=jax.ShapeDtypeStruct((batch * seq, width), BF16),
        scratch_shapes=[
            pltpu.VMEM((seq, width), F32), pltpu.VMEM((heads, seq, LANES), F32),
            pltpu.VMEM((2, tk, heads, LANES), F32), pltpu.VMEM((2, tk, heads, LANES), F32),
            pltpu.SemaphoreType.DMA((2, 2)),
        ],
        compiler_params=pltpu.CompilerParams(dimension_semantics=("arbitrary",)),
        name="sb_attention_sample",
    )(qn, knb, vb, k_past, v_past)


def _lru_kernel(gc_ref, xc_ref, buf_ref, h0_ref, cw_ref, cb_ref, wr_ref, br_ref, wi_ref, bi_ref, lam_ref,
                o_ref, conv_ref, hl_ref, *, has_state):
    seq, bw = xc_ref.shape
    x = xc_ref[...]
    row = lax.broadcasted_iota(jnp.int32, (seq, bw), 0)
    cw = cw_ref[...]
    y = cb_ref[...] + cw[CONV_W - 1:CONV_W, :] * x
    for d in range(1, CONV_W):
        xs = pltpu.roll(x, d, 0)
        for r in range(d):
            prev = buf_ref[CONV_W - 1 - d + r:CONV_W - d + r, :] if has_state else jnp.zeros((1, bw), F32)
            xs = jnp.where(row == r, prev, xs)
        y = y + cw[CONV_W - 1 - d:CONV_W - d, :] * xs
    conv_ref[...] = x[seq - (CONV_W - 1):, :]

    r_gate = _sigmoid(_dot3(y, wr_ref[...]) + br_ref[...])
    i_gate = _sigmoid(_dot3(y, wi_ref[...]) + bi_ref[...])
    log_a = -LRU_C * r_gate * _softplus(-lam_ref[...])
    a = jnp.exp(log_a)
    u = jnp.sqrt(-jnp.tanh(log_a) * (a * a + 1.0)) * (i_gate * y)
    if has_state:
        u = jnp.where(row == 0, u + a * h0_ref[...], u)

    shift = 1
    while shift < seq:
        a_s = pltpu.roll(a, shift, 0)
        u_s = pltpu.roll(u, shift, 0)
        live = row >= shift
        u = jnp.where(live, a * u_s + u, u)
        a = jnp.where(live, a * a_s, a)
        shift *= 2

    hl_ref[...] = u[seq - 1:, :]
    gc = gc_ref[...]
    gelu = 0.5 * gc * (1.0 + jnp.tanh(math.sqrt(2.0 / math.pi) * (gc + 0.044715 * gc * gc * gc)))
    o_ref[...] = (gelu * u).astype(BF16)


def lru_mixer(proj, row0, batch, seq, conv_buf, h0, cw, cb, wr, br, wi, bi, lam):
    width = proj.shape[1] // 5
    bw = width // LRU_BLOCKS
    blk0 = row0 // seq
    has_state = conv_buf is not None
    if not has_state:
        conv_buf = jnp.zeros((1, CONV_W - 1, width), F32)
        h0 = jnp.zeros((1, width), F32)
    h0 = h0.reshape(-1, 1, width)

    def vec(k):
        return pl.BlockSpec((k, bw), lambda b, n: (0, n))

    wspec = pl.BlockSpec((None, bw, bw), lambda b, n: (n, 0, 0))
    return pl.pallas_call(
        functools.partial(_lru_kernel, has_state=has_state),
        grid=(batch, LRU_BLOCKS),
        in_specs=[
            pl.BlockSpec((seq, bw), lambda b, n: (blk0 + b, n)),
            pl.BlockSpec((seq, bw), lambda b, n: (blk0 + b, LRU_BLOCKS + n)),
            pl.BlockSpec((None, CONV_W - 1, bw), lambda b, n: (b if has_state else 0, 0, n)),
            pl.BlockSpec((None, 1, bw), lambda b, n: (b if has_state else 0, 0, n)),
            vec(CONV_W), vec(1), wspec, vec(1), wspec, vec(1), vec(1),
        ],
        out_specs=[
            pl.BlockSpec((seq, bw), lambda b, n: (b, n)),
            pl.BlockSpec((None, CONV_W - 1, bw), lambda b, n: (b, 0, n)),
            pl.BlockSpec((None, 1, bw), lambda b, n: (b, 0, n)),
        ],
        out_shape=[
            jax.ShapeDtypeStruct((batch * seq, width), BF16),
            jax.ShapeDtypeStruct((batch, CONV_W - 1, width), F32),
            jax.ShapeDtypeStruct((batch, 1, width), F32),
        ],
        compiler_params=pltpu.CompilerParams(dimension_semantics=("parallel", "parallel")),
        name="rg_lru",
    )(proj, proj, conv_buf, h0, cw, cb.reshape(1, width), wr, br.reshape(1, width), wi,
      bi.reshape(1, width), lam.reshape(1, width))


def _cb_kernel(q_ref, k_ref, v_ref, kp_ref, vp_ref, qg_ref, kg_ref, bias_ref, o_ref, ko_ref, vo_ref,
               kb_ref, vb_ref, *, has_past):
    seq, dh = q_ref.shape
    keep = ko_ref.shape[0]
    kn = _rms(k_ref[...], kg_ref[...])
    v = v_ref[...]
    ko_ref[...] = kn[seq - keep:, :]
    vo_ref[...] = v[seq - keep:, :]
    if has_past:
        kb_ref[:CB_PAST, :] = kp_ref[...].astype(BF16)
        vb_ref[:CB_PAST, :] = vp_ref[...].astype(BF16)
    else:
        kb_ref[:CB_PAST, :] = jnp.zeros((CB_PAST, dh), BF16)
        vb_ref[:CB_PAST, :] = jnp.zeros((CB_PAST, dh), BF16)
    kb_ref[CB_PAST:, :] = kn.astype(BF16)
    vb_ref[CB_PAST:, :] = v.astype(BF16)
    tq, span = bias_ref.shape
    bias = bias_ref[...]
    kidx = lax.broadcasted_iota(jnp.int32, (tq, span), 1)

    def body(n, _):
        r0 = pl.multiple_of(n * tq, tq)
        qn = (_rms(q_ref[pl.ds(r0, tq), :], qg_ref[...]) * (dh ** -0.5)).astype(BF16)
        s = _dot_nt(qn, kb_ref[pl.ds(r0, span), :]) + bias
        if not has_past:
            s = jnp.where(kidx + r0 >= CB_PAST, s, NEG_BIG)
        p = jnp.exp(s - jnp.max(s, axis=1, keepdims=True))
        o = _dot(p.astype(BF16), vb_ref[pl.ds(r0, span), :]) / jnp.sum(p, axis=1, keepdims=True)
        o_ref[pl.ds(r0, tq), :] = o.astype(BF16)
        return 0

    steps = seq // tq
    lax.fori_loop(0, steps, body, 0, unroll=min(steps, 2))


def _block_bias(bias, g):
    rows = [jnp.pad(bias, ((0, 0), (0, 0), (c * CHUNK, (g - 1 - c) * CHUNK)), constant_values=NEG_BIG)
            for c in range(g)]
    return jnp.concatenate(rows, axis=1)


def cb_attention(proj, row0, batch, seq, keep, k_past, v_past, qg, kg, bias):
    assert seq % CHUNK == 0
    bias = _block_bias(bias, 2 if (seq // CHUNK) % 2 == 0 else 1)
    half_d = proj.shape[1] // 5
    heads = half_d // LANES
    blk0 = row0 // seq
    has_past = k_past is not None
    if not has_past:
        k_past = jnp.zeros((1, CB_PAST, half_d), F32)
        v_past = k_past

    def col(j):
        return pl.BlockSpec((seq, LANES), lambda b, h: (blk0 + b, j * heads + h))

    old = pl.BlockSpec((None, CB_PAST, LANES), lambda b, h: (b if has_past else 0, 0, h))
    gspec = pl.BlockSpec((1, LANES), lambda b, h: (0, 0))
    kept = pl.BlockSpec((None, keep, LANES), lambda b, h: (b, 0, h))
    return pl.pallas_call(
        functools.partial(_cb_kernel, has_past=has_past),
        grid=(batch, heads),
        in_specs=[col(2), col(3), col(4), old, old, gspec, gspec,
                  pl.BlockSpec((None,) + bias.shape[1:], lambda b, h: (h, 0, 0))],
        out_specs=[pl.BlockSpec((seq, LANES), lambda b, h: (b, h)), kept, kept],
        out_shape=[
            jax.ShapeDtypeStruct((batch * seq, half_d), BF16),
            jax.ShapeDtypeStruct((batch, keep, half_d), F32),
            jax.ShapeDtypeStruct((batch, keep, half_d), F32),
        ],
        scratch_shapes=[pltpu.VMEM((CB_PAST + seq, LANES), BF16), pltpu.VMEM((CB_PAST + seq, LANES), BF16)],
        compiler_params=pltpu.CompilerParams(dimension_semantics=("parallel", "parallel")),
        name="cb_attention",
    )(proj, proj, proj, k_past, v_past, qg.reshape(1, LANES), kg.reshape(1, LANES), bias)


def _rope_tables(pos0, seq, half):
    inv = ROPE_BASE ** (-jnp.arange(half, dtype=F32) / half)
    ang = (pos0 + jnp.arange(seq)).astype(F32)[:, None] * inv[None, :]
    return jnp.cos(ang), jnp.sin(ang)


def _band_bias(table):
    lo = -(CHUNK - 1)
    n_flat = CB_PAST + CHUNK - 1 - CB_MAX_REL
    ext = jnp.concatenate([table[:, lo + CB_MAX_REL:], jnp.repeat(table[:, -1:], n_flat, axis=1)], axis=1)
    rev = ext[:, ::-1]
    n = rev.shape[1]
    skew = jnp.tile(rev, (1, CHUNK + 1))[:, :CHUNK * (n + 1)].reshape(-1, CHUNK, n + 1)
    return skew[:, ::-1, :CB_BAND]


def kernel(x_prompt, x_sample, state_ret, cache_sb_k, cache_sb_v, state_conv, state_lru, cache_cb_k, cache_cb_v, e_norm_mix, e_w_in, e_ret_gn, e_sb_qn, e_sb_kn, e_w_out, e_norm_ffn, e_w1, e_w3, e_w2, o_norm_mix, o_w_in, o_conv_w, o_conv_b, o_lru_wr, o_lru_br, o_lru_wi, o_lru_bi, o_lru_lam, o_cb_qn, o_cb_kn, o_cb_bias, o_w_out, o_norm_ffn, o_router, o_w1, o_w3, o_w2):
    bp, lp, d = x_prompt.shape
    bs, ls, _ = x_sample.shape
    half_d = d // 2
    tp = bp * lp
    past = cache_sb_k.shape[2]
    dk = half_d // RET_HEADS
    x = (x_prompt.reshape(tp, d), x_sample.reshape(bs * ls, d))

    qkv = norm_matmul(x, e_norm_mix[0], e_w_in[0].astype(BF16))
    cos_p, sin_p = _rope_tables(0, lp, dk // 2)
    cos_s, sin_s = _rope_tables(past, ls, dk // 2)
    a_p, p_ret = retention_mixer(qkv, 0, bp, lp, cos_p, sin_p, e_ret_gn[0], None)
    a_s, s_ret = retention_mixer(qkv, tp, bs, ls, cos_s, sin_s, e_ret_gn[0], state_ret[0])
    qn, knb, vb, p_sb_k, s_sb_k, p_sb_v, s_sb_v = sb_norm(qkv, e_sb_qn[0], e_sb_kn[0], tp)
    b_p = sb_attention_prompt(qn, knb, vb, bp, lp)
    b_s = sb_attention_sample(qn, knb, vb, cache_sb_k[0], cache_sb_v[0], tp, bs, ls)
    x = out_proj(x, (a_p, a_s), (b_p, b_s), e_w_out[0].astype(BF16))
    x = ffn(x, e_norm_ffn[0], e_w1[0].astype(BF16), e_w3[0].astype(BF16), e_w2[0].astype(BF16))

    proj = norm_matmul(x, o_norm_mix[0], o_w_in[0].astype(BF16))
    lru_w = (o_conv_w[0], o_conv_b[0], o_lru_wr[0], o_lru_br[0], o_lru_wi[0], o_lru_bi[0], o_lru_lam[0])
    c_p, p_conv, p_lru = lru_mixer(proj, 0, bp, lp, None, None, *lru_w)
    c_s, s_conv, s_lru = lru_mixer(proj, tp, bs, ls, state_conv[0], state_lru[0], *lru_w)
    bias = _band_bias(o_cb_bias[0])
    keep_p = min(CB_PAST, lp)
    d_p, p_cb_k, p_cb_v = cb_attention(proj, 0, bp, lp, keep_p, None, None, o_cb_qn[0], o_cb_kn[0], bias)
    d_s, s_cb_k, s_cb_v = cb_attention(proj, tp, bs, ls, ls, cache_cb_k[0].reshape(bs, CB_PAST, half_d),
                                       cache_cb_v[0].reshape(bs, CB_PAST, half_d), o_cb_qn[0], o_cb_kn[0], bias)
    x = out_proj(x, (c_p, c_s), (d_p, d_s), o_w_out[0].astype(BF16))
    y_p, y_s = moe_layer(x, o_norm_ffn[0], o_router[0], o_w1[0], o_w3[0], o_w2[0], tp)

    sbh = (SB_HEADS, half_d // SB_HEADS)
    cbh = (CB_HEADS, half_d // CB_HEADS)
    return (
        y_p.reshape(bp, lp, d), y_s.reshape(bs, ls, d),
        p_ret[None], p_sb_k.reshape(1, bp, lp, *sbh), p_sb_v.reshape(1, bp, lp, *sbh),
        p_conv[None], p_lru.reshape(1, bp, half_d),
        p_cb_k.reshape(1, bp, keep_p, *cbh), p_cb_v.reshape(1, bp, keep_p, *cbh),
        s_ret[None], s_sb_k.reshape(1, bs, ls, *sbh), s_sb_v.reshape(1, bs, ls, *sbh),
        s_conv[None], s_lru.reshape(1, bs, half_d),
        s_cb_k.reshape(1, bs, ls, *cbh), s_cb_v.reshape(1, bs, ls, *cbh),
    )
```

```python
import functools
import math

import jax
import jax.numpy as jnp
from jax import lax
from jax.experimental import pallas as pl
from jax.experimental.pallas import tpu as pltpu

F32 = jnp.float32
BF16 = jnp.bfloat16

NORM_EPS = 1e-6
ROPE_BASE = 10000.0
CHUNK = 64
RET_HEADS = 4
SB_HEADS = 8
CB_HEADS = 8
LRU_BLOCKS = 8
CONV_W = 4
LRU_C = 8.0
CB_PREV_CHUNKS = 8
CB_PAST = CB_PREV_CHUNKS * CHUNK
CB_BAND = (CB_PREV_CHUNKS + 1) * CHUNK
CB_MAX_REL = 128
NEG_BIG = -1e30
TOP_K = 2
LANES = 128


def _row_tile(m, cap=512):
    t = cap
    while m % t:
        t //= 2
    return t


def _rms(x, g):
    ms = jnp.mean(x * x, axis=-1, keepdims=True)
    return x * lax.rsqrt(ms + NORM_EPS) * g


def _dot(a, b):
    return jnp.dot(a, b, preferred_element_type=F32)


def _dot_nt(a, b):
    return lax.dot_general(a, b, (((1,), (1,)), ((), ())), preferred_element_type=F32)


def _dot_tn(a, b):
    return lax.dot_general(a, b, (((0,), (0,)), ((), ())), preferred_element_type=F32)


def _split(x):
    hi = x.astype(BF16)
    lo = (x - hi.astype(F32)).astype(BF16)
    return hi, lo


def _dot3(a, b):
    a_hi, a_lo = _split(a)
    b_hi, b_lo = _split(b)
    return _dot(a_hi, b_hi) + _dot(a_hi, b_lo) + _dot(a_lo, b_hi)


def _softplus(z):
    return jnp.maximum(z, 0.0) + jnp.log(1.0 + jnp.exp(-jnp.abs(z)))


def _sigmoid(z):
    return 1.0 / (1.0 + jnp.exp(-z))


def _group_specs(shape, n_first):
    return (pl.BlockSpec(shape, lambda i, *_: (jnp.minimum(i, n_first - 1), 0)),
            pl.BlockSpec(shape, lambda i, *_: (jnp.maximum(i - n_first, 0), 0)))


def _group_col_specs(shape, n_first):
    return (pl.BlockSpec(shape, lambda i, j: (jnp.minimum(i, n_first - 1), j)),
            pl.BlockSpec(shape, lambda i, j: (jnp.maximum(i - n_first, 0), j)))


def _as_groups(x, tm=None):
    if not isinstance(x, tuple):
        x = (x, x[:0])
    m = x[0].shape[0] + x[1].shape[0]
    if tm is None:
        tm = _row_tile(math.gcd(x[0].shape[0], x[1].shape[0]))
    rest = x[1] if x[1].shape[0] else x[0]
    return x[0], rest, m, tm, x[0].shape[0] // tm


def _norm_matmul_kernel(x1_ref, x2_ref, g_ref, w_ref, o_ref, h_ref, *, n_first):
    first = pl.program_id(0) < n_first
    start = pl.program_id(1) == 0

    @pl.when(start & first)
    def _():
        h_ref[...] = _rms(x1_ref[...], g_ref[...]).astype(BF16)

    @pl.when(start & jnp.logical_not(first))
    def _():
        h_ref[...] = _rms(x2_ref[...], g_ref[...]).astype(BF16)

    o_ref[...] = _dot(h_ref[...], w_ref[...])


def norm_matmul(x, g, w, tn=1024):
    k, n = w.shape
    x1, x2, m, tm, n_first = _as_groups(x)
    return pl.pallas_call(
        functools.partial(_norm_matmul_kernel, n_first=n_first),
        grid=(m // tm, n // tn),
        in_specs=[
            *_group_specs((tm, k), n_first),
            pl.BlockSpec((1, k), lambda i, j: (0, 0)),
            pl.BlockSpec((k, tn), lambda i, j: (0, j)),
        ],
        out_specs=pl.BlockSpec((tm, tn), lambda i, j: (i, j)),
        out_shape=jax.ShapeDtypeStruct((m, n), F32),
        scratch_shapes=[pltpu.VMEM((tm, k), BF16)],
        compiler_params=pltpu.CompilerParams(dimension_semantics=("parallel", "arbitrary")),
        name="norm_matmul",
    )(x1, x2, g.reshape(1, k), w)


def _out_proj_kernel(x1_ref, x2_ref, a1_ref, a2_ref, b1_ref, b2_ref, wa_ref, wb_ref, o_ref, *, n_first, x_split):
    i = pl.program_id(0)

    @pl.when(i < n_first)
    def _():
        o_ref[...] = x1_ref[...] + _dot(a1_ref[...], wa_ref[...]) + _dot(b1_ref[...], wb_ref[...])

    @pl.when(i >= n_first)
    def _():
        x_ref = x2_ref if x_split else x1_ref
        o_ref[...] = x_ref[...] + _dot(a2_ref[...], wa_ref[...]) + _dot(b2_ref[...], wb_ref[...])


def out_proj(x, a, b, w, tn=1024):
    n = w.shape[1]
    kh = a[0].shape[1]
    a1, a2, m, tm, n_first = _as_groups(a)
    b1, b2 = b
    x1, x2, _, _, x_first = _as_groups(x, tm)
    assert x_first in (n_first, m // tm)
    return pl.pallas_call(
        functools.partial(_out_proj_kernel, n_first=n_first, x_split=x_first == n_first),
        grid=(m // tm, n // tn),
        in_specs=[
            *_group_col_specs((tm, tn), x_first),
            *_group_specs((tm, kh), n_first),
            *_group_specs((tm, kh), n_first),
            pl.BlockSpec((kh, tn), lambda i, j: (0, j)),
            pl.BlockSpec((kh, tn), lambda i, j: (1, j)),
        ],
        out_specs=pl.BlockSpec((tm, tn), lambda i, j: (i, j)),
        out_shape=jax.ShapeDtypeStruct((m, n), F32),
        compiler_params=pltpu.CompilerParams(dimension_semantics=("parallel", "arbitrary")),
        name="out_proj",
    )(x1, x2, a1, a2, b1, b2, w, w)


def _ffn_kernel(x_ref, g_ref, w1_ref, w3_ref, w2_ref, o_ref, h_ref):
    @pl.when(pl.program_id(1) == 0)
    def _():
        x = x_ref[...]
        h_ref[...] = _rms(x, g_ref[...]).astype(BF16)
        o_ref[...] = x

    h = h_ref[...]
    a = _dot(h, w1_ref[...])
    u = _dot(h, w3_ref[...])
    o_ref[...] += _dot((a * _sigmoid(a) * u).astype(BF16), w2_ref[...])


FFN_VMEM_LIMIT = 56 * 1024 * 1024


def ffn(x, g, w1, w3, w2, row0, rows, tm, tf=512):
    d = x.shape[1]
    ff = w1.shape[1]
    assert row0 % tm == 0 and rows % tm == 0
    blk0 = row0 // tm
    return pl.pallas_call(
        _ffn_kernel,
        grid=(rows // tm, ff // tf),
        in_specs=[
            pl.BlockSpec((tm, d), lambda i, f: (blk0 + i, 0), pipeline_mode=pl.Buffered(1)),
            pl.BlockSpec((1, d), lambda i, f: (0, 0)),
            pl.BlockSpec((d, tf), lambda i, f: (0, f)),
            pl.BlockSpec((d, tf), lambda i, f: (0, f)),
            pl.BlockSpec((tf, d), lambda i, f: (f, 0)),
        ],
        out_specs=pl.BlockSpec((tm, d), lambda i, f: (i, 0)),
        out_shape=jax.ShapeDtypeStruct((rows, d), F32),
        scratch_shapes=[pltpu.VMEM((tm, d), BF16)],
        compiler_params=pltpu.CompilerParams(dimension_semantics=("parallel", "arbitrary"),
                                             vmem_limit_bytes=FFN_VMEM_LIMIT),
        name="ffn",
    )(x, g.reshape(1, d), w1, w3, w2)


def _router_kernel(x_ref, g_ref, r_ref, eid_ref, p_ref, rank_ref, cnt_ref, run_ref):
    i = pl.program_id(0)

    @pl.when(i == 0)
    def _():
        run_ref[...] = jnp.zeros_like(run_ref)

    h = _rms(x_ref[...], g_ref[...])
    tm, d = h.shape
    logits = _dot3(h, r_ref[...])
    n_e = logits.shape[1]
    lane = lax.broadcasted_iota(jnp.int32, logits.shape, 1).astype(F32)
    m1 = jnp.max(logits, axis=1, keepdims=True)
    i1 = jnp.min(jnp.where(logits == m1, lane, float(n_e)), axis=1, keepdims=True)
    rest = jnp.where(lane == i1, -jnp.inf, logits)
    m2 = jnp.max(rest, axis=1, keepdims=True)
    i2 = jnp.min(jnp.where(rest == m2, lane, float(n_e)), axis=1, keepdims=True)
    e2 = jnp.exp(m2 - m1)
    p1 = 1.0 / (1.0 + e2)
    slot = lax.broadcasted_iota(jnp.int32, (tm, TOP_K), 1)
    eid_ref[...] = jnp.where(slot == 0, i1, i2).astype(jnp.int32)
    p_ref[...] = jnp.where(slot == 0, p1, e2 * p1)

    hit1 = lane == i1
    hit2 = lane == i2
    both = jnp.where(hit1 | hit2, 1.0, 0.0)
    earlier = (lax.broadcasted_iota(jnp.int32, (tm, tm), 1) < lax.broadcasted_iota(jnp.int32, (tm, tm), 0))
    before = _dot(jnp.where(earlier, 1.0, 0.0).astype(BF16), both.astype(BF16)) + run_ref[...]
    r1 = jnp.sum(jnp.where(hit1, before, 0.0), axis=1, keepdims=True)
    r2 = jnp.sum(jnp.where(hit2, before, 0.0), axis=1, keepdims=True)
    rank_ref[...] = jnp.where(slot == 0, r1, r2).astype(jnp.int32)
    run_ref[...] += jnp.sum(both, axis=0, keepdims=True)
    cnt_ref[...] = run_ref[...].astype(jnp.int32)


def moe_route(x, g, router):
    m, d = x.shape
    n_e = router.shape[1]
    tm = _row_tile(m)
    pair = pl.BlockSpec((tm, TOP_K), lambda i: (i, 0))
    return pl.pallas_call(
        _router_kernel,
        grid=(m // tm,),
        in_specs=[
            pl.BlockSpec((tm, d), lambda i: (i, 0)),
            pl.BlockSpec((1, d), lambda i: (0, 0)),
            pl.BlockSpec((d, n_e), lambda i: (0, 0)),
        ],
        out_specs=[pair, pair, pair, pl.BlockSpec((1, n_e), lambda i: (0, 0))],
        out_shape=[
            jax.ShapeDtypeStruct((m, TOP_K), jnp.int32),
            jax.ShapeDtypeStruct((m, TOP_K), F32),
            jax.ShapeDtypeStruct((m, TOP_K), jnp.int32),
            jax.ShapeDtypeStruct((1, n_e), jnp.int32),
        ],
        scratch_shapes=[pltpu.VMEM((1, n_e), F32)],
        compiler_params=pltpu.CompilerParams(dimension_semantics=("arbitrary",)),
        name="moe_route",
    )(x, g.reshape(1, d), router)


SCALAR_UNROLL = 8


def _row_copy(src, s, dst, t, sem):
    return pltpu.make_async_copy(src.at[pl.ds(s, 1)], dst.at[pl.ds(t, 1)], sem)


def _invert_kernel(pos_ref, src_ref):
    def clear(r, _):
        src_ref[r] = 0
        return 0

    def put(t, _):
        for k in range(TOP_K):
            src_ref[pos_ref[TOP_K * t + k]] = t
        return 0

    lax.fori_loop(0, src_ref.shape[0], clear, 0, unroll=SCALAR_UNROLL)
    lax.fori_loop(0, pos_ref.shape[0] // TOP_K, put, 0, unroll=SCALAR_UNROLL)


def moe_invert(pos, n_rows):
    assert n_rows % SCALAR_UNROLL == 0 and pos.shape[0] % SCALAR_UNROLL == 0
    return pl.pallas_call(
        _invert_kernel,
        in_specs=[pl.BlockSpec(memory_space=pltpu.SMEM)],
        out_specs=pl.BlockSpec(memory_space=pltpu.SMEM),
        out_shape=jax.ShapeDtypeStruct((n_rows,), jnp.int32),
        name="moe_invert",
    )(pos.reshape(-1))


def _gather_kernel(src_ref, nu_ref, x_ref, g_ref, hs_ref, xs_ref, sem):
    r = pl.program_id(0)
    tm = hs_ref.shape[0]
    slot = r % 2

    def issue(tile, into):
        def body(j, _):
            _row_copy(x_ref, src_ref[tile * tm + j], xs_ref.at[into], j, sem.at[into]).start()
            return 0
        lax.fori_loop(0, tm, body, 0, unroll=SCALAR_UNROLL)

    @pl.when(r == 0)
    def _():
        issue(0, 0)

    @pl.when(r + 1 < nu_ref[0])
    def _():
        issue(r + 1, 1 - slot)

    @pl.when(r < nu_ref[0])
    def _():
        def drain(j, _):
            _row_copy(x_ref, 0, xs_ref.at[slot], 0, sem.at[slot]).wait()
            return 0

        lax.fori_loop(0, tm, drain, 0, unroll=SCALAR_UNROLL)
        hs_ref[...] = _rms(xs_ref[slot], g_ref[...]).astype(BF16)

    @pl.when(r >= nu_ref[0])
    def _():
        hs_ref[...] = jnp.zeros_like(hs_ref)


def moe_gather(src, n_used, x, g, tm):
    n_rows = src.shape[0]
    d = x.shape[1]
    return pl.pallas_call(
        _gather_kernel,
        grid_spec=pltpu.PrefetchScalarGridSpec(
            num_scalar_prefetch=2,
            grid=(n_rows // tm,),
            in_specs=[pl.BlockSpec(memory_space=pl.ANY), pl.BlockSpec((1, d), lambda r, src, nu: (0, 0))],
            out_specs=pl.BlockSpec((tm, d), lambda r, src, nu: (r, 0)),
            scratch_shapes=[pltpu.VMEM((2, tm, d), F32), pltpu.SemaphoreType.DMA((2,))],
        ),
        out_shape=jax.ShapeDtypeStruct((n_rows, d), BF16),
        compiler_params=pltpu.CompilerParams(dimension_semantics=("arbitrary",)),
        name="moe_gather",
    )(src, n_used, x, g.reshape(1, d))


def _expert_rows(first_ref, count_ref, src_ref, dst_ref, ibuf_ref, obuf_ref, sem_in, sem_out, compute):
    e = pl.program_id(0)
    c = pl.program_id(1)
    tm = ibuf_ref.shape[1]
    width = obuf_ref.shape[2]
    first = first_ref[e]
    n_real = count_ref[e]
    n_all = jnp.where(e == pl.num_programs(0) - 1, dst_ref.shape[0] // tm - first, n_real)
    cols = pl.ds(pl.multiple_of(c * width, width), width)

    def rows(k, base=first):
        return pl.ds(pl.multiple_of((base + k) * tm, tm), tm)

    def load(k, slot, base=first):
        return pltpu.make_async_copy(src_ref.at[rows(k, base)], ibuf_ref.at[slot], sem_in.at[slot])

    def store(k, slot):
        return pltpu.make_async_copy(obuf_ref.at[slot], dst_ref.at[rows(k), cols], sem_out.at[slot])

    @pl.when((e == 0) & (c == 0) & (n_real > 0))
    def _():
        load(0, 0).start()

    def body(k, _):
        slot = k % 2

        @pl.when(k >= 2)
        def _():
            store(k - 2, slot).wait()

        @pl.when(k < n_real)
        def _():
            load(k, slot).wait()

            @pl.when(k + 1 < n_real)
            def _():
                load(k + 1, 1 - slot).start()

            obuf_ref[slot] = compute(ibuf_ref[slot])

        @pl.when(k >= n_real)
        def _():
            obuf_ref[slot] = jnp.zeros(obuf_ref.shape[1:], obuf_ref.dtype)

        store(k, slot).start()
        return 0

    lax.fori_loop(0, n_all, body, 0)

    last_c = c == pl.num_programs(1) - 1
    e_next = jnp.where(last_c, jnp.minimum(e + 1, pl.num_programs(0) - 1), e)
    more = jnp.logical_not(last_c & (e == pl.num_programs(0) - 1))

    @pl.when(more & (count_ref[e_next] > 0))
    def _():
        load(0, 0, first_ref[e_next]).start()

    for back in (2, 1):
        @pl.when(n_all >= back)
        def _():
            store(n_all - back, (n_all - back) % 2).wait()


def _moe_up_kernel(first_ref, count_ref, hs_ref, w1_ref, w3_ref, act_ref, w1b_ref, w3b_ref, ibuf_ref, obuf_ref,
                   sem_in, sem_out):
    w1b_ref[...] = w1_ref[...].astype(BF16)
    w3b_ref[...] = w3_ref[...].astype(BF16)

    def compute(h):
        a = _dot(h, w1b_ref[...])
        u = _dot(h, w3b_ref[...])
        return (a * _sigmoid(a) * u).astype(BF16)

    _expert_rows(first_ref, count_ref, hs_ref, act_ref, ibuf_ref, obuf_ref, sem_in, sem_out, compute)


def _moe_down_kernel(first_ref, count_ref, act_ref, w2_ref, y_ref, w2b_ref, ibuf_ref, obuf_ref, sem_in, sem_out):
    w2b_ref[...] = w2_ref[...].astype(BF16)
    _expert_rows(first_ref, count_ref, act_ref, y_ref, ibuf_ref, obuf_ref, sem_in, sem_out,
                 lambda a: _dot(a, w2b_ref[...]))


def moe_experts(tile_first, tile_count, hs, w1, w3, w2, tm, tf=512, tn=512):
    n_rows, d = hs.shape
    n_e, _, ff = w1.shape
    any_spec = pl.BlockSpec(memory_space=pl.ANY)
    sems = [pltpu.SemaphoreType.DMA((2,)), pltpu.SemaphoreType.DMA((2,))]
    act = pl.pallas_call(
        _moe_up_kernel,
        grid_spec=pltpu.PrefetchScalarGridSpec(
            num_scalar_prefetch=2,
            grid=(n_e, ff // tf),
            in_specs=[
                any_spec,
                pl.BlockSpec((None, d, tf), lambda e, f, *_: (e, 0, f)),
                pl.BlockSpec((None, d, tf), lambda e, f, *_: (e, 0, f)),
            ],
            out_specs=any_spec,
            scratch_shapes=[pltpu.VMEM((d, tf), BF16), pltpu.VMEM((d, tf), BF16),
                            pltpu.VMEM((2, tm, d), BF16), pltpu.VMEM((2, tm, tf), BF16), *sems],
        ),
        out_shape=jax.ShapeDtypeStruct((n_rows, ff), BF16),
        compiler_params=pltpu.CompilerParams(dimension_semantics=("arbitrary", "arbitrary"),
                                             vmem_limit_bytes=48 * 1024 * 1024),
        name="moe_up",
    )(tile_first, tile_count, hs, w1, w3)
    return pl.pallas_call(
        _moe_down_kernel,
        grid_spec=pltpu.PrefetchScalarGridSpec(
            num_scalar_prefetch=2,
            grid=(n_e, d // tn),
            in_specs=[any_spec, pl.BlockSpec((None, ff, tn), lambda e, n, *_: (e, 0, n))],
            out_specs=any_spec,
            scratch_shapes=[pltpu.VMEM((ff, tn), BF16),
                            pltpu.VMEM((2, tm, ff), BF16), pltpu.VMEM((2, tm, tn), F32), *sems],
        ),
        out_shape=jax.ShapeDtypeStruct((n_rows, d), F32),
        compiler_params=pltpu.CompilerParams(dimension_semantics=("arbitrary", "arbitrary"),
                                             vmem_limit_bytes=56 * 1024 * 1024),
        name="moe_down",
    )(tile_first, tile_count, act, w2)


def _combine_kernel(pos_ref, x_ref, p_ref, y_ref, o1_ref, o2_ref, ya_ref, yb_ref, sem, *, n_first):
    tm = x_ref.shape[0]
    i = pl.program_id(0)
    slot = i % 2

    def issue(tile, into):
        def body(j, _):
            t = tile * tm + j
            _row_copy(y_ref, pos_ref[TOP_K * t], ya_ref.at[into], j, sem.at[into]).start()
            _row_copy(y_ref, pos_ref[TOP_K * t + 1], yb_ref.at[into], j, sem.at[into]).start()
            return 0
        lax.fori_loop(0, tm, body, 0, unroll=SCALAR_UNROLL)

    def drain(j, _):
        _row_copy(y_ref, 0, ya_ref.at[slot], 0, sem.at[slot]).wait()
        _row_copy(y_ref, 0, yb_ref.at[slot], 0, sem.at[slot]).wait()
        return 0

    @pl.when(i == 0)
    def _():
        issue(0, 0)

    @pl.when(i + 1 < pl.num_programs(0))
    def _():
        issue(i + 1, 1 - slot)

    lax.fori_loop(0, tm, drain, 0, unroll=SCALAR_UNROLL)
    p = p_ref[...]
    out = x_ref[...] + p[:, 0:1] * ya_ref[slot] + p[:, 1:2] * yb_ref[slot]

    @pl.when(i < n_first)
    def _():
        o1_ref[...] = out

    @pl.when(i >= n_first)
    def _():
        o2_ref[...] = out


def moe_combine(pos, x, p, y, m_first, tm=128):
    m, d = x.shape
    assert m_first % tm == 0 and m % tm == 0
    n_first = m_first // tm
    return pl.pallas_call(
        functools.partial(_combine_kernel, n_first=n_first),
        grid_spec=pltpu.PrefetchScalarGridSpec(
            num_scalar_prefetch=1,
            grid=(m // tm,),
            in_specs=[
                pl.BlockSpec((tm, d), lambda i, pos: (i, 0)),
                pl.BlockSpec((tm, TOP_K), lambda i, pos: (i, 0)),
                pl.BlockSpec(memory_space=pl.ANY),
            ],
            out_specs=list(_group_specs((tm, d), n_first)),
            scratch_shapes=[pltpu.VMEM((2, tm, d), F32), pltpu.VMEM((2, tm, d), F32),
                            pltpu.SemaphoreType.DMA((2,))],
        ),
        out_shape=[jax.ShapeDtypeStruct((m_first, d), F32), jax.ShapeDtypeStruct((m - m_first, d), F32)],
        compiler_params=pltpu.CompilerParams(dimension_semantics=("arbitrary",)),
        name="moe_combine",
    )(pos.reshape(-1), x, p, y)


def moe_layer(x, g, router, w1, w3, w2, m_first, tm=512):
    m = x.shape[0]
    n_e = router.shape[1]
    eid, p, rank, counts = moe_route(x, g, router)
    padded = (counts[0] + tm - 1) // tm * tm
    ends = jnp.cumsum(padded)
    starts = ends - padded
    experts = jnp.arange(n_e, dtype=jnp.int32)
    pos = rank + jnp.sum(jnp.where(eid[:, :, None] == experts, starts, 0), axis=-1)
    n_tiles = -(-TOP_K * m // tm) + n_e
    n_used = (ends[-1:] // tm).astype(jnp.int32)
    hs = moe_gather(moe_invert(pos, n_tiles * tm), n_used, x, g, tm)
    y = moe_experts((starts // tm).astype(jnp.int32), (padded // tm).astype(jnp.int32), hs, w1, w3, w2, tm)
    return moe_combine(pos, x, p, y, m_first)


def _ret_kernel(q_ref, k_ref, v_ref, g_ref, cos_ref, sin_ref, gn_ref, s0_ref, o_ref, s_ref, st_ref,
                *, has_state):
    n = pl.program_id(1)
    c = q_ref.shape[0]
    dk = q_ref.shape[1] // RET_HEADS
    half = dk // 2

    @pl.when(n == 0)
    def _():
        if has_state:
            st_ref[...] = s0_ref[...]
        else:
            st_ref[...] = jnp.zeros_like(st_ref)

    cos = cos_ref[...]
    sin = sin_ref[...]
    row = lax.broadcasted_iota(jnp.int32, (c, c), 0)
    col = lax.broadcasted_iota(jnp.int32, (c, c), 1)
    diff = (row - col).astype(F32)
    ridx = lax.broadcasted_iota(jnp.int32, (c, 1), 0).astype(F32)

    def rot(x):
        x1, x2 = x[:, :half], x[:, half:]
        return jnp.concatenate([x1 * cos - x2 * sin, x1 * sin + x2 * cos], axis=1)

    for h in range(RET_HEADS):
        log_g = math.log1p(-(2.0 ** (-5.0 - h)))
        sl = slice(h * dk, (h + 1) * dk)
        qr = rot(q_ref[:, sl]) * (dk ** -0.5)
        kr = rot(k_ref[:, sl])
        vh = v_ref[:, sl].astype(BF16)
        decay = jnp.where(diff >= 0, jnp.exp(log_g * jnp.maximum(diff, 0.0)), 0.0)
        scores = _dot_nt(qr.astype(BF16), kr.astype(BF16)) * decay
        s = st_ref[h]
        q_dec = jnp.exp(log_g * (ridx + 1.0))
        k_dec = jnp.exp(log_g * (c - 1.0 - ridx))
        o = _dot(scores.astype(BF16), vh) + _dot((qr * q_dec).astype(BF16), s.astype(BF16))
        st_ref[h] = s * math.exp(log_g * c) + _dot_tn((kr * k_dec).astype(BF16), vh)
        gh = g_ref[:, sl]
        o_ref[:, sl] = (gh * _sigmoid(gh) * _rms(o, gn_ref[h:h + 1, :])).astype(BF16)

    @pl.when(n == pl.num_programs(1) - 1)
    def _():
        s_ref[...] = st_ref[...]


def retention_mixer(qkv, row0, batch, seq, cos, sin, gn, s0):
    half_d = qkv.shape[1] // 7
    dk = half_d // RET_HEADS
    c = next((t for t in (2 * CHUNK, CHUNK) if seq % t == 0 and row0 % t == 0), seq)
    nc = seq // c
    blk0 = row0 // c
    has_state = s0 is not None
    if not has_state:
        s0 = jnp.zeros((1, RET_HEADS, dk, dk), F32)

    def col(j):
        return pl.BlockSpec((c, half_d), lambda b, n: (blk0 + b * nc + n, j))

    return pl.pallas_call(
        functools.partial(_ret_kernel, has_state=has_state),
        grid=(batch, nc),
        in_specs=[
            col(0), col(1), col(2), col(3),
            pl.BlockSpec((c, dk // 2), lambda b, n: (n, 0)),
            pl.BlockSpec((c, dk // 2), lambda b, n: (n, 0)),
            pl.BlockSpec((RET_HEADS, dk), lambda b, n: (0, 0)),
            pl.BlockSpec((None, RET_HEADS, dk, dk), lambda b, n: (b if has_state else 0, 0, 0, 0)),
        ],
        out_specs=[
            pl.BlockSpec((c, half_d), lambda b, n: (b * nc + n, 0)),
            pl.BlockSpec((None, RET_HEADS, dk, dk), lambda b, n: (b, 0, 0, 0)),
        ],
        out_shape=[
            jax.ShapeDtypeStruct((batch * seq, half_d), BF16),
            jax.ShapeDtypeStruct((batch, RET_HEADS, dk, dk), F32),
        ],
        scratch_shapes=[pltpu.VMEM((RET_HEADS, dk, dk), F32)],
        compiler_params=pltpu.CompilerParams(dimension_semantics=("parallel", "arbitrary")),
        name="retention",
    )(qkv, qkv, qkv, qkv, cos, sin, gn, s0)


def _sb_norm_kernel(q_ref, k_ref, v_ref, qg_ref, kg_ref, qn_ref, knb_ref, vb_ref, k1_ref, k2_ref, v1_ref, v2_ref,
                    *, n_first):
    dh = qg_ref.shape[1]
    first = pl.program_id(0) < n_first
    for h in range(q_ref.shape[1] // dh):
        sl = slice(h * dh, (h + 1) * dh)
        qn_ref[:, sl] = (_rms(q_ref[:, sl], qg_ref[...]) * (dh ** -0.5)).astype(BF16)
        kn = _rms(k_ref[:, sl], kg_ref[...])
        knb_ref[:, sl] = kn.astype(BF16)

        @pl.when(first)
        def _():
            k1_ref[:, sl] = kn

        @pl.when(jnp.logical_not(first))
        def _():
            k2_ref[:, sl] = kn

    v = v_ref[...]
    vb_ref[...] = v.astype(BF16)

    @pl.when(first)
    def _():
        v1_ref[...] = v

    @pl.when(jnp.logical_not(first))
    def _():
        v2_ref[...] = v


def sb_norm(qkv, qg, kg, m_first):
    m = qkv.shape[0]
    half_d = qkv.shape[1] // 7
    dh = qg.shape[0]
    tm = _row_tile(math.gcd(m_first, m - m_first))
    n_first = m_first // tm

    def col(j):
        return pl.BlockSpec((tm, half_d), lambda i: (i, j))

    out = pl.BlockSpec((tm, half_d), lambda i: (i, 0))
    gspec = pl.BlockSpec((1, dh), lambda i: (0, 0))
    split = _group_specs((tm, half_d), n_first)
    f32_first = jax.ShapeDtypeStruct((m_first, half_d), F32)
    f32_rest = jax.ShapeDtypeStruct((m - m_first, half_d), F32)
    return pl.pallas_call(
        functools.partial(_sb_norm_kernel, n_first=n_first),
        grid=(m // tm,),
        in_specs=[col(4), col(5), col(6), gspec, gspec],
        out_specs=[out] * 3 + [*split, *split],
        out_shape=[jax.ShapeDtypeStruct((m, half_d), BF16)] * 3 + [f32_first, f32_rest, f32_first, f32_rest],
        compiler_params=pltpu.CompilerParams(dimension_semantics=("arbitrary",)),
        name="sb_norm",
    )(qkv, qkv, qkv, qg.reshape(1, dh), kg.reshape(1, dh))


SB_EXIT_LOG = 88.0


def _sb_block(q, kb, vb, carry, acc, strict_diag):
    tq, tk = q.shape[0], kb.shape[0]
    z = _dot_nt(q, kb)
    sp = _softplus(z)
    log_keep = -sp
    if strict_diag:
        mask = (lax.broadcasted_iota(jnp.int32, (tq, tk), 1) < lax.broadcasted_iota(jnp.int32, (tq, tk), 0))
        log_keep = jnp.where(mask, log_keep, 0.0)
    later = (lax.broadcasted_iota(jnp.int32, (tk, tk), 0) > lax.broadcasted_iota(jnp.int32, (tk, tk), 1))
    u = jnp.where(later, 1.0, 0.0).astype(BF16)
    hi, lo = _split(log_keep)
    after = _dot(hi, u) + _dot(lo, u) + carry
    w = jnp.exp(z - sp + after)
    if strict_diag:
        w = jnp.where(mask, w, 0.0)
    return jnp.sum(log_keep, axis=1, keepdims=True), acc + _dot(w.astype(BF16), vb)


def _sb_visit(q_ref, carry_ref, acc_ref, key_block, strict_diag):
    heads = q_ref.shape[1] // LANES
    worst = None
    for h in range(heads):
        sl = slice(h * LANES, (h + 1) * LANES)
        kb, vb = key_block(h, sl)
        tk = kb.shape[0]
        carry = carry_ref[h]
        block_sum, acc = _sb_block(q_ref[:, sl], kb, vb, carry[:, :tk], acc_ref[:, sl], strict_diag)
        carry = carry + block_sum
        carry_ref[h] = carry
        acc_ref[:, sl] = acc
        worst = carry if worst is None else jnp.maximum(worst, carry)
    return jnp.max(worst)


def _sb_prompt_kernel(q_ref, k_ref, v_ref, o_ref, acc_ref, carry_ref):
    i = pl.program_id(1)
    tq = q_ref.shape[0]
    acc_ref[...] = jnp.zeros_like(acc_ref)
    carry_ref[...] = jnp.zeros_like(carry_ref)

    def visit(j, strict_diag):
        s = pl.multiple_of(j * tq, tq)
        return _sb_visit(q_ref, carry_ref, acc_ref,
                         lambda h, sl: (k_ref[pl.ds(s, tq), sl], v_ref[pl.ds(s, tq), sl]), strict_diag)

    top = visit(i, True)
    lax.while_loop(lambda st: (st[0] >= 0) & (st[1] > -SB_EXIT_LOG),
                   lambda st: (st[0] - 1, visit(st[0], False)), (i - 1, top))
    o_ref[...] = acc_ref[...].astype(BF16)


def sb_attention_prompt(qn, knb, vb, batch, seq, tq=LANES):
    width = qn.shape[1]
    nq = seq // tq
    return pl.pallas_call(
        _sb_prompt_kernel,
        grid=(batch, nq),
        in_specs=[
            pl.BlockSpec((tq, width), lambda b, i: (b * nq + i, 0)),
            pl.BlockSpec((seq, width), lambda b, i: (b, 0)),
            pl.BlockSpec((seq, width), lambda b, i: (b, 0)),
        ],
        out_specs=pl.BlockSpec((tq, width), lambda b, i: (b * nq + i, 0)),
        out_shape=jax.ShapeDtypeStruct((batch * seq, width), BF16),
        scratch_shapes=[pltpu.VMEM((tq, width), F32), pltpu.VMEM((width // LANES, tq, LANES), F32)],
        compiler_params=pltpu.CompilerParams(dimension_semantics=("parallel", "arbitrary")),
        name="sb_attention_prompt",
    )(qn, knb, vb)


def _sb_sample_kernel(q_ref, k_ref, v_ref, kp_ref, vp_ref, o_ref, acc_ref, carry_ref, kbuf_ref, vbuf_ref, sem):
    b = pl.program_id(0)
    tk = kbuf_ref.shape[1]
    n_past = kp_ref.shape[1] // tk
    acc_ref[...] = jnp.zeros_like(acc_ref)
    carry_ref[...] = jnp.zeros_like(carry_ref)

    def fetch(j):
        slot = (n_past - 1 - j) % 2
        rows = pl.ds(pl.multiple_of(j * tk, tk), tk)
        return (pltpu.make_async_copy(kp_ref.at[b, rows], kbuf_ref.at[slot], sem.at[0, slot]),
                pltpu.make_async_copy(vp_ref.at[b, rows], vbuf_ref.at[slot], sem.at[1, slot]))

    def start(j):
        for c in fetch(j):
            c.start()

    def wait(j):
        for c in fetch(j):
            c.wait()

    start(n_past - 1)
    top = _sb_visit(q_ref, carry_ref, acc_ref, lambda h, sl: (k_ref[:, sl], v_ref[:, sl]), True)

    def body(st):
        j = st[0]
        slot = (n_past - 1 - j) % 2
        wait(j)

        @pl.when(j >= 1)
        def _():
            start(j - 1)

        top = _sb_visit(q_ref, carry_ref, acc_ref,
                        lambda h, sl: (kbuf_ref[slot, :, h, :].astype(BF16), vbuf_ref[slot, :, h, :].astype(BF16)),
                        False)
        return j - 1, top

    j_end, _ = lax.while_loop(lambda st: (st[0] >= 0) & (st[1] > -SB_EXIT_LOG), body, (n_past - 1, top))

    @pl.when(j_end >= 0)
    def _():
        wait(j_end)

    o_ref[...] = acc_ref[...].astype(BF16)


def sb_attention_sample(qn, knb, vb, k_past, v_past, row0, batch, seq, tk=LANES):
    width = qn.shape[1]
    heads = width // LANES
    blk0 = row0 // seq
    assert k_past.shape[1] % tk == 0 and k_past.shape[2:] == (heads, LANES)
    new = pl.BlockSpec((seq, width), lambda b: (blk0 + b, 0))
    return pl.pallas_call(
        _sb_sample_kernel,
        grid=(batch,),
        in_specs=[new, new, new, pl.BlockSpec(memory_space=pl.ANY), pl.BlockSpec(memory_space=pl.ANY)],
        out_specs=pl.BlockSpec((seq, width), lambda b: (b, 0)),
        out_shape=jax.ShapeDtypeStruct((batch * seq, width), BF16),
        scratch_shapes=[
            pltpu.VMEM((seq, width), F32), pltpu.VMEM((heads, seq, LANES), F32),
            pltpu.VMEM((2, tk, heads, LANES), F32), pltpu.VMEM((2, tk, heads, LANES), F32),
            pltpu.SemaphoreType.DMA((2, 2)),
        ],
        compiler_params=pltpu.CompilerParams(dimension_semantics=("arbitrary",)),
        name="sb_attention_sample",
    )(qn, knb, vb, k_past, v_past)


def _lru_kernel(gc_ref, xc_ref, buf_ref, h0_ref, cw_ref, cb_ref, wr_ref, br_ref, wi_ref, bi_ref, lam_ref,
                o_ref, conv_ref, hl_ref, *, has_state):
    seq, bw = xc_ref.shape
    x = xc_ref[...]
    row = lax.broadcasted_iota(jnp.int32, (seq, bw), 0)
    cw = cw_ref[...]
    y = cb_ref[...] + cw[CONV_W - 1:CONV_W, :] * x
    for d in range(1, CONV_W):
        xs = pltpu.roll(x, d, 0)
        for r in range(d):
            prev = buf_ref[CONV_W - 1 - d + r:CONV_W - d + r, :] if has_state else jnp.zeros((1, bw), F32)
            xs = jnp.where(row == r, prev, xs)
        y = y + cw[CONV_W - 1 - d:CONV_W - d, :] * xs
    conv_ref[...] = x[seq - (CONV_W - 1):, :]

    r_gate = _sigmoid(_dot3(y, wr_ref[...]) + br_ref[...])
    i_gate = _sigmoid(_dot3(y, wi_ref[...]) + bi_ref[...])
    log_a = -LRU_C * r_gate * _softplus(-lam_ref[...])
    a = jnp.exp(log_a)
    u = jnp.sqrt(-jnp.tanh(log_a) * (a * a + 1.0)) * (i_gate * y)
    if has_state:
        u = jnp.where(row == 0, u + a * h0_ref[...], u)

    shift = 1
    while shift < seq:
        a_s = pltpu.roll(a, shift, 0)
        u_s = pltpu.roll(u, shift, 0)
        live = row >= shift
        u = jnp.where(live, a * u_s + u, u)
        a = jnp.where(live, a * a_s, a)
        shift *= 2

    hl_ref[...] = u[seq - 1:, :]
    gc = gc_ref[...]
    gelu = 0.5 * gc * (1.0 + jnp.tanh(math.sqrt(2.0 / math.pi) * (gc + 0.044715 * gc * gc * gc)))
    o_ref[...] = (gelu * u).astype(BF16)


def lru_mixer(proj, row0, batch, seq, conv_buf, h0, cw, cb, wr, br, wi, bi, lam):
    width = proj.shape[1] // 5
    bw = width // LRU_BLOCKS
    blk0 = row0 // seq
    has_state = conv_buf is not None
    if not has_state:
        conv_buf = jnp.zeros((1, CONV_W - 1, width), F32)
        h0 = jnp.zeros((1, width), F32)
    h0 = h0.reshape(-1, 1, width)

    def vec(k):
        return pl.BlockSpec((k, bw), lambda b, n: (0, n))

    wspec = pl.BlockSpec((None, bw, bw), lambda b, n: (n, 0, 0))
    return pl.pallas_call(
        functools.partial(_lru_kernel, has_state=has_state),
        grid=(batch, LRU_BLOCKS),
        in_specs=[
            pl.BlockSpec((seq, bw), lambda b, n: (blk0 + b, n)),
            pl.BlockSpec((seq, bw), lambda b, n: (blk0 + b, LRU_BLOCKS + n)),
            pl.BlockSpec((None, CONV_W - 1, bw), lambda b, n: (b if has_state else 0, 0, n)),
            pl.BlockSpec((None, 1, bw), lambda b, n: (b if has_state else 0, 0, n)),
            vec(CONV_W), vec(1), wspec, vec(1), wspec, vec(1), vec(1),
        ],
        out_specs=[
            pl.BlockSpec((seq, bw), lambda b, n: (b, n)),
            pl.BlockSpec((None, CONV_W - 1, bw), lambda b, n: (b, 0, n)),
            pl.BlockSpec((None, 1, bw), lambda b, n: (b, 0, n)),
        ],
        out_shape=[
            jax.ShapeDtypeStruct((batch * seq, width), BF16),
            jax.ShapeDtypeStruct((batch, CONV_W - 1, width), F32),
            jax.ShapeDtypeStruct((batch, 1, width), F32),
        ],
        compiler_params=pltpu.CompilerParams(dimension_semantics=("parallel", "parallel")),
        name="rg_lru",
    )(proj, proj, conv_buf, h0, cw, cb.reshape(1, width), wr, br.reshape(1, width), wi,
      bi.reshape(1, width), lam.reshape(1, width))


def _cb_kernel(q_ref, k_ref, v_ref, kp_ref, vp_ref, qg_ref, kg_ref, bias_ref, o_ref, ko_ref, vo_ref,
               kb_ref, vb_ref, *, has_past):
    seq, dh = q_ref.shape
    keep = ko_ref.shape[0]
    kn = _rms(k_ref[...], kg_ref[...])
    v = v_ref[...]
    ko_ref[...] = kn[seq - keep:, :]
    vo_ref[...] = v[seq - keep:, :]
    if has_past:
        kb_ref[:CB_PAST, :] = kp_ref[...].astype(BF16)
        vb_ref[:CB_PAST, :] = vp_ref[...].astype(BF16)
    else:
        kb_ref[:CB_PAST, :] = jnp.zeros((CB_PAST, dh), BF16)
        vb_ref[:CB_PAST, :] = jnp.zeros((CB_PAST, dh), BF16)
    kb_ref[CB_PAST:, :] = kn.astype(BF16)
    vb_ref[CB_PAST:, :] = v.astype(BF16)
    tq, span = bias_ref.shape
    bias = bias_ref[...]
    kidx = lax.broadcasted_iota(jnp.int32, (tq, span), 1)

    def body(n, _):
        r0 = pl.multiple_of(n * tq, tq)
        qn = (_rms(q_ref[pl.ds(r0, tq), :], qg_ref[...]) * (dh ** -0.5)).astype(BF16)
        s = _dot_nt(qn, kb_ref[pl.ds(r0, span), :]) + bias
        if not has_past:
            s = jnp.where(kidx + r0 >= CB_PAST, s, NEG_BIG)
        p = jnp.exp(s - jnp.max(s, axis=1, keepdims=True))
        o = _dot(p.astype(BF16), vb_ref[pl.ds(r0, span), :]) / jnp.sum(p, axis=1, keepdims=True)
        o_ref[pl.ds(r0, tq), :] = o.astype(BF16)
        return 0

    steps = seq // tq
    lax.fori_loop(0, steps, body, 0, unroll=min(steps, 2))


def _block_bias(bias, g):
    rows = [jnp.pad(bias, ((0, 0), (0, 0), (c * CHUNK, (g - 1 - c) * CHUNK)), constant_values=NEG_BIG)
            for c in range(g)]
    return jnp.concatenate(rows, axis=1)


def cb_attention(proj, row0, batch, seq, keep, k_past, v_past, qg, kg, bias):
    assert seq % CHUNK == 0
    bias = _block_bias(bias, 2 if (seq // CHUNK) % 2 == 0 else 1)
    half_d = proj.shape[1] // 5
    heads = half_d // LANES
    blk0 = row0 // seq
    has_past = k_past is not None
    if not has_past:
        k_past = jnp.zeros((1, CB_PAST, half_d), F32)
        v_past = k_past

    def col(j):
        return pl.BlockSpec((seq, LANES), lambda b, h: (blk0 + b, j * heads + h))

    old = pl.BlockSpec((None, CB_PAST, LANES), lambda b, h: (b if has_past else 0, 0, h))
    gspec = pl.BlockSpec((1, LANES), lambda b, h: (0, 0))
    kept = pl.BlockSpec((None, keep, LANES), lambda b, h: (b, 0, h))
    return pl.pallas_call(
        functools.partial(_cb_kernel, has_past=has_past),
        grid=(batch, heads),
        in_specs=[col(2), col(3), col(4), old, old, gspec, gspec,
                  pl.BlockSpec((None,) + bias.shape[1:], lambda b, h: (h, 0, 0))],
        out_specs=[pl.BlockSpec((seq, LANES), lambda b, h: (b, h)), kept, kept],
        out_shape=[
            jax.ShapeDtypeStruct((batch * seq, half_d), BF16),
            jax.ShapeDtypeStruct((batch, keep, half_d), F32),
            jax.ShapeDtypeStruct((batch, keep, half_d), F32),
        ],
        scratch_shapes=[pltpu.VMEM((CB_PAST + seq, LANES), BF16), pltpu.VMEM((CB_PAST + seq, LANES), BF16)],
        compiler_params=pltpu.CompilerParams(dimension_semantics=("parallel", "parallel")),
        name="cb_attention",
    )(proj, proj, proj, k_past, v_past, qg.reshape(1, LANES), kg.reshape(1, LANES), bias)


def _rope_tables(pos0, seq, half):
    inv = ROPE_BASE ** (-jnp.arange(half, dtype=F32) / half)
    ang = (pos0 + jnp.arange(seq)).astype(F32)[:, None] * inv[None, :]
    return jnp.cos(ang), jnp.sin(ang)


def _band_bias(table):
    lo = -(CHUNK - 1)
    n_flat = CB_PAST + CHUNK - 1 - CB_MAX_REL
    ext = jnp.concatenate([table[:, lo + CB_MAX_REL:], jnp.repeat(table[:, -1:], n_flat, axis=1)], axis=1)
    rev = ext[:, ::-1]
    n = rev.shape[1]
    skew = jnp.tile(rev, (1, CHUNK + 1))[:, :CHUNK * (n + 1)].reshape(-1, CHUNK, n + 1)
    return skew[:, ::-1, :CB_BAND]


def kernel(x_prompt, x_sample, state_ret, cache_sb_k, cache_sb_v, state_conv, state_lru, cache_cb_k, cache_cb_v, e_norm_mix, e_w_in, e_ret_gn, e_sb_qn, e_sb_kn, e_w_out, e_norm_ffn, e_w1, e_w3, e_w2, o_norm_mix, o_w_in, o_conv_w, o_conv_b, o_lru_wr, o_lru_br, o_lru_wi, o_lru_bi, o_lru_lam, o_cb_qn, o_cb_kn, o_cb_bias, o_w_out, o_norm_ffn, o_router, o_w1, o_w3, o_w2):
    bp, lp, d = x_prompt.shape
    bs, ls, _ = x_sample.shape
    half_d = d // 2
    tp = bp * lp
    past = cache_sb_k.shape[2]
    dk = half_d // RET_HEADS
    x = (x_prompt.reshape(tp, d), x_sample.reshape(bs * ls, d))

    qkv = norm_matmul(x, e_norm_mix[0], e_w_in[0].astype(BF16))
    cos_p, sin_p = _rope_tables(0, lp, dk // 2)
    cos_s, sin_s = _rope_tables(past, ls, dk // 2)
    a_p, p_ret = retention_mixer(qkv, 0, bp, lp, cos_p, sin_p, e_ret_gn[0], None)
    a_s, s_ret = retention_mixer(qkv, tp, bs, ls, cos_s, sin_s, e_ret_gn[0], state_ret[0])
    qn, knb, vb, p_sb_k, s_sb_k, p_sb_v, s_sb_v = sb_norm(qkv, e_sb_qn[0], e_sb_kn[0], tp)
    b_p = sb_attention_prompt(qn, knb, vb, bp, lp)
    b_s = sb_attention_sample(qn, knb, vb, cache_sb_k[0], cache_sb_v[0], tp, bs, ls)
    x = out_proj(x, (a_p, a_s), (b_p, b_s), e_w_out[0].astype(BF16))
    ffn_w = (e_norm_ffn[0], e_w1[0].astype(BF16), e_w3[0].astype(BF16), e_w2[0].astype(BF16))
    ts = bs * ls
    x = (ffn(x, *ffn_w, 0, tp, _row_tile(tp, 1024)), ffn(x, *ffn_w, tp, ts, _row_tile(math.gcd(tp, ts))))

    proj = norm_matmul(x, o_norm_mix[0], o_w_in[0].astype(BF16))
    lru_w = (o_conv_w[0], o_conv_b[0], o_lru_wr[0], o_lru_br[0], o_lru_wi[0], o_lru_bi[0], o_lru_lam[0])
    c_p, p_conv, p_lru = lru_mixer(proj, 0, bp, lp, None, None, *lru_w)
    c_s, s_conv, s_lru = lru_mixer(proj, tp, bs, ls, state_conv[0], state_lru[0], *lru_w)
    bias = _band_bias(o_cb_bias[0])
    keep_p = min(CB_PAST, lp)
    d_p, p_cb_k, p_cb_v = cb_attention(proj, 0, bp, lp, keep_p, None, None, o_cb_qn[0], o_cb_kn[0], bias)
    d_s, s_cb_k, s_cb_v = cb_attention(proj, tp, bs, ls, ls, cache_cb_k[0].reshape(bs, CB_PAST, half_d),
                                       cache_cb_v[0].reshape(bs, CB_PAST, half_d), o_cb_qn[0], o_cb_kn[0], bias)
    x = out_proj(x, (c_p, c_s), (d_p, d_s), o_w_out[0].astype(BF16))
    y_p, y_s = moe_layer(x, o_norm_ffn[0], o_router[0], o_w1[0], o_w3[0], o_w2[0], tp)

    sbh = (SB_HEADS, half_d // SB_HEADS)
    cbh = (CB_HEADS, half_d // CB_HEADS)
    return (
        y_p.reshape(bp, lp, d), y_s.reshape(bs, ls, d),
        p_ret[None], p_sb_k.reshape(1, bp, lp, *sbh), p_sb_v.reshape(1, bp, lp, *sbh),
        p_conv[None], p_lru.reshape(1, bp, half_d),
        p_cb_k.reshape(1, bp, keep_p, *cbh), p_cb_v.reshape(1, bp, keep_p, *cbh),
        s_ret[None], s_sb_k.reshape(1, bs, ls, *sbh), s_sb_v.reshape(1, bs, ls, *sbh),
        s_conv[None], s_lru.reshape(1, bs, half_d),
        s_cb_k.reshape(1, bs, ls, *cbh), s_cb_v.reshape(1, bs, ls, *cbh),
    )
```

```python
import functools
import math

import jax
import jax.numpy as jnp
from jax import lax
from jax.experimental import pallas as pl
from jax.experimental.pallas import tpu as pltpu

F32 = jnp.float32
BF16 = jnp.bfloat16

NORM_EPS = 1e-6
ROPE_BASE = 10000.0
CHUNK = 64
RET_HEADS = 4
SB_HEADS = 8
CB_HEADS = 8
LRU_BLOCKS = 8
CONV_W = 4
LRU_C = 8.0
CB_PREV_CHUNKS = 8
CB_PAST = CB_PREV_CHUNKS * CHUNK
CB_BAND = (CB_PREV_CHUNKS + 1) * CHUNK
CB_MAX_REL = 128
NEG_BIG = -1e30
TOP_K = 2
LANES = 128


def _row_tile(m, cap=512):
    t = cap
    while m % t:
        t //= 2
    return t


def _rms(x, g):
    ms = jnp.mean(x * x, axis=-1, keepdims=True)
    return x * lax.rsqrt(ms + NORM_EPS) * g


def _dot(a, b):
    return jnp.dot(a, b, preferred_element_type=F32)


def _dot_nt(a, b):
    return lax.dot_general(a, b, (((1,), (1,)), ((), ())), preferred_element_type=F32)


def _dot_tn(a, b):
    return lax.dot_general(a, b, (((0,), (0,)), ((), ())), preferred_element_type=F32)


def _split(x):
    hi = x.astype(BF16)
    lo = (x - hi.astype(F32)).astype(BF16)
    return hi, lo


def _dot3(a, b):
    a_hi, a_lo = _split(a)
    b_hi, b_lo = _split(b)
    return _dot(a_hi, b_hi) + _dot(a_hi, b_lo) + _dot(a_lo, b_hi)


def _softplus(z):
    return jnp.maximum(z, 0.0) + jnp.log(1.0 + jnp.exp(-jnp.abs(z)))


def _sigmoid(z):
    return 1.0 / (1.0 + jnp.exp(-z))


def _group_specs(shape, n_first):
    return (pl.BlockSpec(shape, lambda i, *_: (jnp.minimum(i, n_first - 1), 0)),
            pl.BlockSpec(shape, lambda i, *_: (jnp.maximum(i - n_first, 0), 0)))


def _group_col_specs(shape, n_first):
    return (pl.BlockSpec(shape, lambda i, j: (jnp.minimum(i, n_first - 1), j)),
            pl.BlockSpec(shape, lambda i, j: (jnp.maximum(i - n_first, 0), j)))


def _as_groups(x, tm=None):
    if not isinstance(x, tuple):
        x = (x, x[:0])
    m = x[0].shape[0] + x[1].shape[0]
    if tm is None:
        tm = _row_tile(math.gcd(x[0].shape[0], x[1].shape[0]))
    rest = x[1] if x[1].shape[0] else x[0]
    return x[0], rest, m, tm, x[0].shape[0] // tm


def _norm_matmul_kernel(x1_ref, x2_ref, g_ref, w_ref, o_ref, h_ref, *, n_first):
    first = pl.program_id(0) < n_first
    start = pl.program_id(1) == 0

    @pl.when(start & first)
    def _():
        h_ref[...] = _rms(x1_ref[...], g_ref[...]).astype(BF16)

    @pl.when(start & jnp.logical_not(first))
    def _():
        h_ref[...] = _rms(x2_ref[...], g_ref[...]).astype(BF16)

    o_ref[...] = _dot(h_ref[...], w_ref[...])


def norm_matmul(x, g, w, tn=1024):
    k, n = w.shape
    x1, x2, m, tm, n_first = _as_groups(x)
    return pl.pallas_call(
        functools.partial(_norm_matmul_kernel, n_first=n_first),
        grid=(m // tm, n // tn),
        in_specs=[
            *_group_specs((tm, k), n_first),
            pl.BlockSpec((1, k), lambda i, j: (0, 0)),
            pl.BlockSpec((k, tn), lambda i, j: (0, j)),
        ],
        out_specs=pl.BlockSpec((tm, tn), lambda i, j: (i, j)),
        out_shape=jax.ShapeDtypeStruct((m, n), F32),
        scratch_shapes=[pltpu.VMEM((tm, k), BF16)],
        compiler_params=pltpu.CompilerParams(dimension_semantics=("parallel", "arbitrary")),
        name="norm_matmul",
    )(x1, x2, g.reshape(1, k), w)


def _out_proj_kernel(x1_ref, x2_ref, a1_ref, a2_ref, b1_ref, b2_ref, wa_ref, wb_ref, o_ref, *, n_first, x_split):
    i = pl.program_id(0)

    @pl.when(i < n_first)
    def _():
        o_ref[...] = x1_ref[...] + _dot(a1_ref[...], wa_ref[...]) + _dot(b1_ref[...], wb_ref[...])

    @pl.when(i >= n_first)
    def _():
        x_ref = x2_ref if x_split else x1_ref
        o_ref[...] = x_ref[...] + _dot(a2_ref[...], wa_ref[...]) + _dot(b2_ref[...], wb_ref[...])


def out_proj(x, a, b, w, tn=1024):
    n = w.shape[1]
    kh = a[0].shape[1]
    a1, a2, m, tm, n_first = _as_groups(a)
    b1, b2 = b
    x1, x2, _, _, x_first = _as_groups(x, tm)
    assert x_first in (n_first, m // tm)
    return pl.pallas_call(
        functools.partial(_out_proj_kernel, n_first=n_first, x_split=x_first == n_first),
        grid=(m // tm, n // tn),
        in_specs=[
            *_group_col_specs((tm, tn), x_first),
            *_group_specs((tm, kh), n_first),
            *_group_specs((tm, kh), n_first),
            pl.BlockSpec((kh, tn), lambda i, j: (0, j)),
            pl.BlockSpec((kh, tn), lambda i, j: (1, j)),
        ],
        out_specs=pl.BlockSpec((tm, tn), lambda i, j: (i, j)),
        out_shape=jax.ShapeDtypeStruct((m, n), F32),
        compiler_params=pltpu.CompilerParams(dimension_semantics=("parallel", "arbitrary")),
        name="out_proj",
    )(x1, x2, a1, a2, b1, b2, w, w)


def _ffn_kernel(x_ref, g_ref, w1_ref, w3_ref, w2_ref, o_ref, h_ref):
    @pl.when(pl.program_id(1) == 0)
    def _():
        x = x_ref[...]
        h_ref[...] = _rms(x, g_ref[...]).astype(BF16)
        o_ref[...] = x

    h = h_ref[...]
    a = _dot(h, w1_ref[...])
    u = _dot(h, w3_ref[...])
    o_ref[...] += _dot((a * _sigmoid(a) * u).astype(BF16), w2_ref[...])


FFN_VMEM_LIMIT = 56 * 1024 * 1024


def ffn(x, g, w1, w3, w2, row0, rows, tm, tf=512):
    d = x.shape[1]
    ff = w1.shape[1]
    assert row0 % tm == 0 and rows % tm == 0
    blk0 = row0 // tm
    return pl.pallas_call(
        _ffn_kernel,
        grid=(rows // tm, ff // tf),
        in_specs=[
            pl.BlockSpec((tm, d), lambda i, f: (blk0 + i, 0), pipeline_mode=pl.Buffered(1)),
            pl.BlockSpec((1, d), lambda i, f: (0, 0)),
            pl.BlockSpec((d, tf), lambda i, f: (0, f)),
            pl.BlockSpec((d, tf), lambda i, f: (0, f)),
            pl.BlockSpec((tf, d), lambda i, f: (f, 0)),
        ],
        out_specs=pl.BlockSpec((tm, d), lambda i, f: (i, 0)),
        out_shape=jax.ShapeDtypeStruct((rows, d), F32),
        scratch_shapes=[pltpu.VMEM((tm, d), BF16)],
        compiler_params=pltpu.CompilerParams(dimension_semantics=("parallel", "arbitrary"),
                                             vmem_limit_bytes=FFN_VMEM_LIMIT),
        name="ffn",
    )(x, g.reshape(1, d), w1, w3, w2)


def _router_kernel(x_ref, g_ref, r_ref, eid_ref, p_ref, rank_ref, cnt_ref, run_ref):
    i = pl.program_id(0)

    @pl.when(i == 0)
    def _():
        run_ref[...] = jnp.zeros_like(run_ref)

    h = _rms(x_ref[...], g_ref[...])
    tm, d = h.shape
    logits = _dot3(h, r_ref[...])
    n_e = logits.shape[1]
    lane = lax.broadcasted_iota(jnp.int32, logits.shape, 1).astype(F32)
    m1 = jnp.max(logits, axis=1, keepdims=True)
    i1 = jnp.min(jnp.where(logits == m1, lane, float(n_e)), axis=1, keepdims=True)
    rest = jnp.where(lane == i1, -jnp.inf, logits)
    m2 = jnp.max(rest, axis=1, keepdims=True)
    i2 = jnp.min(jnp.where(rest == m2, lane, float(n_e)), axis=1, keepdims=True)
    e2 = jnp.exp(m2 - m1)
    p1 = 1.0 / (1.0 + e2)
    slot = lax.broadcasted_iota(jnp.int32, (tm, TOP_K), 1)
    eid_ref[...] = jnp.where(slot == 0, i1, i2).astype(jnp.int32)
    p_ref[...] = jnp.where(slot == 0, p1, e2 * p1)

    hit1 = lane == i1
    hit2 = lane == i2
    both = jnp.where(hit1 | hit2, 1.0, 0.0)
    earlier = (lax.broadcasted_iota(jnp.int32, (tm, tm), 1) < lax.broadcasted_iota(jnp.int32, (tm, tm), 0))
    before = _dot(jnp.where(earlier, 1.0, 0.0).astype(BF16), both.astype(BF16)) + run_ref[...]
    r1 = jnp.sum(jnp.where(hit1, before, 0.0), axis=1, keepdims=True)
    r2 = jnp.sum(jnp.where(hit2, before, 0.0), axis=1, keepdims=True)
    rank_ref[...] = jnp.where(slot == 0, r1, r2).astype(jnp.int32)
    run_ref[...] += jnp.sum(both, axis=0, keepdims=True)
    cnt_ref[...] = run_ref[...].astype(jnp.int32)


def moe_route(x, g, router):
    m, d = x.shape
    n_e = router.shape[1]
    tm = _row_tile(m)
    pair = pl.BlockSpec((tm, TOP_K), lambda i: (i, 0))
    return pl.pallas_call(
        _router_kernel,
        grid=(m // tm,),
        in_specs=[
            pl.BlockSpec((tm, d), lambda i: (i, 0)),
            pl.BlockSpec((1, d), lambda i: (0, 0)),
            pl.BlockSpec((d, n_e), lambda i: (0, 0)),
        ],
        out_specs=[pair, pair, pair, pl.BlockSpec((1, n_e), lambda i: (0, 0))],
        out_shape=[
            jax.ShapeDtypeStruct((m, TOP_K), jnp.int32),
            jax.ShapeDtypeStruct((m, TOP_K), F32),
            jax.ShapeDtypeStruct((m, TOP_K), jnp.int32),
            jax.ShapeDtypeStruct((1, n_e), jnp.int32),
        ],
        scratch_shapes=[pltpu.VMEM((1, n_e), F32)],
        compiler_params=pltpu.CompilerParams(dimension_semantics=("arbitrary",)),
        name="moe_route",
    )(x, g.reshape(1, d), router)


SCALAR_UNROLL = 8


def _row_copy(src, s, dst, t, sem):
    return pltpu.make_async_copy(src.at[pl.ds(s, 1)], dst.at[pl.ds(t, 1)], sem)


def _invert_kernel(pos_ref, src_ref):
    def clear(r, _):
        src_ref[r] = 0
        return 0

    def put(t, _):
        for k in range(TOP_K):
            src_ref[pos_ref[TOP_K * t + k]] = t
        return 0

    lax.fori_loop(0, src_ref.shape[0], clear, 0, unroll=SCALAR_UNROLL)
    lax.fori_loop(0, pos_ref.shape[0] // TOP_K, put, 0, unroll=SCALAR_UNROLL)


def moe_invert(pos, n_rows):
    assert n_rows % SCALAR_UNROLL == 0 and pos.shape[0] % SCALAR_UNROLL == 0
    return pl.pallas_call(
        _invert_kernel,
        in_specs=[pl.BlockSpec(memory_space=pltpu.SMEM)],
        out_specs=pl.BlockSpec(memory_space=pltpu.SMEM),
        out_shape=jax.ShapeDtypeStruct((n_rows,), jnp.int32),
        name="moe_invert",
    )(pos.reshape(-1))


def _gather_kernel(src_ref, nu_ref, x_ref, g_ref, hs_ref, xs_ref, sem):
    r = pl.program_id(0)
    tm = hs_ref.shape[0]
    slot = r % 2

    def issue(tile, into):
        def body(j, _):
            _row_copy(x_ref, src_ref[tile * tm + j], xs_ref.at[into], j, sem.at[into]).start()
            return 0
        lax.fori_loop(0, tm, body, 0, unroll=SCALAR_UNROLL)

    @pl.when(r == 0)
    def _():
        issue(0, 0)

    @pl.when(r + 1 < nu_ref[0])
    def _():
        issue(r + 1, 1 - slot)

    @pl.when(r < nu_ref[0])
    def _():
        def drain(j, _):
            _row_copy(x_ref, 0, xs_ref.at[slot], 0, sem.at[slot]).wait()
            return 0

        lax.fori_loop(0, tm, drain, 0, unroll=SCALAR_UNROLL)
        hs_ref[...] = _rms(xs_ref[slot], g_ref[...]).astype(BF16)

    @pl.when(r >= nu_ref[0])
    def _():
        hs_ref[...] = jnp.zeros_like(hs_ref)


def moe_gather(src, n_used, x, g, tm):
    n_rows = src.shape[0]
    d = x.shape[1]
    return pl.pallas_call(
        _gather_kernel,
        grid_spec=pltpu.PrefetchScalarGridSpec(
            num_scalar_prefetch=2,
            grid=(n_rows // tm,),
            in_specs=[pl.BlockSpec(memory_space=pl.ANY), pl.BlockSpec((1, d), lambda r, src, nu: (0, 0))],
            out_specs=pl.BlockSpec((tm, d), lambda r, src, nu: (r, 0)),
            scratch_shapes=[pltpu.VMEM((2, tm, d), F32), pltpu.SemaphoreType.DMA((2,))],
        ),
        out_shape=jax.ShapeDtypeStruct((n_rows, d), BF16),
        compiler_params=pltpu.CompilerParams(dimension_semantics=("arbitrary",)),
        name="moe_gather",
    )(src, n_used, x, g.reshape(1, d))


def _expert_rows(first_ref, count_ref, src_ref, dst_ref, ibuf_ref, obuf_ref, sem_in, sem_out, compute):
    e = pl.program_id(0)
    c = pl.program_id(1)
    tm = ibuf_ref.shape[1]
    width = obuf_ref.shape[2]
    first = first_ref[e]
    n_real = count_ref[e]
    n_all = jnp.where(e == pl.num_programs(0) - 1, dst_ref.shape[0] // tm - first, n_real)
    cols = pl.ds(pl.multiple_of(c * width, width), width)

    def rows(k, base=first):
        return pl.ds(pl.multiple_of((base + k) * tm, tm), tm)

    def load(k, slot, base=first):
        return pltpu.make_async_copy(src_ref.at[rows(k, base)], ibuf_ref.at[slot], sem_in.at[slot])

    def store(k, slot):
        return pltpu.make_async_copy(obuf_ref.at[slot], dst_ref.at[rows(k), cols], sem_out.at[slot])

    @pl.when((e == 0) & (c == 0) & (n_real > 0))
    def _():
        load(0, 0).start()

    def body(k, _):
        slot = k % 2

        @pl.when(k >= 2)
        def _():
            store(k - 2, slot).wait()

        @pl.when(k < n_real)
        def _():
            load(k, slot).wait()

            @pl.when(k + 1 < n_real)
            def _():
                load(k + 1, 1 - slot).start()

            obuf_ref[slot] = compute(ibuf_ref[slot])

        @pl.when(k >= n_real)
        def _():
            obuf_ref[slot] = jnp.zeros(obuf_ref.shape[1:], obuf_ref.dtype)

        store(k, slot).start()
        return 0

    lax.fori_loop(0, n_all, body, 0)

    last_c = c == pl.num_programs(1) - 1
    e_next = jnp.where(last_c, jnp.minimum(e + 1, pl.num_programs(0) - 1), e)
    more = jnp.logical_not(last_c & (e == pl.num_programs(0) - 1))

    @pl.when(more & (count_ref[e_next] > 0))
    def _():
        load(0, 0, first_ref[e_next]).start()

    for back in (2, 1):
        @pl.when(n_all >= back)
        def _():
            store(n_all - back, (n_all - back) % 2).wait()


def _moe_up_kernel(first_ref, count_ref, hs_ref, w1_ref, w3_ref, act_ref, w1b_ref, w3b_ref, ibuf_ref, obuf_ref,
                   sem_in, sem_out):
    w1b_ref[...] = w1_ref[...].astype(BF16)
    w3b_ref[...] = w3_ref[...].astype(BF16)

    def compute(h):
        a = _dot(h, w1b_ref[...])
        u = _dot(h, w3b_ref[...])
        return (a * _sigmoid(a) * u).astype(BF16)

    _expert_rows(first_ref, count_ref, hs_ref, act_ref, ibuf_ref, obuf_ref, sem_in, sem_out, compute)


def _moe_down_kernel(first_ref, count_ref, act_ref, w2_ref, y_ref, w2b_ref, ibuf_ref, obuf_ref, sem_in, sem_out):
    w2b_ref[...] = w2_ref[...].astype(BF16)
    _expert_rows(first_ref, count_ref, act_ref, y_ref, ibuf_ref, obuf_ref, sem_in, sem_out,
                 lambda a: _dot(a, w2b_ref[...]))


def moe_experts(tile_first, tile_count, hs, w1, w3, w2, tm, tf=512, tn=512):
    n_rows, d = hs.shape
    n_e, _, ff = w1.shape
    any_spec = pl.BlockSpec(memory_space=pl.ANY)
    sems = [pltpu.SemaphoreType.DMA((2,)), pltpu.SemaphoreType.DMA((2,))]
    act = pl.pallas_call(
        _moe_up_kernel,
        grid_spec=pltpu.PrefetchScalarGridSpec(
            num_scalar_prefetch=2,
            grid=(n_e, ff // tf),
            in_specs=[
                any_spec,
                pl.BlockSpec((None, d, tf), lambda e, f, *_: (e, 0, f)),
                pl.BlockSpec((None, d, tf), lambda e, f, *_: (e, 0, f)),
            ],
            out_specs=any_spec,
            scratch_shapes=[pltpu.VMEM((d, tf), BF16), pltpu.VMEM((d, tf), BF16),
                            pltpu.VMEM((2, tm, d), BF16), pltpu.VMEM((2, tm, tf), BF16), *sems],
        ),
        out_shape=jax.ShapeDtypeStruct((n_rows, ff), BF16),
        compiler_params=pltpu.CompilerParams(dimension_semantics=("arbitrary", "arbitrary"),
                                             vmem_limit_bytes=48 * 1024 * 1024),
        name="moe_up",
    )(tile_first, tile_count, hs, w1, w3)
    return pl.pallas_call(
        _moe_down_kernel,
        grid_spec=pltpu.PrefetchScalarGridSpec(
            num_scalar_prefetch=2,
            grid=(n_e, d // tn),
            in_specs=[any_spec, pl.BlockSpec((None, ff, tn), lambda e, n, *_: (e, 0, n))],
            out_specs=any_spec,
            scratch_shapes=[pltpu.VMEM((ff, tn), BF16),
                            pltpu.VMEM((2, tm, ff), BF16), pltpu.VMEM((2, tm, tn), F32), *sems],
        ),
        out_shape=jax.ShapeDtypeStruct((n_rows, d), F32),
        compiler_params=pltpu.CompilerParams(dimension_semantics=("arbitrary", "arbitrary"),
                                             vmem_limit_bytes=56 * 1024 * 1024),
        name="moe_down",
    )(tile_first, tile_count, act, w2)


def _combine_kernel(pos_ref, x_ref, p_ref, y_ref, o1_ref, o2_ref, ya_ref, yb_ref, sem, *, n_first):
    tm = x_ref.shape[0]
    i = pl.program_id(0)
    slot = i % 2

    def issue(tile, into):
        def body(j, _):
            t = tile * tm + j
            _row_copy(y_ref, pos_ref[TOP_K * t], ya_ref.at[into], j, sem.at[into]).start()
            _row_copy(y_ref, pos_ref[TOP_K * t + 1], yb_ref.at[into], j, sem.at[into]).start()
            return 0
        lax.fori_loop(0, tm, body, 0, unroll=SCALAR_UNROLL)

    def drain(j, _):
        _row_copy(y_ref, 0, ya_ref.at[slot], 0, sem.at[slot]).wait()
        _row_copy(y_ref, 0, yb_ref.at[slot], 0, sem.at[slot]).wait()
        return 0

    @pl.when(i == 0)
    def _():
        issue(0, 0)

    @pl.when(i + 1 < pl.num_programs(0))
    def _():
        issue(i + 1, 1 - slot)

    lax.fori_loop(0, tm, drain, 0, unroll=SCALAR_UNROLL)
    p = p_ref[...]
    out = x_ref[...] + p[:, 0:1] * ya_ref[slot] + p[:, 1:2] * yb_ref[slot]

    @pl.when(i < n_first)
    def _():
        o1_ref[...] = out

    @pl.when(i >= n_first)
    def _():
        o2_ref[...] = out


def moe_combine(pos, x, p, y, m_first, tm=128):
    m, d = x.shape
    assert m_first % tm == 0 and m % tm == 0
    n_first = m_first // tm
    return pl.pallas_call(
        functools.partial(_combine_kernel, n_first=n_first),
        grid_spec=pltpu.PrefetchScalarGridSpec(
            num_scalar_prefetch=1,
            grid=(m // tm,),
            in_specs=[
                pl.BlockSpec((tm, d), lambda i, pos: (i, 0)),
                pl.BlockSpec((tm, TOP_K), lambda i, pos: (i, 0)),
                pl.BlockSpec(memory_space=pl.ANY),
            ],
            out_specs=list(_group_specs((tm, d), n_first)),
            scratch_shapes=[pltpu.VMEM((2, tm, d), F32), pltpu.VMEM((2, tm, d), F32),
                            pltpu.SemaphoreType.DMA((2,))],
        ),
        out_shape=[jax.ShapeDtypeStruct((m_first, d), F32), jax.ShapeDtypeStruct((m - m_first, d), F32)],
        compiler_params=pltpu.CompilerParams(dimension_semantics=("arbitrary",)),
        name="moe_combine",
    )(pos.reshape(-1), x, p, y)


def moe_layer(x, g, router, w1, w3, w2, m_first, tm=512):
    m = x.shape[0]
    n_e = router.shape[1]
    eid, p, rank, counts = moe_route(x, g, router)
    padded = (counts[0] + tm - 1) // tm * tm
    ends = jnp.cumsum(padded)
    starts = ends - padded
    experts = jnp.arange(n_e, dtype=jnp.int32)
    pos = rank + jnp.sum(jnp.where(eid[:, :, None] == experts, starts, 0), axis=-1)
    n_tiles = -(-TOP_K * m // tm) + n_e
    n_used = (ends[-1:] // tm).astype(jnp.int32)
    hs = moe_gather(moe_invert(pos, n_tiles * tm), n_used, x, g, tm)
    y = moe_experts((starts // tm).astype(jnp.int32), (padded // tm).astype(jnp.int32), hs, w1, w3, w2, tm)
    return moe_combine(pos, x, p, y, m_first)


def _ret_kernel(q_ref, k_ref, v_ref, g_ref, cos_ref, sin_ref, gn_ref, s0_ref, o_ref, s_ref, st_ref,
                *, has_state):
    n = pl.program_id(1)
    c = q_ref.shape[0]
    dk = q_ref.shape[1] // RET_HEADS
    half = dk // 2

    @pl.when(n == 0)
    def _():
        if has_state:
            st_ref[...] = s0_ref[...]
        else:
            st_ref[...] = jnp.zeros_like(st_ref)

    cos = cos_ref[...]
    sin = sin_ref[...]
    row = lax.broadcasted_iota(jnp.int32, (c, c), 0)
    col = lax.broadcasted_iota(jnp.int32, (c, c), 1)
    diff = (row - col).astype(F32)
    ridx = lax.broadcasted_iota(jnp.int32, (c, 1), 0).astype(F32)

    def rot(x):
        x1, x2 = x[:, :half], x[:, half:]
        return jnp.concatenate([x1 * cos - x2 * sin, x1 * sin + x2 * cos], axis=1)

    for h in range(RET_HEADS):
        log_g = math.log1p(-(2.0 ** (-5.0 - h)))
        sl = slice(h * dk, (h + 1) * dk)
        qr = rot(q_ref[:, sl]) * (dk ** -0.5)
        kr = rot(k_ref[:, sl])
        vh = v_ref[:, sl].astype(BF16)
        decay = jnp.where(diff >= 0, jnp.exp(log_g * jnp.maximum(diff, 0.0)), 0.0)
        scores = _dot_nt(qr.astype(BF16), kr.astype(BF16)) * decay
        s = st_ref[h]
        q_dec = jnp.exp(log_g * (ridx + 1.0))
        k_dec = jnp.exp(log_g * (c - 1.0 - ridx))
        o = _dot(scores.astype(BF16), vh) + _dot((qr * q_dec).astype(BF16), s.astype(BF16))
        st_ref[h] = s * math.exp(log_g * c) + _dot_tn((kr * k_dec).astype(BF16), vh)
        gh = g_ref[:, sl]
        o_ref[:, sl] = (gh * _sigmoid(gh) * _rms(o, gn_ref[h:h + 1, :])).astype(BF16)

    @pl.when(n == pl.num_programs(1) - 1)
    def _():
        s_ref[...] = st_ref[...]


def retention_mixer(qkv, row0, batch, seq, cos, sin, gn, s0):
    half_d = qkv.shape[1] // 7
    dk = half_d // RET_HEADS
    c = next((t for t in (2 * CHUNK, CHUNK) if seq % t == 0 and row0 % t == 0), seq)
    nc = seq // c
    blk0 = row0 // c
    has_state = s0 is not None
    if not has_state:
        s0 = jnp.zeros((1, RET_HEADS, dk, dk), F32)

    def col(j):
        return pl.BlockSpec((c, half_d), lambda b, n: (blk0 + b * nc + n, j))

    return pl.pallas_call(
        functools.partial(_ret_kernel, has_state=has_state),
        grid=(batch, nc),
        in_specs=[
            col(0), col(1), col(2), col(3),
            pl.BlockSpec((c, dk // 2), lambda b, n: (n, 0)),
            pl.BlockSpec((c, dk // 2), lambda b, n: (n, 0)),
            pl.BlockSpec((RET_HEADS, dk), lambda b, n: (0, 0)),
            pl.BlockSpec((None, RET_HEADS, dk, dk), lambda b, n: (b if has_state else 0, 0, 0, 0)),
        ],
        out_specs=[
            pl.BlockSpec((c, half_d), lambda b, n: (b * nc + n, 0)),
            pl.BlockSpec((None, RET_HEADS, dk, dk), lambda b, n: (b, 0, 0, 0)),
        ],
        out_shape=[
            jax.ShapeDtypeStruct((batch * seq, half_d), BF16),
            jax.ShapeDtypeStruct((batch, RET_HEADS, dk, dk), F32),
        ],
        scratch_shapes=[pltpu.VMEM((RET_HEADS, dk, dk), F32)],
        compiler_params=pltpu.CompilerParams(dimension_semantics=("parallel", "arbitrary")),
        name="retention",
    )(qkv, qkv, qkv, qkv, cos, sin, gn, s0)


def _sb_norm_kernel(q_ref, k_ref, v_ref, qg_ref, kg_ref, qn_ref, knb_ref, vb_ref, k1_ref, k2_ref, v1_ref, v2_ref,
                    *, n_first):
    dh = qg_ref.shape[1]
    first = pl.program_id(0) < n_first
    for h in range(q_ref.shape[1] // dh):
        sl = slice(h * dh, (h + 1) * dh)
        qn_ref[:, sl] = (_rms(q_ref[:, sl], qg_ref[...]) * (dh ** -0.5)).astype(BF16)
        kn = _rms(k_ref[:, sl], kg_ref[...])
        knb_ref[:, sl] = kn.astype(BF16)

        @pl.when(first)
        def _():
            k1_ref[:, sl] = kn

        @pl.when(jnp.logical_not(first))
        def _():
            k2_ref[:, sl] = kn

    v = v_ref[...]
    vb_ref[...] = v.astype(BF16)

    @pl.when(first)
    def _():
        v1_ref[...] = v

    @pl.when(jnp.logical_not(first))
    def _():
        v2_ref[...] = v


def sb_norm(qkv, qg, kg, m_first):
    m = qkv.shape[0]
    half_d = qkv.shape[1] // 7
    dh = qg.shape[0]
    tm = _row_tile(math.gcd(m_first, m - m_first))
    n_first = m_first // tm

    def col(j):
        return pl.BlockSpec((tm, half_d), lambda i: (i, j))

    out = pl.BlockSpec((tm, half_d), lambda i: (i, 0))
    gspec = pl.BlockSpec((1, dh), lambda i: (0, 0))
    split = _group_specs((tm, half_d), n_first)
    f32_first = jax.ShapeDtypeStruct((m_first, half_d), F32)
    f32_rest = jax.ShapeDtypeStruct((m - m_first, half_d), F32)
    return pl.pallas_call(
        functools.partial(_sb_norm_kernel, n_first=n_first),
        grid=(m // tm,),
        in_specs=[col(4), col(5), col(6), gspec, gspec],
        out_specs=[out] * 3 + [*split, *split],
        out_shape=[jax.ShapeDtypeStruct((m, half_d), BF16)] * 3 + [f32_first, f32_rest, f32_first, f32_rest],
        compiler_params=pltpu.CompilerParams(dimension_semantics=("arbitrary",)),
        name="sb_norm",
    )(qkv, qkv, qkv, qg.reshape(1, dh), kg.reshape(1, dh))


SB_EXIT_LOG = 88.0


def _sb_block(q, kb, vb, carry, acc, strict_diag):
    tq, tk = q.shape[0], kb.shape[0]
    z = _dot_nt(q, kb)
    sp = _softplus(z)
    log_keep = -sp
    if strict_diag:
        mask = (lax.broadcasted_iota(jnp.int32, (tq, tk), 1) < lax.broadcasted_iota(jnp.int32, (tq, tk), 0))
        log_keep = jnp.where(mask, log_keep, 0.0)
    later = (lax.broadcasted_iota(jnp.int32, (tk, tk), 0) > lax.broadcasted_iota(jnp.int32, (tk, tk), 1))
    u = jnp.where(later, 1.0, 0.0).astype(BF16)
    hi, lo = _split(log_keep)
    after = _dot(hi, u) + _dot(lo, u) + carry
    w = jnp.exp(z - sp + after)
    if strict_diag:
        w = jnp.where(mask, w, 0.0)
    return jnp.sum(log_keep, axis=1, keepdims=True), acc + _dot(w.astype(BF16), vb)


def _sb_visit(q_ref, carry_ref, acc_ref, key_block, strict_diag):
    heads = q_ref.shape[1] // LANES
    worst = None
    for h in range(heads):
        sl = slice(h * LANES, (h + 1) * LANES)
        kb, vb = key_block(h, sl)
        tk = kb.shape[0]
        carry = carry_ref[h]
        block_sum, acc = _sb_block(q_ref[:, sl], kb, vb, carry[:, :tk], acc_ref[:, sl], strict_diag)
        carry = carry + block_sum
        carry_ref[h] = carry
        acc_ref[:, sl] = acc
        worst = carry if worst is None else jnp.maximum(worst, carry)
    return jnp.max(worst)


def _sb_prompt_kernel(q_ref, k_ref, v_ref, o_ref, acc_ref, carry_ref):
    i = pl.program_id(1)
    tq = q_ref.shape[0]
    acc_ref[...] = jnp.zeros_like(acc_ref)
    carry_ref[...] = jnp.zeros_like(carry_ref)

    def visit(j, strict_diag):
        s = pl.multiple_of(j * tq, tq)
        return _sb_visit(q_ref, carry_ref, acc_ref,
                         lambda h, sl: (k_ref[pl.ds(s, tq), sl], v_ref[pl.ds(s, tq), sl]), strict_diag)

    top = visit(i, True)
    lax.while_loop(lambda st: (st[0] >= 0) & (st[1] > -SB_EXIT_LOG),
                   lambda st: (st[0] - 1, visit(st[0], False)), (i - 1, top))
    o_ref[...] = acc_ref[...].astype(BF16)


def sb_attention_prompt(qn, knb, vb, batch, seq, tq=LANES):
    width = qn.shape[1]
    nq = seq // tq
    return pl.pallas_call(
        _sb_prompt_kernel,
        grid=(batch, nq),
        in_specs=[
            pl.BlockSpec((tq, width), lambda b, i: (b * nq + i, 0)),
            pl.BlockSpec((seq, width), lambda b, i: (b, 0)),
            pl.BlockSpec((seq, width), lambda b, i: (b, 0)),
        ],
        out_specs=pl.BlockSpec((tq, width), lambda b, i: (b * nq + i, 0)),
        out_shape=jax.ShapeDtypeStruct((batch * seq, width), BF16),
        scratch_shapes=[pltpu.VMEM((tq, width), F32), pltpu.VMEM((width // LANES, tq, LANES), F32)],
        compiler_params=pltpu.CompilerParams(dimension_semantics=("parallel", "arbitrary")),
        name="sb_attention_prompt",
    )(qn, knb, vb)


def _sb_sample_kernel(q_ref, k_ref, v_ref, kp_ref, vp_ref, o_ref, acc_ref, carry_ref, kbuf_ref, vbuf_ref, sem):
    b = pl.program_id(0)
    tk = kbuf_ref.shape[1]
    n_past = kp_ref.shape[1] // tk
    acc_ref[...] = jnp.zeros_like(acc_ref)
    carry_ref[...] = jnp.zeros_like(carry_ref)

    def fetch(j):
        slot = (n_past - 1 - j) % 2
        rows = pl.ds(pl.multiple_of(j * tk, tk), tk)
        return (pltpu.make_async_copy(kp_ref.at[b, rows], kbuf_ref.at[slot], sem.at[0, slot]),
                pltpu.make_async_copy(vp_ref.at[b, rows], vbuf_ref.at[slot], sem.at[1, slot]))

    def start(j):
        for c in fetch(j):
            c.start()

    def wait(j):
        for c in fetch(j):
            c.wait()

    start(n_past - 1)
    top = _sb_visit(q_ref, carry_ref, acc_ref, lambda h, sl: (k_ref[:, sl], v_ref[:, sl]), True)

    def body(st):
        j = st[0]
        slot = (n_past - 1 - j) % 2
        wait(j)

        @pl.when(j >= 1)
        def _():
            start(j - 1)

        top = _sb_visit(q_ref, carry_ref, acc_ref,
                        lambda h, sl: (kbuf_ref[slot, :, h, :].astype(BF16), vbuf_ref[slot, :, h, :].astype(BF16)),
                        False)
        return j - 1, top

    j_end, _ = lax.while_loop(lambda st: (st[0] >= 0) & (st[1] > -SB_EXIT_LOG), body, (n_past - 1, top))

    @pl.when(j_end >= 0)
    def _():
        wait(j_end)

    o_ref[...] = acc_ref[...].astype(BF16)


def sb_attention_sample(qn, knb, vb, k_past, v_past, row0, batch, seq, tk=LANES):
    width = qn.shape[1]
    heads = width // LANES
    blk0 = row0 // seq
    assert k_past.shape[1] % tk == 0 and k_past.shape[2:] == (heads, LANES)
    new = pl.BlockSpec((seq, width), lambda b: (blk0 + b, 0))
    return pl.pallas_call(
        _sb_sample_kernel,
        grid=(batch,),
        in_specs=[new, new, new, pl.BlockSpec(memory_space=pl.ANY), pl.BlockSpec(memory_space=pl.ANY)],
        out_specs=pl.BlockSpec((seq, width), lambda b: (b, 0)),
        out_shape=jax.ShapeDtypeStruct((batch * seq, width), BF16),
        scratch_shapes=[
            pltpu.VMEM((seq, width), F32), pltpu.VMEM((heads, seq, LANES), F32),
            pltpu.VMEM((2, tk, heads, LANES), F32), pltpu.VMEM((2, tk, heads, LANES), F32),
            pltpu.SemaphoreType.DMA((2, 2)),
        ],
        compiler_params=pltpu.CompilerParams(dimension_semantics=("arbitrary",)),
        name="sb_attention_sample",
    )(qn, knb, vb, k_past, v_past)


LRU_SCAN_ROWS = 128


def _lru_kernel(gc_ref, xc_ref, buf_ref, h0_ref, cw_ref, cb_ref, wr_ref, br_ref, wi_ref, bi_ref, lam_ref,
                o_ref, conv_ref, hl_ref, *, has_state):
    seq, bw = xc_ref.shape
    x = xc_ref[...]
    row = lax.broadcasted_iota(jnp.int32, (seq, bw), 0)
    cw = cw_ref[...]
    y = cb_ref[...] + cw[CONV_W - 1:CONV_W, :] * x
    for d in range(1, CONV_W):
        xs = pltpu.roll(x, d, 0)
        for r in range(d):
            prev = buf_ref[CONV_W - 1 - d + r:CONV_W - d + r, :] if has_state else jnp.zeros((1, bw), F32)
            xs = jnp.where(row == r, prev, xs)
        y = y + cw[CONV_W - 1 - d:CONV_W - d, :] * xs
    conv_ref[...] = x[seq - (CONV_W - 1):, :]

    r_gate = _sigmoid(_dot3(y, wr_ref[...]) + br_ref[...])
    i_gate = _sigmoid(_dot3(y, wi_ref[...]) + bi_ref[...])
    log_a = -LRU_C * r_gate * _softplus(-lam_ref[...])
    a = jnp.exp(log_a)
    u = jnp.sqrt(-jnp.tanh(log_a) * (a * a + 1.0)) * (i_gate * y)
    if has_state:
        u = jnp.where(row == 0, u + a * h0_ref[...], u)

    blk = math.gcd(seq, LRU_SCAN_ROWS)
    row_b = lax.broadcasted_iota(jnp.int32, (blk, bw), 0)
    h_prev = None
    for r0 in range(0, seq, blk):
        ab = a[r0:r0 + blk, :]
        ub = u[r0:r0 + blk, :]
        shift = 1
        while shift < blk:
            a_s = pltpu.roll(ab, shift, 0)
            u_s = pltpu.roll(ub, shift, 0)
            live = row_b >= shift
            ub = jnp.where(live, ab * u_s + ub, ub)
            ab = jnp.where(live, ab * a_s, ab)
            shift *= 2
        if h_prev is not None:
            ub = ub + ab * h_prev
        h_prev = ub[blk - 1:, :]
        gc = gc_ref[r0:r0 + blk, :]
        gelu = 0.5 * gc * (1.0 + jnp.tanh(math.sqrt(2.0 / math.pi) * (gc + 0.044715 * gc * gc * gc)))
        o_ref[r0:r0 + blk, :] = (gelu * ub).astype(BF16)
    hl_ref[...] = h_prev


def lru_mixer(proj, row0, batch, seq, conv_buf, h0, cw, cb, wr, br, wi, bi, lam):
    width = proj.shape[1] // 5
    bw = width // LRU_BLOCKS
    blk0 = row0 // seq
    has_state = conv_buf is not None
    if not has_state:
        conv_buf = jnp.zeros((1, CONV_W - 1, width), F32)
        h0 = jnp.zeros((1, width), F32)
    h0 = h0.reshape(-1, 1, width)

    def vec(k):
        return pl.BlockSpec((k, bw), lambda b, n: (0, n))

    wspec = pl.BlockSpec((None, bw, bw), lambda b, n: (n, 0, 0))
    return pl.pallas_call(
        functools.partial(_lru_kernel, has_state=has_state),
        grid=(batch, LRU_BLOCKS),
        in_specs=[
            pl.BlockSpec((seq, bw), lambda b, n: (blk0 + b, n)),
            pl.BlockSpec((seq, bw), lambda b, n: (blk0 + b, LRU_BLOCKS + n)),
            pl.BlockSpec((None, CONV_W - 1, bw), lambda b, n: (b if has_state else 0, 0, n)),
            pl.BlockSpec((None, 1, bw), lambda b, n: (b if has_state else 0, 0, n)),
            vec(CONV_W), vec(1), wspec, vec(1), wspec, vec(1), vec(1),
        ],
        out_specs=[
            pl.BlockSpec((seq, bw), lambda b, n: (b, n)),
            pl.BlockSpec((None, CONV_W - 1, bw), lambda b, n: (b, 0, n)),
            pl.BlockSpec((None, 1, bw), lambda b, n: (b, 0, n)),
        ],
        out_shape=[
            jax.ShapeDtypeStruct((batch * seq, width), BF16),
            jax.ShapeDtypeStruct((batch, CONV_W - 1, width), F32),
            jax.ShapeDtypeStruct((batch, 1, width), F32),
        ],
        compiler_params=pltpu.CompilerParams(dimension_semantics=("parallel", "parallel")),
        name="rg_lru",
    )(proj, proj, conv_buf, h0, cw, cb.reshape(1, width), wr, br.reshape(1, width), wi,
      bi.reshape(1, width), lam.reshape(1, width))


def _cb_kernel(q_ref, k_ref, v_ref, kp_ref, vp_ref, qg_ref, kg_ref, bias_ref, o_ref, ko_ref, vo_ref,
               kb_ref, vb_ref, *, has_past):
    seq, dh = q_ref.shape
    keep = ko_ref.shape[0]
    kn = _rms(k_ref[...], kg_ref[...])
    v = v_ref[...]
    ko_ref[...] = kn[seq - keep:, :]
    vo_ref[...] = v[seq - keep:, :]
    if has_past:
        kb_ref[:CB_PAST, :] = kp_ref[...].astype(BF16)
        vb_ref[:CB_PAST, :] = vp_ref[...].astype(BF16)
    else:
        kb_ref[:CB_PAST, :] = jnp.zeros((CB_PAST, dh), BF16)
        vb_ref[:CB_PAST, :] = jnp.zeros((CB_PAST, dh), BF16)
    kb_ref[CB_PAST:, :] = kn.astype(BF16)
    vb_ref[CB_PAST:, :] = v.astype(BF16)
    tq, span = bias_ref.shape
    bias = bias_ref[...]
    kidx = lax.broadcasted_iota(jnp.int32, (tq, span), 1)

    def body(n, _):
        r0 = pl.multiple_of(n * tq, tq)
        qn = (_rms(q_ref[pl.ds(r0, tq), :], qg_ref[...]) * (dh ** -0.5)).astype(BF16)
        s = _dot_nt(qn, kb_ref[pl.ds(r0, span), :]) + bias
        if not has_past:
            s = jnp.where(kidx + r0 >= CB_PAST, s, NEG_BIG)
        p = jnp.exp(s - jnp.max(s, axis=1, keepdims=True))
        o = _dot(p.astype(BF16), vb_ref[pl.ds(r0, span), :]) / jnp.sum(p, axis=1, keepdims=True)
        o_ref[pl.ds(r0, tq), :] = o.astype(BF16)
        return 0

    steps = seq // tq
    lax.fori_loop(0, steps, body, 0, unroll=min(steps, 2))


def _block_bias(bias, g):
    rows = [jnp.pad(bias, ((0, 0), (0, 0), (c * CHUNK, (g - 1 - c) * CHUNK)), constant_values=NEG_BIG)
            for c in range(g)]
    return jnp.concatenate(rows, axis=1)


def cb_attention(proj, row0, batch, seq, keep, k_past, v_past, qg, kg, bias):
    assert seq % CHUNK == 0
    bias = _block_bias(bias, 2 if (seq // CHUNK) % 2 == 0 else 1)
    half_d = proj.shape[1] // 5
    heads = half_d // LANES
    blk0 = row0 // seq
    has_past = k_past is not None
    if not has_past:
        k_past = jnp.zeros((1, CB_PAST, half_d), F32)
        v_past = k_past

    def col(j):
        return pl.BlockSpec((seq, LANES), lambda b, h: (blk0 + b, j * heads + h))

    old = pl.BlockSpec((None, CB_PAST, LANES), lambda b, h: (b if has_past else 0, 0, h))
    gspec = pl.BlockSpec((1, LANES), lambda b, h: (0, 0))
    kept = pl.BlockSpec((None, keep, LANES), lambda b, h: (b, 0, h))
    return pl.pallas_call(
        functools.partial(_cb_kernel, has_past=has_past),
        grid=(batch, heads),
        in_specs=[col(2), col(3), col(4), old, old, gspec, gspec,
                  pl.BlockSpec((None,) + bias.shape[1:], lambda b, h: (h, 0, 0))],
        out_specs=[pl.BlockSpec((seq, LANES), lambda b, h: (b, h)), kept, kept],
        out_shape=[
            jax.ShapeDtypeStruct((batch * seq, half_d), BF16),
            jax.ShapeDtypeStruct((batch, keep, half_d), F32),
            jax.ShapeDtypeStruct((batch, keep, half_d), F32),
        ],
        scratch_shapes=[pltpu.VMEM((CB_PAST + seq, LANES), BF16), pltpu.VMEM((CB_PAST + seq, LANES), BF16)],
        compiler_params=pltpu.CompilerParams(dimension_semantics=("parallel", "parallel")),
        name="cb_attention",
    )(proj, proj, proj, k_past, v_past, qg.reshape(1, LANES), kg.reshape(1, LANES), bias)


def _rope_tables(pos0, seq, half):
    inv = ROPE_BASE ** (-jnp.arange(half, dtype=F32) / half)
    ang = (pos0 + jnp.arange(seq)).astype(F32)[:, None] * inv[None, :]
    return jnp.cos(ang), jnp.sin(ang)


def _band_bias(table):
    lo = -(CHUNK - 1)
    n_flat = CB_PAST + CHUNK - 1 - CB_MAX_REL
    ext = jnp.concatenate([table[:, lo + CB_MAX_REL:], jnp.repeat(table[:, -1:], n_flat, axis=1)], axis=1)
    rev = ext[:, ::-1]
    n = rev.shape[1]
    skew = jnp.tile(rev, (1, CHUNK + 1))[:, :CHUNK * (n + 1)].reshape(-1, CHUNK, n + 1)
    return skew[:, ::-1, :CB_BAND]


def kernel(x_prompt, x_sample, state_ret, cache_sb_k, cache_sb_v, state_conv, state_lru, cache_cb_k, cache_cb_v, e_norm_mix, e_w_in, e_ret_gn, e_sb_qn, e_sb_kn, e_w_out, e_norm_ffn, e_w1, e_w3, e_w2, o_norm_mix, o_w_in, o_conv_w, o_conv_b, o_lru_wr, o_lru_br, o_lru_wi, o_lru_bi, o_lru_lam, o_cb_qn, o_cb_kn, o_cb_bias, o_w_out, o_norm_ffn, o_router, o_w1, o_w3, o_w2):
    bp, lp, d = x_prompt.shape
    bs, ls, _ = x_sample.shape
    half_d = d // 2
    tp = bp * lp
    past = cache_sb_k.shape[2]
    dk = half_d // RET_HEADS
    x = (x_prompt.reshape(tp, d), x_sample.reshape(bs * ls, d))

    qkv = norm_matmul(x, e_norm_mix[0], e_w_in[0].astype(BF16))
    cos_p, sin_p = _rope_tables(0, lp, dk // 2)
    cos_s, sin_s = _rope_tables(past, ls, dk // 2)
    a_p, p_ret = retention_mixer(qkv, 0, bp, lp, cos_p, sin_p, e_ret_gn[0], None)
    a_s, s_ret = retention_mixer(qkv, tp, bs, ls, cos_s, sin_s, e_ret_gn[0], state_ret[0])
    qn, knb, vb, p_sb_k, s_sb_k, p_sb_v, s_sb_v = sb_norm(qkv, e_sb_qn[0], e_sb_kn[0], tp)
    b_p = sb_attention_prompt(qn, knb, vb, bp, lp)
    b_s = sb_attention_sample(qn, knb, vb, cache_sb_k[0], cache_sb_v[0], tp, bs, ls)
    x = out_proj(x, (a_p, a_s), (b_p, b_s), e_w_out[0].astype(BF16))
    ffn_w = (e_norm_ffn[0], e_w1[0].astype(BF16), e_w3[0].astype(BF16), e_w2[0].astype(BF16))
    x = ffn(x, *ffn_w, 0, x.shape[0], _row_tile(x.shape[0]))

    proj = norm_matmul(x, o_norm_mix[0], o_w_in[0].astype(BF16))
    lru_w = (o_conv_w[0], o_conv_b[0], o_lru_wr[0], o_lru_br[0], o_lru_wi[0], o_lru_bi[0], o_lru_lam[0])
    c_p, p_conv, p_lru = lru_mixer(proj, 0, bp, lp, None, None, *lru_w)
    c_s, s_conv, s_lru = lru_mixer(proj, tp, bs, ls, state_conv[0], state_lru[0], *lru_w)
    bias = _band_bias(o_cb_bias[0])
    keep_p = min(CB_PAST, lp)
    d_p, p_cb_k, p_cb_v = cb_attention(proj, 0, bp, lp, keep_p, None, None, o_cb_qn[0], o_cb_kn[0], bias)
    d_s, s_cb_k, s_cb_v = cb_attention(proj, tp, bs, ls, ls, cache_cb_k[0].reshape(bs, CB_PAST, half_d),
                                       cache_cb_v[0].reshape(bs, CB_PAST, half_d), o_cb_qn[0], o_cb_kn[0], bias)
    x = out_proj(x, (c_p, c_s), (d_p, d_s), o_w_out[0].astype(BF16))
    y_p, y_s = moe_layer(x, o_norm_ffn[0], o_router[0], o_w1[0], o_w3[0], o_w2[0], tp)

    sbh = (SB_HEADS, half_d // SB_HEADS)
    cbh = (CB_HEADS, half_d // CB_HEADS)
    return (
        y_p.reshape(bp, lp, d), y_s.reshape(bs, ls, d),
        p_ret[None], p_sb_k.reshape(1, bp, lp, *sbh), p_sb_v.reshape(1, bp, lp, *sbh),
        p_conv[None], p_lru.reshape(1, bp, half_d),
        p_cb_k.reshape(1, bp, keep_p, *cbh), p_cb_v.reshape(1, bp, keep_p, *cbh),
        s_ret[None], s_sb_k.reshape(1, bs, ls, *sbh), s_sb_v.reshape(1, bs, ls, *sbh),
        s_conv[None], s_lru.reshape(1, bs, half_d),
        s_cb_k.reshape(1, bs, ls, *cbh), s_cb_v.reshape(1, bs, ls, *cbh),
    )
```

```python
import functools
import math

import jax
import jax.numpy as jnp
from jax import lax
from jax.experimental import pallas as pl
from jax.experimental.pallas import tpu as pltpu

F32 = jnp.float32
BF16 = jnp.bfloat16

NORM_EPS = 1e-6
ROPE_BASE = 10000.0
CHUNK = 64
RET_HEADS = 4
SB_HEADS = 8
CB_HEADS = 8
LRU_BLOCKS = 8
CONV_W = 4
LRU_C = 8.0
CB_PREV_CHUNKS = 8
CB_PAST = CB_PREV_CHUNKS * CHUNK
CB_BAND = (CB_PREV_CHUNKS + 1) * CHUNK
CB_MAX_REL = 128
NEG_BIG = -1e30
TOP_K = 2
LANES = 128


def _row_tile(m, cap=512):
    t = cap
    while m % t:
        t //= 2
    return t


def _rms(x, g):
    ms = jnp.mean(x * x, axis=-1, keepdims=True)
    return x * lax.rsqrt(ms + NORM_EPS) * g


def _dot(a, b):
    return jnp.dot(a, b, preferred_element_type=F32)


def _dot_nt(a, b):
    return lax.dot_general(a, b, (((1,), (1,)), ((), ())), preferred_element_type=F32)


def _dot_tn(a, b):
    return lax.dot_general(a, b, (((0,), (0,)), ((), ())), preferred_element_type=F32)


def _split(x):
    hi = x.astype(BF16)
    lo = (x - hi.astype(F32)).astype(BF16)
    return hi, lo


def _dot3(a, b):
    a_hi, a_lo = _split(a)
    b_hi, b_lo = _split(b)
    return _dot(a_hi, b_hi) + _dot(a_hi, b_lo) + _dot(a_lo, b_hi)


def _softplus(z):
    return jnp.maximum(z, 0.0) + jnp.log(1.0 + jnp.exp(-jnp.abs(z)))


def _sigmoid(z):
    return 1.0 / (1.0 + jnp.exp(-z))


def _group_specs(shape, n_first):
    return (pl.BlockSpec(shape, lambda i, *_: (jnp.minimum(i, n_first - 1), 0)),
            pl.BlockSpec(shape, lambda i, *_: (jnp.maximum(i - n_first, 0), 0)))


def _group_col_specs(shape, n_first):
    return (pl.BlockSpec(shape, lambda i, j: (jnp.minimum(i, n_first - 1), j)),
            pl.BlockSpec(shape, lambda i, j: (jnp.maximum(i - n_first, 0), j)))


def _as_groups(x, tm=None):
    if not isinstance(x, tuple):
        x = (x, x[:0])
    m = x[0].shape[0] + x[1].shape[0]
    if tm is None:
        tm = _row_tile(math.gcd(x[0].shape[0], x[1].shape[0]))
    rest = x[1] if x[1].shape[0] else x[0]
    return x[0], rest, m, tm, x[0].shape[0] // tm


def _norm_matmul_kernel(x1_ref, x2_ref, g_ref, w_ref, o_ref, h_ref, *, n_first):
    first = pl.program_id(0) < n_first
    start = pl.program_id(1) == 0

    @pl.when(start & first)
    def _():
        h_ref[...] = _rms(x1_ref[...], g_ref[...]).astype(BF16)

    @pl.when(start & jnp.logical_not(first))
    def _():
        h_ref[...] = _rms(x2_ref[...], g_ref[...]).astype(BF16)

    o_ref[...] = _dot(h_ref[...], w_ref[...])


def norm_matmul(x, g, w, tn=1024):
    k, n = w.shape
    x1, x2, m, tm, n_first = _as_groups(x)
    return pl.pallas_call(
        functools.partial(_norm_matmul_kernel, n_first=n_first),
        grid=(m // tm, n // tn),
        in_specs=[
            *_group_specs((tm, k), n_first),
            pl.BlockSpec((1, k), lambda i, j: (0, 0)),
            pl.BlockSpec((k, tn), lambda i, j: (0, j)),
        ],
        out_specs=pl.BlockSpec((tm, tn), lambda i, j: (i, j)),
        out_shape=jax.ShapeDtypeStruct((m, n), F32),
        scratch_shapes=[pltpu.VMEM((tm, k), BF16)],
        compiler_params=pltpu.CompilerParams(dimension_semantics=("parallel", "arbitrary")),
        name="norm_matmul",
    )(x1, x2, g.reshape(1, k), w)


def _out_proj_kernel(x1_ref, x2_ref, a1_ref, a2_ref, b1_ref, b2_ref, wa_ref, wb_ref, o_ref, *, n_first, x_split):
    i = pl.program_id(0)

    @pl.when(i < n_first)
    def _():
        o_ref[...] = x1_ref[...] + _dot(a1_ref[...], wa_ref[...]) + _dot(b1_ref[...], wb_ref[...])

    @pl.when(i >= n_first)
    def _():
        x_ref = x2_ref if x_split else x1_ref
        o_ref[...] = x_ref[...] + _dot(a2_ref[...], wa_ref[...]) + _dot(b2_ref[...], wb_ref[...])


def out_proj(x, a, b, w, tn=1024):
    n = w.shape[1]
    kh = a[0].shape[1]
    a1, a2, m, tm, n_first = _as_groups(a)
    b1, b2 = b
    x1, x2, _, _, x_first = _as_groups(x, tm)
    assert x_first in (n_first, m // tm)
    return pl.pallas_call(
        functools.partial(_out_proj_kernel, n_first=n_first, x_split=x_first == n_first),
        grid=(m // tm, n // tn),
        in_specs=[
            *_group_col_specs((tm, tn), x_first),
            *_group_specs((tm, kh), n_first),
            *_group_specs((tm, kh), n_first),
            pl.BlockSpec((kh, tn), lambda i, j: (0, j)),
            pl.BlockSpec((kh, tn), lambda i, j: (1, j)),
        ],
        out_specs=pl.BlockSpec((tm, tn), lambda i, j: (i, j)),
        out_shape=jax.ShapeDtypeStruct((m, n), F32),
        compiler_params=pltpu.CompilerParams(dimension_semantics=("parallel", "arbitrary")),
        name="out_proj",
    )(x1, x2, a1, a2, b1, b2, w, w)


def _ffn_kernel(x_ref, g_ref, w1_ref, w3_ref, w2_ref, o_ref, h_ref, acc_ref):
    f = pl.program_id(1)

    @pl.when(f == 0)
    def _():
        h_ref[...] = _rms(x_ref[...], g_ref[...]).astype(BF16)
        acc_ref[...] = jnp.zeros_like(acc_ref)

    h = h_ref[...]
    a = _dot(h, w1_ref[...])
    u = _dot(h, w3_ref[...])
    acc_ref[...] += _dot((a * _sigmoid(a) * u).astype(BF16), w2_ref[...])

    @pl.when(f == pl.num_programs(1) - 1)
    def _():
        o_ref[...] = x_ref[...] + acc_ref[...]


def ffn(x, g, w1, w3, w2, tf=512):
    m, d = x.shape
    ff = w1.shape[1]
    tm = _row_tile(m)
    return pl.pallas_call(
        _ffn_kernel,
        grid=(m // tm, ff // tf),
        in_specs=[
            pl.BlockSpec((tm, d), lambda i, f: (i, 0)),
            pl.BlockSpec((1, d), lambda i, f: (0, 0)),
            pl.BlockSpec((d, tf), lambda i, f: (0, f)),
            pl.BlockSpec((d, tf), lambda i, f: (0, f)),
            pl.BlockSpec((tf, d), lambda i, f: (f, 0)),
        ],
        out_specs=pl.BlockSpec((tm, d), lambda i, f: (i, 0)),
        out_shape=jax.ShapeDtypeStruct((m, d), F32),
        scratch_shapes=[pltpu.VMEM((tm, d), BF16), pltpu.VMEM((tm, d), F32)],
        compiler_params=pltpu.CompilerParams(dimension_semantics=("parallel", "arbitrary")),
        name="ffn",
    )(x, g.reshape(1, d), w1, w3, w2)


def _router_kernel(x_ref, g_ref, r_ref, eid_ref, p_ref, rank_ref, cnt_ref, run_ref):
    i = pl.program_id(0)

    @pl.when(i == 0)
    def _():
        run_ref[...] = jnp.zeros_like(run_ref)

    h = _rms(x_ref[...], g_ref[...])
    tm, d = h.shape
    logits = _dot3(h, r_ref[...])
    n_e = logits.shape[1]
    lane = lax.broadcasted_iota(jnp.int32, logits.shape, 1).astype(F32)
    m1 = jnp.max(logits, axis=1, keepdims=True)
    i1 = jnp.min(jnp.where(logits == m1, lane, float(n_e)), axis=1, keepdims=True)
    rest = jnp.where(lane == i1, -jnp.inf, logits)
    m2 = jnp.max(rest, axis=1, keepdims=True)
    i2 = jnp.min(jnp.where(rest == m2, lane, float(n_e)), axis=1, keepdims=True)
    e2 = jnp.exp(m2 - m1)
    p1 = 1.0 / (1.0 + e2)
    slot = lax.broadcasted_iota(jnp.int32, (tm, TOP_K), 1)
    eid_ref[...] = jnp.where(slot == 0, i1, i2).astype(jnp.int32)
    p_ref[...] = jnp.where(slot == 0, p1, e2 * p1)

    hit1 = lane == i1
    hit2 = lane == i2
    both = jnp.where(hit1 | hit2, 1.0, 0.0)
    earlier = (lax.broadcasted_iota(jnp.int32, (tm, tm), 1) < lax.broadcasted_iota(jnp.int32, (tm, tm), 0))
    before = _dot(jnp.where(earlier, 1.0, 0.0).astype(BF16), both.astype(BF16)) + run_ref[...]
    r1 = jnp.sum(jnp.where(hit1, before, 0.0), axis=1, keepdims=True)
    r2 = jnp.sum(jnp.where(hit2, before, 0.0), axis=1, keepdims=True)
    rank_ref[...] = jnp.where(slot == 0, r1, r2).astype(jnp.int32)
    run_ref[...] += jnp.sum(both, axis=0, keepdims=True)
    cnt_ref[...] = run_ref[...].astype(jnp.int32)


def moe_route(x, g, router):
    m, d = x.shape
    n_e = router.shape[1]
    tm = _row_tile(m)
    pair = pl.BlockSpec((tm, TOP_K), lambda i: (i, 0))
    return pl.pallas_call(
        _router_kernel,
        grid=(m // tm,),
        in_specs=[
            pl.BlockSpec((tm, d), lambda i: (i, 0)),
            pl.BlockSpec((1, d), lambda i: (0, 0)),
            pl.BlockSpec((d, n_e), lambda i: (0, 0)),
        ],
        out_specs=[pair, pair, pair, pl.BlockSpec((1, n_e), lambda i: (0, 0))],
        out_shape=[
            jax.ShapeDtypeStruct((m, TOP_K), jnp.int32),
            jax.ShapeDtypeStruct((m, TOP_K), F32),
            jax.ShapeDtypeStruct((m, TOP_K), jnp.int32),
            jax.ShapeDtypeStruct((1, n_e), jnp.int32),
        ],
        scratch_shapes=[pltpu.VMEM((1, n_e), F32)],
        compiler_params=pltpu.CompilerParams(dimension_semantics=("arbitrary",)),
        name="moe_route",
    )(x, g.reshape(1, d), router)


SCALAR_UNROLL = 8


def _row_copy(src, s, dst, t, sem):
    return pltpu.make_async_copy(src.at[pl.ds(s, 1)], dst.at[pl.ds(t, 1)], sem)


def _invert_kernel(pos_ref, src_ref):
    def clear(r, _):
        src_ref[r] = 0
        return 0

    def put(t, _):
        for k in range(TOP_K):
            src_ref[pos_ref[TOP_K * t + k]] = t
        return 0

    lax.fori_loop(0, src_ref.shape[0], clear, 0, unroll=SCALAR_UNROLL)
    lax.fori_loop(0, pos_ref.shape[0] // TOP_K, put, 0, unroll=SCALAR_UNROLL)


def moe_invert(pos, n_rows):
    assert n_rows % SCALAR_UNROLL == 0 and pos.shape[0] % SCALAR_UNROLL == 0
    return pl.pallas_call(
        _invert_kernel,
        in_specs=[pl.BlockSpec(memory_space=pltpu.SMEM)],
        out_specs=pl.BlockSpec(memory_space=pltpu.SMEM),
        out_shape=jax.ShapeDtypeStruct((n_rows,), jnp.int32),
        name="moe_invert",
    )(pos.reshape(-1))


def _gather_kernel(src_ref, nu_ref, x_ref, g_ref, hs_ref, xs_ref, sem):
    r = pl.program_id(0)
    tm = hs_ref.shape[0]
    slot = r % 2

    def issue(tile, into):
        def body(j, _):
            _row_copy(x_ref, src_ref[tile * tm + j], xs_ref.at[into], j, sem.at[into]).start()
            return 0
        lax.fori_loop(0, tm, body, 0, unroll=SCALAR_UNROLL)

    @pl.when(r == 0)
    def _():
        issue(0, 0)

    @pl.when(r + 1 < nu_ref[0])
    def _():
        issue(r + 1, 1 - slot)

    @pl.when(r < nu_ref[0])
    def _():
        def drain(j, _):
            _row_copy(x_ref, 0, xs_ref.at[slot], 0, sem.at[slot]).wait()
            return 0

        lax.fori_loop(0, tm, drain, 0, unroll=SCALAR_UNROLL)
        hs_ref[...] = _rms(xs_ref[slot], g_ref[...]).astype(BF16)

    @pl.when(r >= nu_ref[0])
    def _():
        hs_ref[...] = jnp.zeros_like(hs_ref)


def moe_gather(src, n_used, x, g, tm):
    n_rows = src.shape[0]
    d = x.shape[1]
    return pl.pallas_call(
        _gather_kernel,
        grid_spec=pltpu.PrefetchScalarGridSpec(
            num_scalar_prefetch=2,
            grid=(n_rows // tm,),
            in_specs=[pl.BlockSpec(memory_space=pl.ANY), pl.BlockSpec((1, d), lambda r, src, nu: (0, 0))],
            out_specs=pl.BlockSpec((tm, d), lambda r, src, nu: (r, 0)),
            scratch_shapes=[pltpu.VMEM((2, tm, d), F32), pltpu.SemaphoreType.DMA((2,))],
        ),
        out_shape=jax.ShapeDtypeStruct((n_rows, d), BF16),
        compiler_params=pltpu.CompilerParams(dimension_semantics=("arbitrary",)),
        name="moe_gather",
    )(src, n_used, x, g.reshape(1, d))


def _expert_rows(first_ref, count_ref, src_ref, dst_ref, ibuf_ref, obuf_ref, sem_in, sem_out, compute):
    e = pl.program_id(0)
    c = pl.program_id(1)
    tm = ibuf_ref.shape[1]
    width = obuf_ref.shape[2]
    first = first_ref[e]
    n_real = count_ref[e]
    n_all = jnp.where(e == pl.num_programs(0) - 1, dst_ref.shape[0] // tm - first, n_real)
    cols = pl.ds(pl.multiple_of(c * width, width), width)

    def rows(k, base=first):
        return pl.ds(pl.multiple_of((base + k) * tm, tm), tm)

    def load(k, slot, base=first):
        return pltpu.make_async_copy(src_ref.at[rows(k, base)], ibuf_ref.at[slot], sem_in.at[slot])

    def store(k, slot):
        return pltpu.make_async_copy(obuf_ref.at[slot], dst_ref.at[rows(k), cols], sem_out.at[slot])

    @pl.when((e == 0) & (c == 0) & (n_real > 0))
    def _():
        load(0, 0).start()

    def body(k, _):
        slot = k % 2

        @pl.when(k >= 2)
        def _():
            store(k - 2, slot).wait()

        @pl.when(k < n_real)
        def _():
            load(k, slot).wait()

            @pl.when(k + 1 < n_real)
            def _():
                load(k + 1, 1 - slot).start()

            obuf_ref[slot] = compute(ibuf_ref[slot])

        @pl.when(k >= n_real)
        def _():
            obuf_ref[slot] = jnp.zeros(obuf_ref.shape[1:], obuf_ref.dtype)

        store(k, slot).start()
        return 0

    lax.fori_loop(0, n_all, body, 0)

    last_c = c == pl.num_programs(1) - 1
    e_next = jnp.where(last_c, jnp.minimum(e + 1, pl.num_programs(0) - 1), e)
    more = jnp.logical_not(last_c & (e == pl.num_programs(0) - 1))

    @pl.when(more & (count_ref[e_next] > 0))
    def _():
        load(0, 0, first_ref[e_next]).start()

    for back in (2, 1):
        @pl.when(n_all >= back)
        def _():
            store(n_all - back, (n_all - back) % 2).wait()


def _moe_up_kernel(first_ref, count_ref, hs_ref, w1_ref, w3_ref, act_ref, w1b_ref, w3b_ref, ibuf_ref, obuf_ref,
                   sem_in, sem_out):
    w1b_ref[...] = w1_ref[...].astype(BF16)
    w3b_ref[...] = w3_ref[...].astype(BF16)

    def compute(h):
        a = _dot(h, w1b_ref[...])
        u = _dot(h, w3b_ref[...])
        return (a * _sigmoid(a) * u).astype(BF16)

    _expert_rows(first_ref, count_ref, hs_ref, act_ref, ibuf_ref, obuf_ref, sem_in, sem_out, compute)


def _moe_down_kernel(first_ref, count_ref, act_ref, w2_ref, y_ref, w2b_ref, ibuf_ref, obuf_ref, sem_in, sem_out):
    w2b_ref[...] = w2_ref[...].astype(BF16)
    _expert_rows(first_ref, count_ref, act_ref, y_ref, ibuf_ref, obuf_ref, sem_in, sem_out,
                 lambda a: _dot(a, w2b_ref[...]))


def moe_experts(tile_first, tile_count, hs, w1, w3, w2, tm, tf=512, tn=512):
    n_rows, d = hs.shape
    n_e, _, ff = w1.shape
    any_spec = pl.BlockSpec(memory_space=pl.ANY)
    sems = [pltpu.SemaphoreType.DMA((2,)), pltpu.SemaphoreType.DMA((2,))]
    act = pl.pallas_call(
        _moe_up_kernel,
        grid_spec=pltpu.PrefetchScalarGridSpec(
            num_scalar_prefetch=2,
            grid=(n_e, ff // tf),
            in_specs=[
                any_spec,
                pl.BlockSpec((None, d, tf), lambda e, f, *_: (e, 0, f)),
                pl.BlockSpec((None, d, tf), lambda e, f, *_: (e, 0, f)),
            ],
            out_specs=any_spec,
            scratch_shapes=[pltpu.VMEM((d, tf), BF16), pltpu.VMEM((d, tf), BF16),
                            pltpu.VMEM((2, tm, d), BF16), pltpu.VMEM((2, tm, tf), BF16), *sems],
        ),
        out_shape=jax.ShapeDtypeStruct((n_rows, ff), BF16),
        compiler_params=pltpu.CompilerParams(dimension_semantics=("arbitrary", "arbitrary"),
                                             vmem_limit_bytes=48 * 1024 * 1024),
        name="moe_up",
    )(tile_first, tile_count, hs, w1, w3)
    return pl.pallas_call(
        _moe_down_kernel,
        grid_spec=pltpu.PrefetchScalarGridSpec(
            num_scalar_prefetch=2,
            grid=(n_e, d // tn),
            in_specs=[any_spec, pl.BlockSpec((None, ff, tn), lambda e, n, *_: (e, 0, n))],
            out_specs=any_spec,
            scratch_shapes=[pltpu.VMEM((ff, tn), BF16),
                            pltpu.VMEM((2, tm, ff), BF16), pltpu.VMEM((2, tm, tn), F32), *sems],
        ),
        out_shape=jax.ShapeDtypeStruct((n_rows, d), F32),
        compiler_params=pltpu.CompilerParams(dimension_semantics=("arbitrary", "arbitrary"),
                                             vmem_limit_bytes=56 * 1024 * 1024),
        name="moe_down",
    )(tile_first, tile_count, act, w2)


def _combine_kernel(pos_ref, x_ref, p_ref, y_ref, o1_ref, o2_ref, ya_ref, yb_ref, sem, *, n_first):
    tm = x_ref.shape[0]
    i = pl.program_id(0)
    slot = i % 2

    def issue(tile, into):
        def body(j, _):
            t = tile * tm + j
            _row_copy(y_ref, pos_ref[TOP_K * t], ya_ref.at[into], j, sem.at[into]).start()
            _row_copy(y_ref, pos_ref[TOP_K * t + 1], yb_ref.at[into], j, sem.at[into]).start()
            return 0
        lax.fori_loop(0, tm, body, 0, unroll=SCALAR_UNROLL)

    def drain(j, _):
        _row_copy(y_ref, 0, ya_ref.at[slot], 0, sem.at[slot]).wait()
        _row_copy(y_ref, 0, yb_ref.at[slot], 0, sem.at[slot]).wait()
        return 0

    @pl.when(i == 0)
    def _():
        issue(0, 0)

    @pl.when(i + 1 < pl.num_programs(0))
    def _():
        issue(i + 1, 1 - slot)

    lax.fori_loop(0, tm, drain, 0, unroll=SCALAR_UNROLL)
    p = p_ref[...]
    out = x_ref[...] + p[:, 0:1] * ya_ref[slot] + p[:, 1:2] * yb_ref[slot]

    @pl.when(i < n_first)
    def _():
        o1_ref[...] = out

    @pl.when(i >= n_first)
    def _():
        o2_ref[...] = out


def moe_combine(pos, x, p, y, m_first, tm=128):
    m, d = x.shape
    assert m_first % tm == 0 and m % tm == 0
    n_first = m_first // tm
    return pl.pallas_call(
        functools.partial(_combine_kernel, n_first=n_first),
        grid_spec=pltpu.PrefetchScalarGridSpec(
            num_scalar_prefetch=1,
            grid=(m // tm,),
            in_specs=[
                pl.BlockSpec((tm, d), lambda i, pos: (i, 0)),
                pl.BlockSpec((tm, TOP_K), lambda i, pos: (i, 0)),
                pl.BlockSpec(memory_space=pl.ANY),
            ],
            out_specs=list(_group_specs((tm, d), n_first)),
            scratch_shapes=[pltpu.VMEM((2, tm, d), F32), pltpu.VMEM((2, tm, d), F32),
                            pltpu.SemaphoreType.DMA((2,))],
        ),
        out_shape=[jax.ShapeDtypeStruct((m_first, d), F32), jax.ShapeDtypeStruct((m - m_first, d), F32)],
        compiler_params=pltpu.CompilerParams(dimension_semantics=("arbitrary",)),
        name="moe_combine",
    )(pos.reshape(-1), x, p, y)


def moe_layer(x, g, router, w1, w3, w2, m_first, tm=512):
    m = x.shape[0]
    n_e = router.shape[1]
    eid, p, rank, counts = moe_route(x, g, router)
    padded = (counts[0] + tm - 1) // tm * tm
    ends = jnp.cumsum(padded)
    starts = ends - padded
    experts = jnp.arange(n_e, dtype=jnp.int32)
    pos = rank + jnp.sum(jnp.where(eid[:, :, None] == experts, starts, 0), axis=-1)
    n_tiles = -(-TOP_K * m // tm) + n_e
    n_used = (ends[-1:] // tm).astype(jnp.int32)
    hs = moe_gather(moe_invert(pos, n_tiles * tm), n_used, x, g, tm)
    y = moe_experts((starts // tm).astype(jnp.int32), (padded // tm).astype(jnp.int32), hs, w1, w3, w2, tm)
    return moe_combine(pos, x, p, y, m_first)


def _ret_kernel(q_ref, k_ref, v_ref, g_ref, cos_ref, sin_ref, gn_ref, s0_ref, o_ref, s_ref, st_ref,
                *, has_state):
    n = pl.program_id(1)
    c = q_ref.shape[0]
    dk = q_ref.shape[1] // RET_HEADS
    half = dk // 2

    @pl.when(n == 0)
    def _():
        if has_state:
            st_ref[...] = s0_ref[...]
        else:
            st_ref[...] = jnp.zeros_like(st_ref)

    cos = cos_ref[...]
    sin = sin_ref[...]
    row = lax.broadcasted_iota(jnp.int32, (c, c), 0)
    col = lax.broadcasted_iota(jnp.int32, (c, c), 1)
    diff = (row - col).astype(F32)
    ridx = lax.broadcasted_iota(jnp.int32, (c, 1), 0).astype(F32)

    def rot(x):
        x1, x2 = x[:, :half], x[:, half:]
        return jnp.concatenate([x1 * cos - x2 * sin, x1 * sin + x2 * cos], axis=1)

    for h in range(RET_HEADS):
        log_g = math.log1p(-(2.0 ** (-5.0 - h)))
        sl = slice(h * dk, (h + 1) * dk)
        qr = rot(q_ref[:, sl]) * (dk ** -0.5)
        kr = rot(k_ref[:, sl])
        vh = v_ref[:, sl].astype(BF16)
        decay = jnp.where(diff >= 0, jnp.exp(log_g * jnp.maximum(diff, 0.0)), 0.0)
        scores = _dot_nt(qr.astype(BF16), kr.astype(BF16)) * decay
        s = st_ref[h]
        q_dec = jnp.exp(log_g * (ridx + 1.0))
        k_dec = jnp.exp(log_g * (c - 1.0 - ridx))
        o = _dot(scores.astype(BF16), vh) + _dot((qr * q_dec).astype(BF16), s.astype(BF16))
        st_ref[h] = s * math.exp(log_g * c) + _dot_tn((kr * k_dec).astype(BF16), vh)
        gh = g_ref[:, sl]
        o_ref[:, sl] = (gh * _sigmoid(gh) * _rms(o, gn_ref[h:h + 1, :])).astype(BF16)

    @pl.when(n == pl.num_programs(1) - 1)
    def _():
        s_ref[...] = st_ref[...]


def retention_mixer(qkv, row0, batch, seq, cos, sin, gn, s0):
    half_d = qkv.shape[1] // 7
    dk = half_d // RET_HEADS
    c = next((t for t in (2 * CHUNK, CHUNK) if seq % t == 0 and row0 % t == 0), seq)
    nc = seq // c
    blk0 = row0 // c
    has_state = s0 is not None
    if not has_state:
        s0 = jnp.zeros((1, RET_HEADS, dk, dk), F32)

    def col(j):
        return pl.BlockSpec((c, half_d), lambda b, n: (blk0 + b * nc + n, j))

    return pl.pallas_call(
        functools.partial(_ret_kernel, has_state=has_state),
        grid=(batch, nc),
        in_specs=[
            col(0), col(1), col(2), col(3),
            pl.BlockSpec((c, dk // 2), lambda b, n: (n, 0)),
            pl.BlockSpec((c, dk // 2), lambda b, n: (n, 0)),
            pl.BlockSpec((RET_HEADS, dk), lambda b, n: (0, 0)),
            pl.BlockSpec((None, RET_HEADS, dk, dk), lambda b, n: (b if has_state else 0, 0, 0, 0)),
        ],
        out_specs=[
            pl.BlockSpec((c, half_d), lambda b, n: (b * nc + n, 0)),
            pl.BlockSpec((None, RET_HEADS, dk, dk), lambda b, n: (b, 0, 0, 0)),
        ],
        out_shape=[
            jax.ShapeDtypeStruct((batch * seq, half_d), BF16),
            jax.ShapeDtypeStruct((batch, RET_HEADS, dk, dk), F32),
        ],
        scratch_shapes=[pltpu.VMEM((RET_HEADS, dk, dk), F32)],
        compiler_params=pltpu.CompilerParams(dimension_semantics=("parallel", "arbitrary")),
        name="retention",
    )(qkv, qkv, qkv, qkv, cos, sin, gn, s0)


def _sb_norm_kernel(q_ref, k_ref, v_ref, qg_ref, kg_ref, qn_ref, knb_ref, vb_ref, k1_ref, k2_ref, v1_ref, v2_ref,
                    *, n_first):
    dh = qg_ref.shape[1]
    first = pl.program_id(0) < n_first
    for h in range(q_ref.shape[1] // dh):
        sl = slice(h * dh, (h + 1) * dh)
        qn_ref[:, sl] = (_rms(q_ref[:, sl], qg_ref[...]) * (dh ** -0.5)).astype(BF16)
        kn = _rms(k_ref[:, sl], kg_ref[...])
        knb_ref[:, sl] = kn.astype(BF16)

        @pl.when(first)
        def _():
            k1_ref[:, sl] = kn

        @pl.when(jnp.logical_not(first))
        def _():
            k2_ref[:, sl] = kn

    v = v_ref[...]
    vb_ref[...] = v.astype(BF16)

    @pl.when(first)
    def _():
        v1_ref[...] = v

    @pl.when(jnp.logical_not(first))
    def _():
        v2_ref[...] = v


def sb_norm(qkv, qg, kg, m_first):
    m = qkv.shape[0]
    half_d = qkv.shape[1] // 7
    dh = qg.shape[0]
    tm = _row_tile(math.gcd(m_first, m - m_first))
    n_first = m_first // tm

    def col(j):
        return pl.BlockSpec((tm, half_d), lambda i: (i, j))

    out = pl.BlockSpec((tm, half_d), lambda i: (i, 0))
    gspec = pl.BlockSpec((1, dh), lambda i: (0, 0))
    split = _group_specs((tm, half_d), n_first)
    f32_first = jax.ShapeDtypeStruct((m_first, half_d), F32)
    f32_rest = jax.ShapeDtypeStruct((m - m_first, half_d), F32)
    return pl.pallas_call(
        functools.partial(_sb_norm_kernel, n_first=n_first),
        grid=(m // tm,),
        in_specs=[col(4), col(5), col(6), gspec, gspec],
        out_specs=[out] * 3 + [*split, *split],
        out_shape=[jax.ShapeDtypeStruct((m, half_d), BF16)] * 3 + [f32_first, f32_rest, f32_first, f32_rest],
        compiler_params=pltpu.CompilerParams(dimension_semantics=("arbitrary",)),
        name="sb_norm",
    )(qkv, qkv, qkv, qg.reshape(1, dh), kg.reshape(1, dh))


SB_EXIT_LOG = 88.0


def _sb_block(q, kb, vb, carry, acc, strict_diag):
    tq, tk = q.shape[0], kb.shape[0]
    z = _dot_nt(q, kb)
    sp = _softplus(z)
    log_keep = -sp
    if strict_diag:
        mask = (lax.broadcasted_iota(jnp.int32, (tq, tk), 1) < lax.broadcasted_iota(jnp.int32, (tq, tk), 0))
        log_keep = jnp.where(mask, log_keep, 0.0)
    later = (lax.broadcasted_iota(jnp.int32, (tk, tk), 0) > lax.broadcasted_iota(jnp.int32, (tk, tk), 1))
    u = jnp.where(later, 1.0, 0.0).astype(BF16)
    hi, lo = _split(log_keep)
    after = _dot(hi, u) + _dot(lo, u) + carry
    w = jnp.exp(z - sp + after)
    if strict_diag:
        w = jnp.where(mask, w, 0.0)
    return jnp.sum(log_keep, axis=1, keepdims=True), acc + _dot(w.astype(BF16), vb)


def _sb_visit(q_ref, carry_ref, acc_ref, key_block, strict_diag):
    heads = q_ref.shape[1] // LANES
    worst = None
    for h in range(heads):
        sl = slice(h * LANES, (h + 1) * LANES)
        kb, vb = key_block(h, sl)
        tk = kb.shape[0]
        carry = carry_ref[h]
        block_sum, acc = _sb_block(q_ref[:, sl], kb, vb, carry[:, :tk], acc_ref[:, sl], strict_diag)
        carry = carry + block_sum
        carry_ref[h] = carry
        acc_ref[:, sl] = acc
        worst = carry if worst is None else jnp.maximum(worst, carry)
    return jnp.max(worst)


def _sb_prompt_kernel(q_ref, k_ref, v_ref, o_ref, acc_ref, carry_ref):
    i = pl.program_id(1)
    tq = q_ref.shape[0]
    acc_ref[...] = jnp.zeros_like(acc_ref)
    carry_ref[...] = jnp.zeros_like(carry_ref)

    def visit(j, strict_diag):
        s = pl.multiple_of(j * tq, tq)
        return _sb_visit(q_ref, carry_ref, acc_ref,
                         lambda h, sl: (k_ref[pl.ds(s, tq), sl], v_ref[pl.ds(s, tq), sl]), strict_diag)

    top = visit(i, True)
    lax.while_loop(lambda st: (st[0] >= 0) & (st[1] > -SB_EXIT_LOG),
                   lambda st: (st[0] - 1, visit(st[0], False)), (i - 1, top))
    o_ref[...] = acc_ref[...].astype(BF16)


def sb_attention_prompt(qn, knb, vb, batch, seq, tq=LANES):
    width = qn.shape[1]
    nq = seq // tq
    return pl.pallas_call(
        _sb_prompt_kernel,
        grid=(batch, nq),
        in_specs=[
            pl.BlockSpec((tq, width), lambda b, i: (b * nq + i, 0)),
            pl.BlockSpec((seq, width), lambda b, i: (b, 0)),
            pl.BlockSpec((seq, width), lambda b, i: (b, 0)),
        ],
        out_specs=pl.BlockSpec((tq, width), lambda b, i: (b * nq + i, 0)),
        out_shape=jax.ShapeDtypeStruct((batch * seq, width), BF16),
        scratch_shapes=[pltpu.VMEM((tq, width), F32), pltpu.VMEM((width // LANES, tq, LANES), F32)],
        compiler_params=pltpu.CompilerParams(dimension_semantics=("parallel", "arbitrary")),
        name="sb_attention_prompt",
    )(qn, knb, vb)


def _sb_sample_kernel(q_ref, k_ref, v_ref, kp_ref, vp_ref, o_ref, acc_ref, carry_ref, kbuf_ref, vbuf_ref, sem):
    b = pl.program_id(0)
    tk = kbuf_ref.shape[1]
    n_past = kp_ref.shape[1] // tk
    acc_ref[...] = jnp.zeros_like(acc_ref)
    carry_ref[...] = jnp.zeros_like(carry_ref)

    def fetch(j):
        slot = (n_past - 1 - j) % 2
        rows = pl.ds(pl.multiple_of(j * tk, tk), tk)
        return (pltpu.make_async_copy(kp_ref.at[b, rows], kbuf_ref.at[slot], sem.at[0, slot]),
                pltpu.make_async_copy(vp_ref.at[b, rows], vbuf_ref.at[slot], sem.at[1, slot]))

    def start(j):
        for c in fetch(j):
            c.start()

    def wait(j):
        for c in fetch(j):
            c.wait()

    start(n_past - 1)
    top = _sb_visit(q_ref, carry_ref, acc_ref, lambda h, sl: (k_ref[:, sl], v_ref[:, sl]), True)

    def body(st):
        j = st[0]
        slot = (n_past - 1 - j) % 2
        wait(j)

        @pl.when(j >= 1)
        def _():
            start(j - 1)

        top = _sb_visit(q_ref, carry_ref, acc_ref,
                        lambda h, sl: (kbuf_ref[slot, :, h, :].astype(BF16), vbuf_ref[slot, :, h, :].astype(BF16)),
                        False)
        return j - 1, top

    j_end, _ = lax.while_loop(lambda st: (st[0] >= 0) & (st[1] > -SB_EXIT_LOG), body, (n_past - 1, top))

    @pl.when(j_end >= 0)
    def _():
        wait(j_end)

    o_ref[...] = acc_ref[...].astype(BF16)


def sb_attention_sample(qn, knb, vb, k_past, v_past, row0, batch, seq, tk=LANES):
    width = qn.shape[1]
    heads = width // LANES
    blk0 = row0 // seq
    assert k_past.shape[1] % tk == 0 and k_past.shape[2:] == (heads, LANES)
    new = pl.BlockSpec((seq, width), lambda b: (blk0 + b, 0))
    return pl.pallas_call(
        _sb_sample_kernel,
        grid=(batch,),
        in_specs=[new, new, new, pl.BlockSpec(memory_space=pl.ANY), pl.BlockSpec(memory_space=pl.ANY)],
        out_specs=pl.BlockSpec((seq, width), lambda b: (b, 0)),
        out_shape=jax.ShapeDtypeStruct((batch * seq, width), BF16),
        scratch_shapes=[
            pltpu.VMEM((seq, width), F32), pltpu.VMEM((heads, seq, LANES), F32),
            pltpu.VMEM((2, tk, heads, LANES), F32), pltpu.VMEM((2, tk, heads, LANES), F32),
            pltpu.SemaphoreType.DMA((2, 2)),
        ],
        compiler_params=pltpu.CompilerParams(dimension_semantics=("arbitrary",)),
        name="sb_attention_sample",
    )(qn, knb, vb, k_past, v_past)


LRU_SCAN_ROWS = 128


def _lru_kernel(gc_ref, xc_ref, buf_ref, h0_ref, cw_ref, cb_ref, wr_ref, br_ref, wi_ref, bi_ref, lam_ref,
                o_ref, conv_ref, hl_ref, *, has_state):
    seq, bw = xc_ref.shape
    x = xc_ref[...]
    row = lax.broadcasted_iota(jnp.int32, (seq, bw), 0)
    cw = cw_ref[...]
    y = cb_ref[...] + cw[CONV_W - 1:CONV_W, :] * x
    for d in range(1, CONV_W):
        xs = pltpu.roll(x, d, 0)
        for r in range(d):
            prev = buf_ref[CONV_W - 1 - d + r:CONV_W - d + r, :] if has_state else jnp.zeros((1, bw), F32)
            xs = jnp.where(row == r, prev, xs)
        y = y + cw[CONV_W - 1 - d:CONV_W - d, :] * xs
    conv_ref[...] = x[seq - (CONV_W - 1):, :]

    r_gate = _sigmoid(_dot3(y, wr_ref[...]) + br_ref[...])
    i_gate = _sigmoid(_dot3(y, wi_ref[...]) + bi_ref[...])
    log_a = -LRU_C * r_gate * _softplus(-lam_ref[...])
    a = jnp.exp(log_a)
    u = jnp.sqrt(-jnp.tanh(log_a) * (a * a + 1.0)) * (i_gate * y)
    if has_state:
        u = jnp.where(row == 0, u + a * h0_ref[...], u)

    blk = math.gcd(seq, LRU_SCAN_ROWS)
    row_b = lax.broadcasted_iota(jnp.int32, (blk, bw), 0)
    h_prev = None
    for r0 in range(0, seq, blk):
        ab = a[r0:r0 + blk, :]
        ub = u[r0:r0 + blk, :]
        shift = 1
        while shift < blk:
            a_s = pltpu.roll(ab, shift, 0)
            u_s = pltpu.roll(ub, shift, 0)
            live = row_b >= shift
            ub = jnp.where(live, ab * u_s + ub, ub)
            ab = jnp.where(live, ab * a_s, ab)
            shift *= 2
        if h_prev is not None:
            ub = ub + ab * h_prev
        h_prev = ub[blk - 1:, :]
        gc = gc_ref[r0:r0 + blk, :]
        gelu = 0.5 * gc * (1.0 + jnp.tanh(math.sqrt(2.0 / math.pi) * (gc + 0.044715 * gc * gc * gc)))
        o_ref[r0:r0 + blk, :] = (gelu * ub).astype(BF16)
    hl_ref[...] = h_prev


def lru_mixer(proj, row0, batch, seq, conv_buf, h0, cw, cb, wr, br, wi, bi, lam):
    width = proj.shape[1] // 5
    bw = width // LRU_BLOCKS
    blk0 = row0 // seq
    has_state = conv_buf is not None
    if not has_state:
        conv_buf = jnp.zeros((1, CONV_W - 1, width), F32)
        h0 = jnp.zeros((1, width), F32)
    h0 = h0.reshape(-1, 1, width)

    def vec(k):
        return pl.BlockSpec((k, bw), lambda b, n: (0, n))

    wspec = pl.BlockSpec((None, bw, bw), lambda b, n: (n, 0, 0))
    return pl.pallas_call(
        functools.partial(_lru_kernel, has_state=has_state),
        grid=(batch, LRU_BLOCKS),
        in_specs=[
            pl.BlockSpec((seq, bw), lambda b, n: (blk0 + b, n)),
            pl.BlockSpec((seq, bw), lambda b, n: (blk0 + b, LRU_BLOCKS + n)),
            pl.BlockSpec((None, CONV_W - 1, bw), lambda b, n: (b if has_state else 0, 0, n)),
            pl.BlockSpec((None, 1, bw), lambda b, n: (b if has_state else 0, 0, n)),
            vec(CONV_W), vec(1), wspec, vec(1), wspec, vec(1), vec(1),
        ],
        out_specs=[
            pl.BlockSpec((seq, bw), lambda b, n: (b, n)),
            pl.BlockSpec((None, CONV_W - 1, bw), lambda b, n: (b, 0, n)),
            pl.BlockSpec((None, 1, bw), lambda b, n: (b, 0, n)),
        ],
        out_shape=[
            jax.ShapeDtypeStruct((batch * seq, width), BF16),
            jax.ShapeDtypeStruct((batch, CONV_W - 1, width), F32),
            jax.ShapeDtypeStruct((batch, 1, width), F32),
        ],
        compiler_params=pltpu.CompilerParams(dimension_semantics=("parallel", "parallel")),
        name="rg_lru",
    )(proj, proj, conv_buf, h0, cw, cb.reshape(1, width), wr, br.reshape(1, width), wi,
      bi.reshape(1, width), lam.reshape(1, width))


def _cb_kernel(q_ref, k_ref, v_ref, kp_ref, vp_ref, qg_ref, kg_ref, bias_ref, o_ref, ko_ref, vo_ref,
               kb_ref, vb_ref, *, has_past):
    seq, dh = q_ref.shape
    keep = ko_ref.shape[0]
    kn = _rms(k_ref[...], kg_ref[...])
    v = v_ref[...]
    ko_ref[...] = kn[seq - keep:, :]
    vo_ref[...] = v[seq - keep:, :]
    if has_past:
        kb_ref[:CB_PAST, :] = kp_ref[...].astype(BF16)
        vb_ref[:CB_PAST, :] = vp_ref[...].astype(BF16)
    else:
        kb_ref[:CB_PAST, :] = jnp.zeros((CB_PAST, dh), BF16)
        vb_ref[:CB_PAST, :] = jnp.zeros((CB_PAST, dh), BF16)
    kb_ref[CB_PAST:, :] = kn.astype(BF16)
    vb_ref[CB_PAST:, :] = v.astype(BF16)
    tq, span = bias_ref.shape
    bias = bias_ref[...]
    kidx = lax.broadcasted_iota(jnp.int32, (tq, span), 1)

    def body(n, _):
        r0 = pl.multiple_of(n * tq, tq)
        qn = (_rms(q_ref[pl.ds(r0, tq), :], qg_ref[...]) * (dh ** -0.5)).astype(BF16)
        s = _dot_nt(qn, kb_ref[pl.ds(r0, span), :]) + bias
        if not has_past:
            s = jnp.where(kidx + r0 >= CB_PAST, s, NEG_BIG)
        p = jnp.exp(s - jnp.max(s, axis=1, keepdims=True))
        o = _dot(p.astype(BF16), vb_ref[pl.ds(r0, span), :]) / jnp.sum(p, axis=1, keepdims=True)
        o_ref[pl.ds(r0, tq), :] = o.astype(BF16)
        return 0

    steps = seq // tq
    lax.fori_loop(0, steps, body, 0, unroll=min(steps, 2))


def _block_bias(bias, g):
    rows = [jnp.pad(bias, ((0, 0), (0, 0), (c * CHUNK, (g - 1 - c) * CHUNK)), constant_values=NEG_BIG)
            for c in range(g)]
    return jnp.concatenate(rows, axis=1)


def cb_attention(proj, row0, batch, seq, keep, k_past, v_past, qg, kg, bias):
    assert seq % CHUNK == 0
    bias = _block_bias(bias, 2 if (seq // CHUNK) % 2 == 0 else 1)
    half_d = proj.shape[1] // 5
    heads = half_d // LANES
    blk0 = row0 // seq
    has_past = k_past is not None
    if not has_past:
        k_past = jnp.zeros((1, CB_PAST, half_d), F32)
        v_past = k_past

    def col(j):
        return pl.BlockSpec((seq, LANES), lambda b, h: (blk0 + b, j * heads + h))

    old = pl.BlockSpec((None, CB_PAST, LANES), lambda b, h: (b if has_past else 0, 0, h))
    gspec = pl.BlockSpec((1, LANES), lambda b, h: (0, 0))
    kept = pl.BlockSpec((None, keep, LANES), lambda b, h: (b, 0, h))
    return pl.pallas_call(
        functools.partial(_cb_kernel, has_past=has_past),
        grid=(batch, heads),
        in_specs=[col(2), col(3), col(4), old, old, gspec, gspec,
                  pl.BlockSpec((None,) + bias.shape[1:], lambda b, h: (h, 0, 0))],
        out_specs=[pl.BlockSpec((seq, LANES), lambda b, h: (b, h)), kept, kept],
        out_shape=[
            jax.ShapeDtypeStruct((batch * seq, half_d), BF16),
            jax.ShapeDtypeStruct((batch, keep, half_d), F32),
            jax.ShapeDtypeStruct((batch, keep, half_d), F32),
        ],
        scratch_shapes=[pltpu.VMEM((CB_PAST + seq, LANES), BF16), pltpu.VMEM((CB_PAST + seq, LANES), BF16)],
        compiler_params=pltpu.CompilerParams(dimension_semantics=("parallel", "parallel")),
        name="cb_attention",
    )(proj, proj, proj, k_past, v_past, qg.reshape(1, LANES), kg.reshape(1, LANES), bias)


def _rope_tables(pos0, seq, half):
    inv = ROPE_BASE ** (-jnp.arange(half, dtype=F32) / half)
    ang = (pos0 + jnp.arange(seq)).astype(F32)[:, None] * inv[None, :]
    return jnp.cos(ang), jnp.sin(ang)


def _band_bias(table):
    lo = -(CHUNK - 1)
    n_flat = CB_PAST + CHUNK - 1 - CB_MAX_REL
    ext = jnp.concatenate([table[:, lo + CB_MAX_REL:], jnp.repeat(table[:, -1:], n_flat, axis=1)], axis=1)
    rev = ext[:, ::-1]
    n = rev.shape[1]
    skew = jnp.tile(rev, (1, CHUNK + 1))[:, :CHUNK * (n + 1)].reshape(-1, CHUNK, n + 1)
    return skew[:, ::-1, :CB_BAND]


def kernel(x_prompt, x_sample, state_ret, cache_sb_k, cache_sb_v, state_conv, state_lru, cache_cb_k, cache_cb_v, e_norm_mix, e_w_in, e_ret_gn, e_sb_qn, e_sb_kn, e_w_out, e_norm_ffn, e_w1, e_w3, e_w2, o_norm_mix, o_w_in, o_conv_w, o_conv_b, o_lru_wr, o_lru_br, o_lru_wi, o_lru_bi, o_lru_lam, o_cb_qn, o_cb_kn, o_cb_bias, o_w_out, o_norm_ffn, o_router, o_w1, o_w3, o_w2):
    bp, lp, d = x_prompt.shape
    bs, ls, _ = x_sample.shape
    half_d = d // 2
    tp = bp * lp
    past = cache_sb_k.shape[2]
    dk = half_d // RET_HEADS
    x = (x_prompt.reshape(tp, d), x_sample.reshape(bs * ls, d))

    qkv = norm_matmul(x, e_norm_mix[0], e_w_in[0].astype(BF16))
    cos_p, sin_p = _rope_tables(0, lp, dk // 2)
    cos_s, sin_s = _rope_tables(past, ls, dk // 2)
    a_p, p_ret = retention_mixer(qkv, 0, bp, lp, cos_p, sin_p, e_ret_gn[0], None)
    a_s, s_ret = retention_mixer(qkv, tp, bs, ls, cos_s, sin_s, e_ret_gn[0], state_ret[0])
    qn, knb, vb, p_sb_k, s_sb_k, p_sb_v, s_sb_v = sb_norm(qkv, e_sb_qn[0], e_sb_kn[0], tp)
    b_p = sb_attention_prompt(qn, knb, vb, bp, lp)
    b_s = sb_attention_sample(qn, knb, vb, cache_sb_k[0], cache_sb_v[0], tp, bs, ls)
    x = out_proj(x, (a_p, a_s), (b_p, b_s), e_w_out[0].astype(BF16))
    ffn_w = (e_norm_ffn[0], e_w1[0].astype(BF16), e_w3[0].astype(BF16), e_w2[0].astype(BF16))
    x = ffn(x, *ffn_w)

    proj = norm_matmul(x, o_norm_mix[0], o_w_in[0].astype(BF16))
    lru_w = (o_conv_w[0], o_conv_b[0], o_lru_wr[0], o_lru_br[0], o_lru_wi[0], o_lru_bi[0], o_lru_lam[0])
    c_p, p_conv, p_lru = lru_mixer(proj, 0, bp, lp, None, None, *lru_w)
    c_s, s_conv, s_lru = lru_mixer(proj, tp, bs, ls, state_conv[0], state_lru[0], *lru_w)
    bias = _band_bias(o_cb_bias[0])
    keep_p = min(CB_PAST, lp)
    d_p, p_cb_k, p_cb_v = cb_attention(proj, 0, bp, lp, keep_p, None, None, o_cb_qn[0], o_cb_kn[0], bias)
    d_s, s_cb_k, s_cb_v = cb_attention(proj, tp, bs, ls, ls, cache_cb_k[0].reshape(bs, CB_PAST, half_d),
                                       cache_cb_v[0].reshape(bs, CB_PAST, half_d), o_cb_qn[0], o_cb_kn[0], bias)
    x = out_proj(x, (c_p, c_s), (d_p, d_s), o_w_out[0].astype(BF16))
    y_p, y_s = moe_layer(x, o_norm_ffn[0], o_router[0], o_w1[0], o_w3[0], o_w2[0], tp)

    sbh = (SB_HEADS, half_d // SB_HEADS)
    cbh = (CB_HEADS, half_d // CB_HEADS)
    return (
        y_p.reshape(bp, lp, d), y_s.reshape(bs, ls, d),
        p_ret[None], p_sb_k.reshape(1, bp, lp, *sbh), p_sb_v.reshape(1, bp, lp, *sbh),
        p_conv[None], p_lru.reshape(1, bp, half_d),
        p_cb_k.reshape(1, bp, keep_p, *cbh), p_cb_v.reshape(1, bp, keep_p, *cbh),
        s_ret[None], s_sb_k.reshape(1, bs, ls, *sbh), s_sb_v.reshape(1, bs, ls, *sbh),
        s_conv[None], s_lru.reshape(1, bs, half_d),
        s_cb_k.reshape(1, bs, ls, *cbh), s_cb_v.reshape(1, bs, ls, *cbh),
    )
```
